```python
import math
import jax, jax.numpy as jnp
from jax import lax
import numpy as np

D_MODEL = 1024
BATCH = 32
SEQ = 2048
DEPTH = 1

N_HEADS = 8
HEAD_DIM = 128
N_KV_GROUPS = 2
HEADS_PER_GROUP = N_HEADS // N_KV_GROUPS
KV_WIDTH = N_KV_GROUPS * HEAD_DIM
NSA_WIDTH = N_HEADS * HEAD_DIM
N_NSA_BRANCHES = 3
CMP_BLOCK = 32
CMP_STRIDE = 16
CMP_HIDDEN = 256
SEL_BLOCK = 64
N_SELECT = 8
WINDOW = 512
QUERY_BLOCK = 128
SEL_QUERY_CHUNK = 16
ROPE_THETA = 10000.0
LRU_WIDTH = D_MODEL
LRU_BLOCKS = 4
LRU_BLOCK_WIDTH = LRU_WIDTH // LRU_BLOCKS
CONV_WIDTH = 4
LRU_C = 8.0
N_BRANCHES = 2
N_EXPERTS = 32
TOP_K = 4
D_FF = D_MODEL
SWIGLU_LIMIT = 7.0
SWIGLU_ALPHA = 1.702
EXPERT_ROW_BLOCK = 256
RMS_EPS = 1e-6
NEG_INF = -1e30
IN_SPLITS = (NSA_WIDTH, 2 * N_NSA_BRANCHES * KV_WIDTH, N_HEADS * N_NSA_BRANCHES, LRU_WIDTH, LRU_WIDTH, N_BRANCHES * D_MODEL)
D_IN_TOTAL = sum(IN_SPLITS)

kernel_name = "hybrid_nsa_rglru_moe_adaln"


def rms_norm(x, g):
    xf = x.astype(jnp.float32)
    y = xf * lax.rsqrt(jnp.mean(xf * xf, axis=-1, keepdims=True) + RMS_EPS)
    return (y * g.astype(jnp.float32)).astype(x.dtype)


def rope_tables(positions):
    inv = ROPE_THETA ** (-jnp.arange(0, HEAD_DIM, 2, dtype=jnp.float32) / HEAD_DIM)
    ang = positions.astype(jnp.float32)[..., None] * inv
    return jnp.cos(ang), jnp.sin(ang)


def apply_rope(x, cos, sin):
    extra = x.ndim - 3
    shp = cos.shape[:2] + (1,) * extra + cos.shape[-1:]
    cos, sin = cos.reshape(shp), sin.reshape(shp)
    x1, x2 = jnp.split(x.astype(jnp.float32), 2, axis=-1)
    return jnp.concatenate([x1 * cos - x2 * sin, x2 * cos + x1 * sin], axis=-1).astype(x.dtype)


def masked_softmax(s, mask):
    s = jnp.where(mask, s.astype(jnp.float32), NEG_INF)
    return jnp.where(mask, jax.nn.softmax(s, axis=-1), 0.0)


def block_overlap(n_cmp, n_blk):
    cs = np.arange(n_cmp)[:, None] * CMP_STRIDE
    js = np.arange(n_blk)[None, :] * SEL_BLOCK
    m = (cs <= js + SEL_BLOCK - 1) & (cs + CMP_BLOCK - 1 >= js)
    return jnp.asarray(m.astype(np.float32))


def nsa_attention(q, k_cmp, v_cmp, k_slc, v_slc, k_win, v_win, gates, cos, sin,
                  q_norm_g, k_norm_g, cmp_pe_k, cmp_pe_v, cmp_wk1, cmp_wk2, cmp_wv1, cmp_wv2):
    B, S = q.shape[0], q.shape[1]
    scale = HEAD_DIM ** -0.5
    t_pos = jnp.arange(S)
    q = rms_norm(q, q_norm_g)
    q_rot = apply_rope(q, cos, sin)

    n_cmp = (S - CMP_BLOCK) // CMP_STRIDE + 1
    blk_idx = np.arange(n_cmp)[:, None] * CMP_STRIDE + np.arange(CMP_BLOCK)[None, :]

    def compress(t, pe, w1, w2):
        blocks = t[:, blk_idx] + pe[:, None, :]
        blocks = blocks.transpose(0, 1, 3, 2, 4).reshape(B, n_cmp, N_KV_GROUPS, CMP_BLOCK * HEAD_DIM)
        return jax.nn.gelu(blocks @ w1) @ w2

    kc = rms_norm(compress(k_cmp, cmp_pe_k, cmp_wk1, cmp_wk2), k_norm_g[0])
    vc = compress(v_cmp, cmp_pe_v, cmp_wv1, cmp_wv2)
    cmp_end = jnp.arange(n_cmp) * CMP_STRIDE + CMP_BLOCK - 1
    cmp_mask = cmp_end[None, :] <= t_pos[:, None]
    s_cmp = jnp.einsum('bsghd,bcgd->bghsc', q, kc) * scale
    p_cmp = masked_softmax(s_cmp, cmp_mask)
    o_cmp = jnp.einsum('bghsc,bcgd->bsghd', p_cmp.astype(vc.dtype), vc)

    n_blk = S // SEL_BLOCK
    n_sel = min(N_SELECT, n_blk)
    imp = jnp.einsum('bghsc,cj->bgsj', p_cmp, block_overlap(n_cmp, n_blk))
    blk = jnp.arange(n_blk)[None, :]
    cur = (t_pos // SEL_BLOCK)[:, None]
    causal_blk = blk <= cur
    forced = (blk == 0) | (blk == cur) | (blk == cur - 1)
    score = jnp.where(forced, 1e6, jnp.where(causal_blk, imp, -1e6))
    top_val, sel_idx = lax.top_k(score, n_sel)
    sel_ok = top_val > -1.0

    k_sel = rms_norm(k_slc, k_norm_g[1])
    k_sel = apply_rope(k_sel, cos, sin)
    k_blk = k_sel.reshape(B, n_blk, SEL_BLOCK, N_KV_GROUPS, HEAD_DIM).transpose(0, 3, 1, 2, 4)
    v_blk = v_slc.reshape(B, n_blk, SEL_BLOCK, N_KV_GROUPS, HEAD_DIM).transpose(0, 3, 1, 2, 4)
    b_ix = jnp.arange(B)[:, None, None]
    g_ix = jnp.arange(N_KV_GROUPS)[None, :, None]
    qc_len = SEL_QUERY_CHUNK

    def sel_chunk(s0):
        qc = lax.dynamic_slice_in_dim(q_rot, s0, qc_len, axis=1)
        ic = lax.dynamic_slice_in_dim(sel_idx, s0, qc_len, axis=2)
        okc = lax.dynamic_slice_in_dim(sel_ok, s0, qc_len, axis=2)
        flat = ic.reshape(B, N_KV_GROUPS, qc_len * n_sel)
        kg = k_blk[b_ix, g_ix, flat].reshape(B, N_KV_GROUPS, qc_len, n_sel, SEL_BLOCK, HEAD_DIM)
        vg = v_blk[b_ix, g_ix, flat].reshape(B, N_KV_GROUPS, qc_len, n_sel, SEL_BLOCK, HEAD_DIM)
        s = jnp.einsum('bqghd,bgqnkd->bghqnk', qc, kg) * scale
        kpos = ic[..., None] * SEL_BLOCK + jnp.arange(SEL_BLOCK)
        tq = s0 + jnp.arange(qc_len)
        mask = (kpos <= tq[:, None, None]) & okc[..., None]
        mask = mask[:, :, None].reshape(B, N_KV_GROUPS, 1, qc_len, n_sel * SEL_BLOCK)
        p = masked_softmax(s.reshape(B, N_KV_GROUPS, HEADS_PER_GROUP, qc_len, n_sel * SEL_BLOCK), mask)
        p = p.reshape(s.shape).astype(vg.dtype)
        return jnp.einsum('bghqnk,bgqnkd->bqghd', p, vg)

    o_slc = lax.map(sel_chunk, jnp.arange(S // qc_len) * qc_len)
    o_slc = jnp.moveaxis(o_slc, 0, 1).reshape(B, S, N_KV_GROUPS, HEADS_PER_GROUP, HEAD_DIM)

    k_w = apply_rope(rms_norm(k_win, k_norm_g[2]), cos, sin)
    k_pad = jnp.pad(k_w, ((0, 0), (WINDOW, 0), (0, 0), (0, 0)))
    v_pad = jnp.pad(v_win, ((0, 0), (WINDOW, 0), (0, 0), (0, 0)))
    span = WINDOW + QUERY_BLOCK

    def win_block(s0):
        qb = lax.dynamic_slice_in_dim(q_rot, s0, QUERY_BLOCK, axis=1)
        kb = lax.dynamic_slice_in_dim(k_pad, s0, span, axis=1)
        vb = lax.dynamic_slice_in_dim(v_pad, s0, span, axis=1)
        s = jnp.einsum('bqghd,bkgd->bghqk', qb, kb) * scale
        tq = (s0 + jnp.arange(QUERY_BLOCK))[:, None]
        kp = (s0 - WINDOW + jnp.arange(span))[None, :]
        mask = (kp <= tq) & (kp > tq - WINDOW) & (kp >= 0)
        p = masked_softmax(s, mask).astype(vb.dtype)
        return jnp.einsum('bghqk,bkgd->bqghd', p, vb)

    o_win = lax.map(win_block, jnp.arange(S // QUERY_BLOCK) * QUERY_BLOCK)
    o_win = jnp.moveaxis(o_win, 0, 1).reshape(B, S, N_KV_GROUPS, HEADS_PER_GROUP, HEAD_DIM)

    o = gates[..., 0:1] * o_cmp + gates[..., 1:2] * o_slc + gates[..., 2:3] * o_win
    return o.reshape(B, S, NSA_WIDTH)


def linear_combine(left, right):
    a_l, b_l = left
    a_r, b_r = right
    return a_l * a_r, a_r * b_l + b_r


def rg_lru_branch(xb, gb, conv_w, conv_b, lru_wa, lru_ba, lru_wx, lru_bx, lru_lambda):
    B, S = xb.shape[0], xb.shape[1]
    xc = lax.conv_general_dilated(xb, conv_w[:, None, :], window_strides=(1,),
                                  padding=[(CONV_WIDTH - 1, 0)],
                                  dimension_numbers=('NWC', 'WIO', 'NWC'),
                                  feature_group_count=LRU_WIDTH) + conv_b
    xblk = xc.reshape(B, S, LRU_BLOCKS, LRU_BLOCK_WIDTH)
    r = jax.nn.sigmoid(jnp.einsum('bshi,hij->bshj', xblk, lru_wa).reshape(B, S, LRU_WIDTH) + lru_ba)
    i = jax.nn.sigmoid(jnp.einsum('bshi,hij->bshj', xblk, lru_wx).reshape(B, S, LRU_WIDTH) + lru_bx)
    log_a = -LRU_C * r.astype(jnp.float32) * jax.nn.softplus(-lru_lambda.astype(jnp.float32))
    a = jnp.exp(log_a)
    mult = jnp.sqrt(-jnp.expm1(2.0 * log_a))
    u = mult * (i * xc).astype(jnp.float32)
    _, h = lax.associative_scan(linear_combine, (a, u), axis=1)
    return jax.nn.gelu(gb) * h.astype(xb.dtype)


def swiglu_expert(xb, w1, b1, w2, b2):
    hcat = xb @ w1 + b1
    x_glu = jnp.minimum(hcat[:, 0::2], SWIGLU_LIMIT)
    x_lin = jnp.clip(hcat[:, 1::2], -SWIGLU_LIMIT, SWIGLU_LIMIT)
    act = x_glu * jax.nn.sigmoid(SWIGLU_ALPHA * x_glu) * (x_lin + 1)
    return act @ w2 + b2


def moe_ffn(h, router_w, router_b, w1, b1, w2, b2):
    B, S, D = h.shape
    T = B * S
    hf = h.reshape(T, D)
    logits = (hf @ router_w + router_b).astype(jnp.float32)
    top_logit, top_e = lax.top_k(logits, TOP_K)
    top_w = jax.nn.softmax(top_logit, axis=-1)
    n_assign = T * TOP_K
    e_flat = top_e.reshape(-1)
    order = jnp.argsort(e_flat)
    e_sorted = e_flat[order]
    tok_sorted = order // TOP_K
    counts = jnp.zeros((N_EXPERTS,), jnp.int32).at[e_flat].add(1)
    padded = (counts + EXPERT_ROW_BLOCK - 1) // EXPERT_ROW_BLOCK * EXPERT_ROW_BLOCK
    pad_end = jnp.cumsum(padded)
    pad_start = pad_end - padded
    grp_start = jnp.cumsum(counts) - counts
    dest_sorted = pad_start[e_sorted] + jnp.arange(n_assign) - grp_start[e_sorted]
    n_rows = -(-(n_assign + N_EXPERTS * (EXPERT_ROW_BLOCK - 1)) // EXPERT_ROW_BLOCK) * EXPERT_ROW_BLOCK
    n_row_blocks = n_rows // EXPERT_ROW_BLOCK
    src_tok = jnp.zeros((n_rows,), jnp.int32).at[dest_sorted].set(tok_sorted)
    used = jnp.zeros((n_rows,), jnp.bool_).at[dest_sorted].set(True)
    buf = jnp.where(used[:, None], hf[src_tok], 0)
    blk_e = jnp.minimum(jnp.searchsorted(pad_end, jnp.arange(n_row_blocks) * EXPERT_ROW_BLOCK, side='right'),
                        N_EXPERTS - 1)

    def expert_block(args):
        xb, e = args
        return swiglu_expert(xb, w1[e], b1[e], w2[e], b2[e])

    ybuf = lax.map(expert_block, (buf.reshape(n_row_blocks, EXPERT_ROW_BLOCK, D), blk_e)).reshape(n_rows, D)
    dest = jnp.zeros((n_assign,), jnp.int32).at[order].set(dest_sorted).reshape(T, TOP_K)
    out = jnp.einsum('tk,tkd->td', top_w.astype(h.dtype), ybuf[dest])
    return out.reshape(B, S, D)


def hybrid_layer(x, c, cos, sin, ada_w, ada_b, norm1_g, w_in, q_norm_g, k_norm_g,
                 cmp_pe_k, cmp_pe_v, cmp_wk1, cmp_wk2, cmp_wv1, cmp_wv2,
                 conv_w, conv_b, lru_wa, lru_ba, lru_wx, lru_bx, lru_lambda,
                 w_branch_out, w_out, norm2_g, router_w, router_b, moe_w1, moe_b1, moe_w2, moe_b2):
    B, S, D = x.shape
    mod = jnp.einsum('bd,de->be', jax.nn.silu(c), ada_w) + ada_b
    shift1, scale1, gate1, shift2, scale2, gate2 = [m[:, None, :] for m in jnp.split(mod, 6, axis=-1)]

    h = rms_norm(x, norm1_g) * (1 + scale1) + shift1
    proj = h @ w_in
    split_points = np.cumsum(IN_SPLITS)[:-1].tolist()
    q, kv, nsa_g, lru_x, lru_g, merge_g = jnp.split(proj, split_points, axis=-1)
    q = q.reshape(B, S, N_KV_GROUPS, HEADS_PER_GROUP, HEAD_DIM)
    k_cmp, v_cmp, k_slc, v_slc, k_win, v_win = [t.reshape(B, S, N_KV_GROUPS, HEAD_DIM)
                                                 for t in jnp.split(kv, 2 * N_NSA_BRANCHES, axis=-1)]
    nsa_gates = jax.nn.sigmoid(nsa_g.reshape(B, S, N_KV_GROUPS, HEADS_PER_GROUP, N_NSA_BRANCHES))
    o_nsa = nsa_attention(q, k_cmp, v_cmp, k_slc, v_slc, k_win, v_win, nsa_gates, cos, sin,
                          q_norm_g, k_norm_g, cmp_pe_k, cmp_pe_v, cmp_wk1, cmp_wk2, cmp_wv1, cmp_wv2)
    o_lru = rg_lru_branch(lru_x, lru_g, conv_w, conv_b, lru_wa, lru_ba, lru_wx, lru_bx, lru_lambda)
    y_nsa = o_nsa @ w_branch_out[:NSA_WIDTH]
    y_lru = o_lru @ w_branch_out[NSA_WIDTH:]
    g = jax.nn.sigmoid(merge_g).reshape(B, S, N_BRANCHES, D)
    merged = g[:, :, 0] * y_nsa + g[:, :, 1] * y_lru
    x = x + gate1 * (merged @ w_out)

    h2 = rms_norm(x, norm2_g) * (1 + scale2) + shift2
    x = x + gate2 * moe_ffn(h2, router_w, router_b, moe_w1, moe_b1, moe_w2, moe_b2)
    return x


def setup_inputs(seed: int = 0) -> dict:
    key = jax.random.key(seed)
    ks = iter(jax.random.split(key, 40))
    L = DEPTH

    def nrm(shape, scale):
        return jax.random.normal(next(ks), shape, jnp.float32) * scale

    def gain(shape):
        return 1.0 + nrm(shape, 0.02)

    x = nrm((BATCH, SEQ, D_MODEL), 1.0)
    c = nrm((BATCH, D_MODEL), 1.0)
    start = jax.random.randint(next(ks), (BATCH, 1), 0, 4096, dtype=jnp.int32)
    positions = start + jnp.arange(SEQ, dtype=jnp.int32)[None, :]
    u = jax.random.uniform(next(ks), (L, LRU_WIDTH), jnp.float32, minval=0.9, maxval=0.999)
    p = u ** (1.0 / LRU_C)
    lru_lambda = jnp.log(p) - jnp.log1p(-p)
    return {
        "x": x,
        "c": c,
        "positions": positions,
        "ada_w": nrm((L, D_MODEL, 6 * D_MODEL), 0.5 * D_MODEL ** -0.5),
        "ada_b": nrm((L, 6 * D_MODEL), 0.02),
        "norm1_g": gain((L, D_MODEL)),
        "w_in": nrm((L, D_MODEL, D_IN_TOTAL), D_MODEL ** -0.5),
        "q_norm_g": gain((L, HEAD_DIM)),
        "k_norm_g": gain((L, N_NSA_BRANCHES, HEAD_DIM)),
        "cmp_pe_k": nrm((L, CMP_BLOCK, HEAD_DIM), 0.2),
        "cmp_pe_v": nrm((L, CMP_BLOCK, HEAD_DIM), 0.2),
        "cmp_wk1": nrm((L, CMP_BLOCK * HEAD_DIM, CMP_HIDDEN), (CMP_BLOCK * HEAD_DIM) ** -0.5),
        "cmp_wk2": nrm((L, CMP_HIDDEN, HEAD_DIM), CMP_HIDDEN ** -0.5),
        "cmp_wv1": nrm((L, CMP_BLOCK * HEAD_DIM, CMP_HIDDEN), (CMP_BLOCK * HEAD_DIM) ** -0.5),
        "cmp_wv2": nrm((L, CMP_HIDDEN, HEAD_DIM), CMP_HIDDEN ** -0.5),
        "conv_w": nrm((L, CONV_WIDTH, LRU_WIDTH), CONV_WIDTH ** -0.5),
        "conv_b": nrm((L, LRU_WIDTH), 0.02),
        "lru_wa": nrm((L, LRU_BLOCKS, LRU_BLOCK_WIDTH, LRU_BLOCK_WIDTH), LRU_BLOCK_WIDTH ** -0.5),
        "lru_ba": nrm((L, LRU_WIDTH), 0.02),
        "lru_wx": nrm((L, LRU_BLOCKS, LRU_BLOCK_WIDTH, LRU_BLOCK_WIDTH), LRU_BLOCK_WIDTH ** -0.5),
        "lru_bx": nrm((L, LRU_WIDTH), 0.02),
        "lru_lambda": lru_lambda,
        "w_branch_out": nrm((L, NSA_WIDTH + LRU_WIDTH, D_MODEL), NSA_WIDTH ** -0.5),
        "w_out": nrm((L, D_MODEL, D_MODEL), D_MODEL ** -0.5),
        "norm2_g": gain((L, D_MODEL)),
        "router_w": nrm((L, D_MODEL, N_EXPERTS), D_MODEL ** -0.5),
        "router_b": nrm((L, N_EXPERTS), 0.01),
        "moe_w1": nrm((L, N_EXPERTS, D_MODEL, 2 * D_FF), D_MODEL ** -0.5),
        "moe_b1": nrm((L, N_EXPERTS, 2 * D_FF), 0.02),
        "moe_w2": nrm((L, N_EXPERTS, D_FF, D_MODEL), D_FF ** -0.5),
        "moe_b2": nrm((L, N_EXPERTS, D_MODEL), 0.02),
    }


def reference(x, c, positions, ada_w, ada_b, norm1_g, w_in, q_norm_g, k_norm_g,
              cmp_pe_k, cmp_pe_v, cmp_wk1, cmp_wk2, cmp_wv1, cmp_wv2,
              conv_w, conv_b, lru_wa, lru_ba, lru_wx, lru_bx, lru_lambda,
              w_branch_out, w_out, norm2_g, router_w, router_b, moe_w1, moe_b1, moe_w2, moe_b2):
    cos, sin = rope_tables(positions)
    for l in range(DEPTH):
        x = hybrid_layer(x, c, cos, sin, ada_w[l], ada_b[l], norm1_g[l], w_in[l], q_norm_g[l], k_norm_g[l],
                         cmp_pe_k[l], cmp_pe_v[l], cmp_wk1[l], cmp_wk2[l], cmp_wv1[l], cmp_wv2[l],
                         conv_w[l], conv_b[l], lru_wa[l], lru_ba[l], lru_wx[l], lru_bx[l], lru_lambda[l],
                         w_branch_out[l], w_out[l], norm2_g[l], router_w[l], router_b[l],
                         moe_w1[l], moe_b1[l], moe_w2[l], moe_b2[l])
    return x
```

```python
import functools
import math

import jax
import jax.numpy as jnp
import numpy as np
from jax import lax
from jax.experimental import pallas as pl
from jax.experimental.pallas import tpu as pltpu

F32 = jnp.float32
BF16 = jnp.bfloat16

N_HEADS = 8
HEAD_DIM = 128
N_KV_GROUPS = 2
HEADS_PER_GROUP = N_HEADS // N_KV_GROUPS
N_NSA_BRANCHES = 3
CMP_BLOCK = 32
CMP_STRIDE = 16
CMP_HIDDEN = 256
SEL_BLOCK = 64
N_SELECT = 8
WINDOW = 512
ROPE_THETA = 10000.0
LRU_BLOCKS = 4
CONV_WIDTH = 4
LRU_C = 8.0
N_EXPERTS = 32
TOP_K = 4
SWIGLU_LIMIT = 7.0
SWIGLU_ALPHA = 1.702
RMS_EPS = 1e-6
NEG_INF = -1e30
LOWEST = -3.0e38

LANES = 128
SUBLANES = 8
VMEM_LIMIT = 56 * 1024 * 1024

Q_TILE = 128
KEY_TILE = 128
IN_ROWS = 512
LRU_ROWS = 256
EXPERT_ROWS = 512
COMBINE_ROWS = 256


def _sigmoid(v):
    return 1.0 / (1.0 + jnp.exp(-v))


def _gelu_tanh(v):
    return 0.5 * v * (1.0 + jnp.tanh(math.sqrt(2.0 / math.pi) * (v + 0.044715 * (v * v * v))))


def _rms(v, g):
    return v * lax.rsqrt(jnp.mean(v * v, axis=-1, keepdims=True) + RMS_EPS) * g


def _dot(a, b, **kw):
    return jnp.dot(a, b, preferred_element_type=F32, **kw)


def _dot_nt(a, b):
    return lax.dot_general(a, b, (((1,), (1,)), ((), ())), preferred_element_type=F32)


def _full(shape):
    nd = len(shape)
    return pl.BlockSpec(shape, lambda *_: (0,) * nd)


def _params(sem):
    return pltpu.CompilerParams(dimension_semantics=sem, vmem_limit_bytes=VMEM_LIMIT)


def _ada_kernel(c_ref, w_ref, b_ref, o_ref):
    c = c_ref[...]
    o_ref[...] = _dot(c * _sigmoid(c), w_ref[...], precision=lax.Precision.HIGHEST) + b_ref[...]


def _ada(c, ada_w, ada_b):
    B, D = c.shape
    N = ada_w.shape[1]
    return pl.pallas_call(
        _ada_kernel,
        grid=(N // D,),
        in_specs=[_full((B, D)), pl.BlockSpec((D, D), lambda j: (0, j)), pl.BlockSpec((1, D), lambda j: (0, j))],
        out_specs=pl.BlockSpec((B, D), lambda j: (0, j)),
        out_shape=jax.ShapeDtypeStruct((B, N), F32),
        compiler_params=_params(("arbitrary",)),
        name="ada",
    )(c, ada_w, ada_b.reshape(1, N))


def _in_kernel(x_ref, mod_ref, g1_ref, ang_ref, qg_ref, kg_ref, wq_ref, wkc_ref, wkr_ref, wg_ref, wlx_ref, wlg_ref,
               wm_ref, qn_ref, qr_ref, kvc_ref, kvr_ref, gt_ref, lx_ref, lg_ref, mg_ref):
    x = x_ref[...]
    shift1 = mod_ref[0:1, :]
    scale1 = mod_ref[1:2, :]
    h = _rms(x, g1_ref[...]) * (1.0 + scale1) + shift1
    hb = h.astype(BF16)

    ang = ang_ref[...]
    cos = jnp.cos(ang)
    sin = jnp.sin(ang)
    lane = lax.broadcasted_iota(jnp.int32, ang.shape, 1)
    sin_signed = jnp.where(lane < HEAD_DIM // 2, -sin, sin)

    def rope(v):
        return v * cos + pltpu.roll(v, HEAD_DIM // 2, 1) * sin_signed

    q = _dot(hb, wq_ref[...])
    for hh in range(N_HEADS):
        sl = slice(hh * HEAD_DIM, (hh + 1) * HEAD_DIM)
        qh = _rms(q[:, sl], qg_ref[...])
        qn_ref[:, sl] = qh.astype(BF16)
        qr_ref[:, sl] = rope(qh).astype(BF16)

    kvc_ref[...] = _dot(hb, wkc_ref[...]).astype(BF16)

    kvr = _dot(hb, wkr_ref[...])
    kvw = N_KV_GROUPS * HEAD_DIM
    for part in range(4):
        for gg in range(N_KV_GROUPS):
            sl = slice(part * kvw + gg * HEAD_DIM, part * kvw + (gg + 1) * HEAD_DIM)
            v = kvr[:, sl]
            if part % 2 == 0:
                row = 1 + part // 2
                v = rope(_rms(v, kg_ref[row:row + 1, :]))
            kvr_ref[:, sl] = v.astype(BF16)

    gt_ref[...] = _sigmoid(_dot(hb, wg_ref[...]))
    lx_ref[...] = _dot(hb, wlx_ref[...]).astype(BF16)
    lg_ref[...] = _dot(hb, wlg_ref[...]).astype(BF16)
    mg_ref[...] = _sigmoid(_dot(hb, wm_ref[...])).astype(BF16)


def _in_proj(x, mod3, norm1_g, ang, q_norm_g, k_norm_g, wq, wkc, wkr, wg, wlx, wlg, wm):
    B, S, D = x.shape
    tm = min(IN_ROWS, S)
    row = lambda w: pl.BlockSpec((None, tm, w), lambda b, s: (b, s, 0))
    widths = (wq.shape[1], wq.shape[1], wkc.shape[1], wkr.shape[1], wg.shape[1], wlx.shape[1], wlg.shape[1], wm.shape[1])
    dtypes = (BF16, BF16, BF16, BF16, F32, BF16, BF16, BF16)
    weights = (wq, wkc, wkr, wg, wlx, wlg, wm)
    return pl.pallas_call(
        _in_kernel,
        grid=(B, S // tm),
        in_specs=[row(D), pl.BlockSpec((None, 6, D), lambda b, s: (b, 0, 0)), _full((1, D)), row(HEAD_DIM),
                  _full((1, HEAD_DIM)), _full(k_norm_g.shape)] + [_full(w.shape) for w in weights],
        out_specs=[row(w) for w in widths],
        out_shape=[jax.ShapeDtypeStruct((B, S, w), dt) for w, dt in zip(widths, dtypes)],
        compiler_params=_params(("arbitrary", "arbitrary")),
        name="in_proj",
    )(x, mod3, norm1_g.reshape(1, D), ang, q_norm_g.reshape(1, HEAD_DIM), k_norm_g, *weights)


def _cmp_kernel(ak_ref, av_ref, pek_ref, pev_ref, wk1_ref, wk2_ref, wv1_ref, wv2_ref, kg_ref, kc_ref, vc_ref):
    half = CMP_STRIDE * HEAD_DIM

    def compress(a_ref, pe_ref, w1_ref, w2_ref):
        a = a_ref[...]
        u = _dot(a, w1_ref[0:half, :])
        v = _dot(a, w1_ref[half:2 * half, :])
        pw = _dot(pe_ref[...], w1_ref[...])
        pre = u + pltpu.roll(v, v.shape[0] - 1, 0) + pw[0:1, :]
        return _dot(_gelu_tanh(pre).astype(BF16), w2_ref[...])

    kc_ref[...] = _rms(compress(ak_ref, pek_ref, wk1_ref, wk2_ref), kg_ref[0:1, :]).astype(BF16)
    vc_ref[...] = compress(av_ref, pev_ref, wv1_ref, wv2_ref).astype(BF16)


def _compress(a_k, a_v, pe_k, pe_v, wk1, wk2, wv1, wv2, k_norm_g):
    B, G, NC, W = a_k.shape
    blk = pl.BlockSpec((None, None, NC, W), lambda b, g: (b, g, 0, 0))
    out = pl.BlockSpec((None, None, NC, HEAD_DIM), lambda b, g: (b, g, 0, 0))
    consts = (pe_k, pe_v, wk1, wk2, wv1, wv2, k_norm_g)
    return pl.pallas_call(
        _cmp_kernel,
        grid=(B, G),
        in_specs=[blk, blk] + [_full(a.shape) for a in consts],
        out_specs=[out, out],
        out_shape=[jax.ShapeDtypeStruct((B, G, NC, HEAD_DIM), BF16)] * 2,
        compiler_params=_params(("arbitrary", "arbitrary")),
        name="compress",
    )(a_k, a_v, *consts)


def _nsa_kernel(qn_ref, qr_ref, kc_ref, vc_ref, ks_ref, vs_ref, kw_ref, vw_ref, gt_ref, ov_ref, ex_ref, o_ref,
                m_sc, l_sc, acc_sc):
    qi = pl.program_id(2)
    tq = qn_ref.shape[0]
    hpg = HEADS_PER_GROUP
    scale = HEAD_DIM ** -0.5

    def stack(ref):
        return jnp.concatenate([ref[:, h * HEAD_DIM:(h + 1) * HEAD_DIM] for h in range(hpg)], axis=0)

    def rep(v):
        return jnp.concatenate([v] * hpg, axis=0)

    qn = stack(qn_ref)
    qr = stack(qr_ref)
    lane = lax.broadcasted_iota(jnp.int32, (tq, LANES), 1)
    lane_f = lane.astype(F32)
    t_q = qi * tq + lax.broadcasted_iota(jnp.int32, (tq, LANES), 0)
    lane4 = lax.broadcasted_iota(jnp.int32, (hpg * tq, LANES), 1)
    t4 = rep(t_q)

    cmask = lane4 * CMP_STRIDE + (CMP_BLOCK - 1) <= t4
    s = jnp.where(cmask, _dot_nt(qn, kc_ref[...]) * scale, NEG_INF)
    e = jnp.exp(s - jnp.max(s, axis=-1, keepdims=True))
    p = jnp.where(cmask, e / jnp.sum(e, axis=-1, keepdims=True), 0.0)
    o_cmp = _dot(p.astype(BF16), vc_ref[...])

    psum = p[0:tq]
    for h in range(1, hpg):
        psum = psum + p[h * tq:(h + 1) * tq]
    imp = _dot(psum, ov_ref[...], precision=lax.Precision.HIGHEST)
    n_blk = ks_ref.shape[0] // SEL_BLOCK
    cur = t_q // SEL_BLOCK
    forced = (lane == 0) | (lane == cur) | (lane == cur - 1)
    score = jnp.where(forced, 1e6, jnp.where(lane <= cur, imp, -1e6))
    score = jnp.where(lane < n_blk, score, LOWEST)
    sel = jnp.zeros((tq, LANES), F32)
    for _ in range(min(N_SELECT, n_blk)):
        best = jnp.max(score, axis=-1, keepdims=True)
        idx = jnp.min(jnp.where(score == best, lane_f, float(LANES)), axis=-1, keepdims=True)
        hit = lane_f == idx
        sel = jnp.where(hit, 1.0, sel)
        score = jnp.where(hit, LOWEST, score)
    sel_b = sel.astype(BF16)

    def attend(k_ref, v_ref, lo, hi, valid_fn):
        m_sc[...] = jnp.full(m_sc.shape, NEG_INF, F32)
        l_sc[...] = jnp.zeros(l_sc.shape, F32)
        acc_sc[...] = jnp.zeros(acc_sc.shape, F32)

        def body(kt, carry):
            start = pl.multiple_of(kt * KEY_TILE, KEY_TILE)
            k = k_ref[pl.ds(start, KEY_TILE), :]
            v = v_ref[pl.ds(start, KEY_TILE), :]
            valid = valid_fn(kt, kt * KEY_TILE + lane4)
            sc = jnp.where(valid, _dot_nt(qr, k) * scale, NEG_INF)
            m_old = m_sc[...]
            m_new = jnp.maximum(m_old, jnp.max(sc, axis=-1, keepdims=True))
            alpha = jnp.exp(m_old - m_new)
            pr = jnp.exp(sc - m_new)
            l_sc[...] = alpha * l_sc[...] + jnp.sum(pr, axis=-1, keepdims=True)
            acc_sc[...] = alpha * acc_sc[...] + _dot(pr.astype(BF16), v)
            m_sc[...] = m_new
            return carry

        lax.fori_loop(lo, hi, body, 0)
        return acc_sc[...] / l_sc[...]

    def slc_valid(kt, kpos):
        in_sel = rep(_dot(sel_b, ex_ref[kt]))
        return (in_sel > 0.5) & (kpos <= t4)

    o_slc = attend(ks_ref, vs_ref, 0, qi + 1, slc_valid)

    def win_valid(kt, kpos):
        return (kpos <= t4) & (kpos > t4 - WINDOW)

    o_win = attend(kw_ref, vw_ref, jnp.maximum(qi - WINDOW // KEY_TILE, 0), qi + 1, win_valid)

    gt = gt_ref[...]
    for h in range(hpg):
        rows = slice(h * tq, (h + 1) * tq)
        c0 = h * N_NSA_BRANCHES
        o = gt[:, c0:c0 + 1] * o_cmp[rows] + gt[:, c0 + 1:c0 + 2] * o_slc[rows] + gt[:, c0 + 2:c0 + 3] * o_win[rows]
        o_ref[:, h * HEAD_DIM:(h + 1) * HEAD_DIM] = o.astype(BF16)


def _nsa(qn, qr, kc, vc, kvr, gates, overlap, expand):
    B, S, _ = qn.shape
    G = N_KV_GROUPS
    tq = Q_TILE
    gw = HEADS_PER_GROUP * HEAD_DIM
    qspec = pl.BlockSpec((None, tq, gw), lambda b, g, i: (b, i, g))
    cspec = pl.BlockSpec((None, None) + kc.shape[2:], lambda b, g, i: (b, g, 0, 0))
    kv = lambda part: pl.BlockSpec((None, S, HEAD_DIM), lambda b, g, i: (b, 0, part * G + g))
    m = HEADS_PER_GROUP * tq
    return pl.pallas_call(
        _nsa_kernel,
        grid=(B, G, S // tq),
        in_specs=[qspec, qspec, cspec, cspec, kv(0), kv(1), kv(2), kv(3),
                  pl.BlockSpec((None, tq, LANES), lambda b, g, i: (b, i, g)), _full(overlap.shape), _full(expand.shape)],
        out_specs=qspec,
        out_shape=jax.ShapeDtypeStruct(qn.shape, BF16),
        scratch_shapes=[pltpu.VMEM((m, 1), F32), pltpu.VMEM((m, 1), F32), pltpu.VMEM((m, HEAD_DIM), F32)],
        compiler_params=_params(("arbitrary", "arbitrary", "arbitrary")),
        name="nsa",
    )(qn, qr, kc, vc, kvr, kvr, kvr, kvr, gates, overlap, expand)


def _lru_kernel(lx_ref, lg_ref, cw_ref, cb_ref, wa_ref, ba_ref, wx_ref, bx_ref, lam_ref, o_ref,
                xs_sc, a_sc, u_sc, h_sc, carry_sc):
    ts = lx_ref.shape[0]
    width = lx_ref.shape[1]
    bw = width // LRU_BLOCKS

    @pl.when(pl.program_id(1) == 0)
    def _():
        xs_sc[0:SUBLANES, :] = jnp.zeros((SUBLANES, width), F32)
        carry_sc[...] = jnp.zeros(carry_sc.shape, F32)

    xs_sc[SUBLANES:SUBLANES + ts, :] = lx_ref[...].astype(F32)
    xc = cb_ref[...] + cw_ref[CONV_WIDTH - 1:CONV_WIDTH, :] * xs_sc[SUBLANES:SUBLANES + ts, :]
    for d in range(1, CONV_WIDTH):
        w = cw_ref[CONV_WIDTH - 1 - d:CONV_WIDTH - d, :]
        xc = xc + w * xs_sc[SUBLANES - d:SUBLANES - d + ts, :]
    xs_sc[0:SUBLANES, :] = xs_sc[ts:ts + SUBLANES, :]

    lam = -lam_ref[...]
    neg_c_softplus = -LRU_C * (jnp.maximum(lam, 0.0) + jnp.log1p(jnp.exp(-jnp.abs(lam))))
    xcb = xc.astype(BF16)
    for blk in range(LRU_BLOCKS):
        sl = slice(blk * bw, (blk + 1) * bw)
        r = _sigmoid(_dot(xcb[:, sl], wa_ref[blk]) + ba_ref[:, sl])
        i = _sigmoid(_dot(xcb[:, sl], wx_ref[blk]) + bx_ref[:, sl])
        log_a = r * neg_c_softplus[:, sl]
        a = jnp.exp(log_a)
        a_sc[:, sl] = a
        u_sc[:, sl] = jnp.sqrt(-jnp.tanh(log_a) * (a * a + 1.0)) * (i * xc[:, sl])

    row = lax.broadcasted_iota(jnp.int32, (SUBLANES, width), 0)

    def chunk(c, h_prev):
        r0 = pl.multiple_of(c * SUBLANES, SUBLANES)
        a = a_sc[pl.ds(r0, SUBLANES), :]
        u = u_sc[pl.ds(r0, SUBLANES), :]
        for d in (1, 2, 4):
            keep = row >= d
            u = jnp.where(keep, a * pltpu.roll(u, d, 0) + u, u)
            a = jnp.where(keep, a * pltpu.roll(a, d, 0), a)
        h = a * h_prev + u
        h_sc[pl.ds(r0, SUBLANES), :] = h
        return jnp.broadcast_to(h[SUBLANES - 1:SUBLANES, :], h.shape)

    carry_sc[...] = lax.fori_loop(0, ts // SUBLANES, chunk, carry_sc[...])
    o_ref[...] = (_gelu_tanh(lg_ref[...].astype(F32)) * h_sc[...]).astype(BF16)


def _lru(lx, lg, conv_w, conv_b, wa, ba, wx, bx, lam):
    B, S, W = lx.shape
    ts = min(LRU_ROWS, S)
    row = pl.BlockSpec((None, ts, W), lambda b, s: (b, s, 0))
    consts = (conv_w, conv_b.reshape(1, W), wa, ba.reshape(1, W), wx, bx.reshape(1, W), lam.reshape(1, W))
    return pl.pallas_call(
        _lru_kernel,
        grid=(B, S // ts),
        in_specs=[row, row] + [_full(a.shape) for a in consts],
        out_specs=row,
        out_shape=jax.ShapeDtypeStruct((B, S, W), BF16),
        scratch_shapes=[pltpu.VMEM((ts + SUBLANES, W), F32), pltpu.VMEM((ts, W), F32), pltpu.VMEM((ts, W), F32),
                        pltpu.VMEM((ts, W), F32), pltpu.VMEM((SUBLANES, W), F32)],
        compiler_params=_params(("arbitrary", "arbitrary")),
        name="lru",
    )(lx, lg, *consts)


def _merge_kernel(on_ref, ol_ref, mg_ref, x_ref, mod_ref, g2_ref, wn_ref, wl_ref, wo_ref, rw_ref, rb_ref, tri_ref,
                  x1_ref, h2_ref, rt_ref, cnt_ref, carry_sc):
    first = (pl.program_id(0) == 0) & (pl.program_id(1) == 0)

    @pl.when(first)
    def _():
        carry_sc[...] = jnp.zeros(carry_sc.shape, F32)

    d = x_ref.shape[1]
    y_nsa = _dot(on_ref[...], wn_ref[...])
    y_lru = _dot(ol_ref[...], wl_ref[...])
    merged = mg_ref[:, 0:d].astype(F32) * y_nsa + mg_ref[:, d:2 * d].astype(F32) * y_lru
    gate1 = mod_ref[2:3, :]
    shift2 = mod_ref[3:4, :]
    scale2 = mod_ref[4:5, :]
    x1 = x_ref[...] + gate1 * _dot(merged.astype(BF16), wo_ref[...])
    x1_ref[...] = x1
    h2 = _rms(x1, g2_ref[...]) * (1.0 + scale2) + shift2
    h2_ref[...] = h2.astype(BF16)

    logits = _dot(h2, rw_ref[...], precision=lax.Precision.HIGHEST) + rb_ref[...]
    lane = lax.broadcasted_iota(jnp.int32, logits.shape, 1)
    lane_f = lane.astype(F32)
    score = jnp.where(lane < N_EXPERTS, logits, LOWEST)
    picks = []
    onehot = jnp.zeros(logits.shape, F32)
    for _ in range(TOP_K):
        best = jnp.max(score, axis=-1, keepdims=True)
        idx = jnp.min(jnp.where(score == best, lane_f, float(LANES)), axis=-1, keepdims=True)
        hit = lane_f == idx
        picks.append((idx, best, hit))
        onehot = jnp.where(hit, 1.0, onehot)
        score = jnp.where(hit, LOWEST, score)
    ew = [jnp.exp(v - picks[0][1]) for _, v, _ in picks]
    den = ew[0]
    for v in ew[1:]:
        den = den + v

    before = _dot(tri_ref[...], onehot.astype(BF16)) + carry_sc[0:1, :]
    carry_sc[...] = carry_sc[...] + jnp.sum(onehot, axis=0, keepdims=True)
    cnt_ref[...] = carry_sc[...]

    out = jnp.zeros(logits.shape, F32)
    for k, (idx, _, hit) in enumerate(picks):
        rank = jnp.sum(jnp.where(hit, before, 0.0), axis=-1, keepdims=True)
        out = jnp.where(lane == k, idx, out)
        out = jnp.where(lane == TOP_K + k, rank, out)
        out = jnp.where(lane == 2 * TOP_K + k, ew[k] / den, out)
    rt_ref[...] = out


def _merge(o_nsa, o_lru, mg, x, mod3, norm2_g, wn, wl, wo, rw, rb, tri):
    B, S, D = x.shape
    tm = tri.shape[0]
    row = lambda w: pl.BlockSpec((None, tm, w), lambda b, s: (b, s, 0))
    consts = (norm2_g.reshape(1, D), wn, wl, wo, rw, rb, tri)
    return pl.pallas_call(
        _merge_kernel,
        grid=(B, S // tm),
        in_specs=[row(D), row(D), row(2 * D), row(D), pl.BlockSpec((None, 6, D), lambda b, s: (b, 0, 0))]
                 + [_full(a.shape) for a in consts],
        out_specs=[row(D), row(D), row(LANES), _full((SUBLANES, LANES))],
        out_shape=[jax.ShapeDtypeStruct((B, S, D), F32), jax.ShapeDtypeStruct((B, S, D), BF16),
                   jax.ShapeDtypeStruct((B, S, LANES), F32), jax.ShapeDtypeStruct((SUBLANES, LANES), F32)],
        scratch_shapes=[pltpu.VMEM((SUBLANES, LANES), F32)],
        compiler_params=_params(("arbitrary", "arbitrary")),
        name="merge",
    )(o_nsa, o_lru, mg, x, mod3, *consts)


def _expert_kernel(be_ref, nb_ref, x_ref, wg_ref, bg_ref, wl_ref, bl_ref, w2_ref, b2_ref, y_ref):
    i = pl.program_id(0)

    @pl.when(i < nb_ref[0])
    def _():
        xb = x_ref[...]
        x_glu = jnp.minimum(_dot(xb, wg_ref[...]) + bg_ref[...], SWIGLU_LIMIT)
        x_lin = jnp.clip(_dot(xb, wl_ref[...]) + bl_ref[...], -SWIGLU_LIMIT, SWIGLU_LIMIT)
        act = x_glu * _sigmoid(SWIGLU_ALPHA * x_glu) * (x_lin + 1.0)
        y_ref[...] = (_dot(act.astype(BF16), w2_ref[...]) + b2_ref[...]).astype(BF16)

    @pl.when(i >= nb_ref[0])
    def _():
        y_ref[...] = jnp.zeros(y_ref.shape, BF16)


def _experts(blk_e, n_used, buf, w_glu, b_glu, w_lin, b_lin, w2, b2):
    n_rows, D = buf.shape
    F = w_glu.shape[2]
    tm = EXPERT_ROWS
    wspec = lambda a, b: pl.BlockSpec((None, a, b), lambda i, be, nb: (be[i], 0, 0))
    return pl.pallas_call(
        _expert_kernel,
        grid_spec=pltpu.PrefetchScalarGridSpec(
            num_scalar_prefetch=2,
            grid=(n_rows // tm,),
            in_specs=[pl.BlockSpec((tm, D), lambda i, be, nb: (i, 0)), wspec(D, F), wspec(1, F), wspec(D, F),
                      wspec(1, F), wspec(F, D), wspec(1, D)],
            out_specs=pl.BlockSpec((tm, D), lambda i, be, nb: (i, 0)),
        ),
        out_shape=jax.ShapeDtypeStruct((n_rows, D), BF16),
        compiler_params=_params(("arbitrary",)),
        name="experts",
    )(blk_e, n_used, buf, w_glu, b_glu, w_lin, b_lin, w2, b2)


def _combine_kernel(x1_ref, yg_ref, rt_ref, mod_ref, o_ref):
    gate2 = mod_ref[5:6, :]
    rt = rt_ref[...]
    acc = rt[:, 2 * TOP_K:2 * TOP_K + 1] * yg_ref[0].astype(F32)
    for k in range(1, TOP_K):
        acc = acc + rt[:, 2 * TOP_K + k:2 * TOP_K + k + 1] * yg_ref[k].astype(F32)
    o_ref[...] = x1_ref[...] + gate2 * acc


def _combine(x1, yg, rt, mod3):
    B, S, D = x1.shape
    tm = min(COMBINE_ROWS, S)
    row = lambda w: pl.BlockSpec((None, tm, w), lambda b, s: (b, s, 0))
    return pl.pallas_call(
        _combine_kernel,
        grid=(B, S // tm),
        in_specs=[row(D), pl.BlockSpec((TOP_K, None, tm, D), lambda b, s: (0, b, s, 0)), row(LANES),
                  pl.BlockSpec((None, 6, D), lambda b, s: (b, 0, 0))],
        out_specs=row(D),
        out_shape=jax.ShapeDtypeStruct((B, S, D), F32),
        compiler_params=_params(("arbitrary", "arbitrary")),
        name="combine",
    )(x1, yg, rt, mod3)


def _overlap_matrix(n_cmp_pad, n_blk):
    cs = np.arange(n_cmp_pad)[:, None] * CMP_STRIDE
    js = np.arange(LANES)[None, :] * SEL_BLOCK
    m = (cs <= js + SEL_BLOCK - 1) & (cs + CMP_BLOCK - 1 >= js) & (np.arange(LANES)[None, :] < n_blk)
    return jnp.asarray(m.astype(np.float32))


def _expand_matrix(seq):
    kt = np.arange(seq // KEY_TILE)[:, None, None]
    j = np.arange(LANES)[None, :, None]
    c = np.arange(KEY_TILE)[None, None, :]
    return jnp.asarray((j == (kt * KEY_TILE + c) // SEL_BLOCK).astype(np.float32), dtype=BF16)


def _layer(x, c, ang, ada_w, ada_b, norm1_g, w_in, q_norm_g, k_norm_g, cmp_pe_k, cmp_pe_v, cmp_wk1, cmp_wk2, cmp_wv1,
           cmp_wv2, conv_w, conv_b, lru_wa, lru_ba, lru_wx, lru_bx, lru_lambda, w_branch_out, w_out, norm2_g,
           router_w, router_b, moe_w1, moe_b1, moe_w2, moe_b2):
    B, S, D = x.shape
    T = B * S
    G = N_KV_GROUPS
    nsa_w = N_HEADS * HEAD_DIM
    kv_w = G * HEAD_DIM

    mod3 = _ada(c, ada_w, ada_b).reshape(B, 6, D)

    o = 0
    wq = w_in[:, o:o + nsa_w]; o += nsa_w
    wkc = w_in[:, o:o + 2 * kv_w]; o += 2 * kv_w
    wkr = w_in[:, o:o + 4 * kv_w]; o += 4 * kv_w
    n_gate = N_HEADS * N_NSA_BRANCHES
    wg_raw = w_in[:, o:o + n_gate].reshape(D, G, n_gate // G); o += n_gate
    wg = jnp.pad(wg_raw, ((0, 0), (0, 0), (0, LANES - n_gate // G))).reshape(D, G * LANES)
    wlx = w_in[:, o:o + D]; o += D
    wlg = w_in[:, o:o + D]; o += D
    wm = w_in[:, o:o + 2 * D]
    bf = lambda a: a.astype(BF16)

    qn, qr, kvc, kvr, gates, lx, lg, mg = _in_proj(x, mod3, norm1_g, ang, q_norm_g, k_norm_g, bf(wq), bf(wkc), bf(wkr),
                                                   bf(wg), bf(wlx), bf(wlg), bf(wm))

    nc = S // CMP_STRIDE
    a = kvc.reshape(B, nc, CMP_STRIDE, 2, G, HEAD_DIM).transpose(3, 0, 4, 1, 2, 5).reshape(2, B, G, nc, CMP_STRIDE * HEAD_DIM)
    pe = lambda p: jnp.broadcast_to(p.reshape(1, CMP_BLOCK * HEAD_DIM), (SUBLANES, CMP_BLOCK * HEAD_DIM)).astype(BF16)
    kc, vc = _compress(a[0], a[1], pe(cmp_pe_k), pe(cmp_pe_v), bf(cmp_wk1), bf(cmp_wk2), bf(cmp_wv1), bf(cmp_wv2),
                       k_norm_g)

    o_nsa = _nsa(qn, qr, kc, vc, kvr, gates, _overlap_matrix(nc, S // SEL_BLOCK), _expand_matrix(S))
    o_lru = _lru(lx, lg, conv_w, conv_b, bf(lru_wa), lru_ba, bf(lru_wx), lru_bx, lru_lambda)

    tm = min(IN_ROWS, S)
    tri = jnp.asarray(np.tril(np.ones((tm, tm), np.float32), -1), dtype=BF16)
    rw = jnp.pad(router_w, ((0, 0), (0, LANES - N_EXPERTS)))
    rb = jnp.pad(router_b, (0, LANES - N_EXPERTS)).reshape(1, LANES)
    x1, h2, rt, cnt = _merge(o_nsa, o_lru, mg, x, mod3, norm2_g, bf(w_branch_out[:nsa_w]), bf(w_branch_out[nsa_w:]),
                             bf(w_out), rw, rb, tri)

    rt2 = rt.reshape(T, LANES)
    top_e = rt2[:, 0:TOP_K].astype(jnp.int32)
    rank = rt2[:, TOP_K:2 * TOP_K].astype(jnp.int32)
    counts = cnt[0, :N_EXPERTS].astype(jnp.int32)
    padded = (counts + EXPERT_ROWS - 1) // EXPERT_ROWS * EXPERT_ROWS
    ends = jnp.cumsum(padded)
    start = ends - padded
    dest = start[top_e] + rank
    n_rows = -(-(T * TOP_K + N_EXPERTS * (EXPERT_ROWS - 1)) // EXPERT_ROWS) * EXPERT_ROWS
    n_blocks = n_rows // EXPERT_ROWS
    blk_e = jnp.minimum(jnp.searchsorted(ends, jnp.arange(n_blocks) * EXPERT_ROWS, side='right'),
                        N_EXPERTS - 1).astype(jnp.int32)
    n_used = (ends[-1:] // EXPERT_ROWS).astype(jnp.int32)

    h2f = h2.reshape(T, D)
    flat_dest = dest.reshape(-1)
    src_tok = jnp.zeros((n_rows,), jnp.int32).at[flat_dest].set(jnp.arange(T * TOP_K, dtype=jnp.int32) // TOP_K)
    used = jnp.zeros((n_rows,), jnp.bool_).at[flat_dest].set(True)
    buf = jnp.where(used[:, None], h2f[src_tok], jnp.zeros((), BF16))

    F = moe_w2.shape[1]
    ybuf = _experts(blk_e, n_used, buf, bf(moe_w1[:, :, 0::2]), moe_b1[:, 0::2].reshape(N_EXPERTS, 1, F),
                    bf(moe_w1[:, :, 1::2]), moe_b1[:, 1::2].reshape(N_EXPERTS, 1, F), bf(moe_w2),
                    moe_b2.reshape(N_EXPERTS, 1, D))
    yg = ybuf[dest.T].reshape(TOP_K, B, S, D)
    return _combine(x1, yg, rt, mod3)


def kernel(x, c, positions, ada_w, ada_b, norm1_g, w_in, q_norm_g, k_norm_g, cmp_pe_k, cmp_pe_v, cmp_wk1, cmp_wk2, cmp_wv1, cmp_wv2, conv_w, conv_b, lru_wa, lru_ba, lru_wx, lru_bx, lru_lambda, w_branch_out, w_out, norm2_g, router_w, router_b, moe_w1, moe_b1, moe_w2, moe_b2):
    inv = ROPE_THETA ** (-jnp.arange(0, HEAD_DIM, 2, dtype=F32) / HEAD_DIM)
    ang = positions.astype(F32)[..., None] * inv
    ang = jnp.concatenate([ang, ang], axis=-1)
    for l in range(ada_w.shape[0]):
        x = _layer(x, c, ang, ada_w[l], ada_b[l], norm1_g[l], w_in[l], q_norm_g[l], k_norm_g[l], cmp_pe_k[l],
                   cmp_pe_v[l], cmp_wk1[l], cmp_wk2[l], cmp_wv1[l], cmp_wv2[l], conv_w[l], conv_b[l], lru_wa[l],
                   lru_ba[l], lru_wx[l], lru_bx[l], lru_lambda[l], w_branch_out[l], w_out[l], norm2_g[l], router_w[l],
                   router_b[l], moe_w1[l], moe_b1[l], moe_w2[l], moe_b2[l])
    return x
```

```python
import functools
import math

import jax
import jax.numpy as jnp
import numpy as np
from jax import lax
from jax.experimental import pallas as pl
from jax.experimental.pallas import tpu as pltpu

F32 = jnp.float32
BF16 = jnp.bfloat16

N_HEADS = 8
HEAD_DIM = 128
N_KV_GROUPS = 2
HEADS_PER_GROUP = N_HEADS // N_KV_GROUPS
N_NSA_BRANCHES = 3
CMP_BLOCK = 32
CMP_STRIDE = 16
CMP_HIDDEN = 256
SEL_BLOCK = 64
N_SELECT = 8
WINDOW = 512
ROPE_THETA = 10000.0
LRU_BLOCKS = 4
CONV_WIDTH = 4
LRU_C = 8.0
N_EXPERTS = 32
TOP_K = 4
SWIGLU_LIMIT = 7.0
SWIGLU_ALPHA = 1.702
RMS_EPS = 1e-6
NEG_INF = -1e30
LOWEST = -3.0e38

LANES = 128
SUBLANES = 8
VMEM_LIMIT = 56 * 1024 * 1024

Q_TILE = 128
KEY_TILE = 128
SLC_TILE = 512
IN_ROWS = 512
LRU_ROWS = 256
EXPERT_ROWS = 512
COMBINE_ROWS = 256


def _sigmoid(v):
    return 1.0 / (1.0 + jnp.exp(-v))


def _gelu_tanh(v):
    return 0.5 * v * (1.0 + jnp.tanh(math.sqrt(2.0 / math.pi) * (v + 0.044715 * (v * v * v))))


def _rms(v, g):
    return v * lax.rsqrt(jnp.mean(v * v, axis=-1, keepdims=True) + RMS_EPS) * g


def _dot(a, b, **kw):
    return jnp.dot(a, b, preferred_element_type=F32, **kw)


def _dot_nt(a, b):
    return lax.dot_general(a, b, (((1,), (1,)), ((), ())), preferred_element_type=F32)


def _full(shape):
    nd = len(shape)
    return pl.BlockSpec(shape, lambda *_: (0,) * nd)


def _params(sem):
    return pltpu.CompilerParams(dimension_semantics=sem, vmem_limit_bytes=VMEM_LIMIT)


def _ada_kernel(c_ref, w_ref, b_ref, o_ref):
    c = c_ref[...]
    o_ref[...] = _dot(c * _sigmoid(c), w_ref[...], precision=lax.Precision.HIGHEST) + b_ref[...]


def _ada(c, ada_w, ada_b):
    B, D = c.shape
    N = ada_w.shape[1]
    return pl.pallas_call(
        _ada_kernel,
        grid=(N // D,),
        in_specs=[_full((B, D)), pl.BlockSpec((D, D), lambda j: (0, j)), pl.BlockSpec((1, D), lambda j: (0, j))],
        out_specs=pl.BlockSpec((B, D), lambda j: (0, j)),
        out_shape=jax.ShapeDtypeStruct((B, N), F32),
        compiler_params=_params(("arbitrary",)),
        name="ada",
    )(c, ada_w, ada_b.reshape(1, N))


def _in_kernel(x_ref, mod_ref, g1_ref, ang_ref, qg_ref, kg_ref, wq_ref, wkc_ref, wkr_ref, wg_ref, wlx_ref, wlg_ref,
               wm_ref, qn_ref, qr_ref, kvc_ref, kvr_ref, gt_ref, lx_ref, lg_ref, mg_ref):
    x = x_ref[...]
    shift1 = mod_ref[0:1, :]
    scale1 = mod_ref[1:2, :]
    h = _rms(x, g1_ref[...]) * (1.0 + scale1) + shift1
    hb = h.astype(BF16)

    ang = ang_ref[...]
    cos = jnp.cos(ang)
    sin = jnp.sin(ang)
    lane = lax.broadcasted_iota(jnp.int32, ang.shape, 1)
    sin_signed = jnp.where(lane < HEAD_DIM // 2, -sin, sin)

    def rope(v):
        return v * cos + pltpu.roll(v, HEAD_DIM // 2, 1) * sin_signed

    q = _dot(hb, wq_ref[...])
    for hh in range(N_HEADS):
        sl = slice(hh * HEAD_DIM, (hh + 1) * HEAD_DIM)
        qh = _rms(q[:, sl], qg_ref[...])
        qn_ref[:, sl] = (qh * HEAD_DIM ** -0.5).astype(BF16)
        qr_ref[:, sl] = (rope(qh) * HEAD_DIM ** -0.5).astype(BF16)

    kvc_ref[...] = _dot(hb, wkc_ref[...]).astype(BF16)

    kvr = _dot(hb, wkr_ref[...])
    kvw = N_KV_GROUPS * HEAD_DIM
    for part in range(4):
        for gg in range(N_KV_GROUPS):
            sl = slice(part * kvw + gg * HEAD_DIM, part * kvw + (gg + 1) * HEAD_DIM)
            v = kvr[:, sl]
            if part % 2 == 0:
                row = 1 + part // 2
                v = rope(_rms(v, kg_ref[row:row + 1, :]))
            kvr_ref[:, sl] = v.astype(BF16)

    gt_ref[...] = _sigmoid(_dot(hb, wg_ref[...]))
    lx_ref[...] = _dot(hb, wlx_ref[...]).astype(BF16)
    lg_ref[...] = _dot(hb, wlg_ref[...]).astype(BF16)
    mg_ref[...] = _sigmoid(_dot(hb, wm_ref[...])).astype(BF16)


def _in_proj(x, mod3, norm1_g, ang, q_norm_g, k_norm_g, wq, wkc, wkr, wg, wlx, wlg, wm):
    B, S, D = x.shape
    tm = min(IN_ROWS, S)
    row = lambda w: pl.BlockSpec((None, tm, w), lambda b, s: (b, s, 0))
    widths = (wq.shape[1], wq.shape[1], wkc.shape[1], wkr.shape[1], wg.shape[1], wlx.shape[1], wlg.shape[1], wm.shape[1])
    dtypes = (BF16, BF16, BF16, BF16, F32, BF16, BF16, BF16)
    weights = (wq, wkc, wkr, wg, wlx, wlg, wm)
    return pl.pallas_call(
        _in_kernel,
        grid=(B, S // tm),
        in_specs=[row(D), pl.BlockSpec((None, 6, D), lambda b, s: (b, 0, 0)), _full((1, D)), row(HEAD_DIM),
                  _full((1, HEAD_DIM)), _full(k_norm_g.shape)] + [_full(w.shape) for w in weights],
        out_specs=[row(w) for w in widths],
        out_shape=[jax.ShapeDtypeStruct((B, S, w), dt) for w, dt in zip(widths, dtypes)],
        compiler_params=_params(("arbitrary", "arbitrary")),
        name="in_proj",
    )(x, mod3, norm1_g.reshape(1, D), ang, q_norm_g.reshape(1, HEAD_DIM), k_norm_g, *weights)


def _cmp_kernel(ak_ref, av_ref, pek_ref, pev_ref, wk1_ref, wk2_ref, wv1_ref, wv2_ref, kg_ref, kc_ref, vc_ref):
    half = CMP_STRIDE * HEAD_DIM

    def compress(a_ref, pe_ref, w1_ref, w2_ref):
        a = a_ref[...]
        u = _dot(a, w1_ref[0:half, :])
        v = _dot(a, w1_ref[half:2 * half, :])
        pw = _dot(pe_ref[...], w1_ref[...])
        pre = u + pltpu.roll(v, v.shape[0] - 1, 0) + pw[0:1, :]
        return _dot(_gelu_tanh(pre).astype(BF16), w2_ref[...])

    kc_ref[...] = _rms(compress(ak_ref, pek_ref, wk1_ref, wk2_ref), kg_ref[0:1, :]).astype(BF16)
    vc_ref[...] = compress(av_ref, pev_ref, wv1_ref, wv2_ref).T.astype(BF16)


def _compress(a_k, a_v, pe_k, pe_v, wk1, wk2, wv1, wv2, k_norm_g):
    B, G, NC, W = a_k.shape
    blk = pl.BlockSpec((None, None, NC, W), lambda b, g: (b, g, 0, 0))
    out = pl.BlockSpec((None, None, NC, HEAD_DIM), lambda b, g: (b, g, 0, 0))
    consts = (pe_k, pe_v, wk1, wk2, wv1, wv2, k_norm_g)
    return pl.pallas_call(
        _cmp_kernel,
        grid=(B, G),
        in_specs=[blk, blk] + [_full(a.shape) for a in consts],
        out_specs=[out, out],
        out_shape=[jax.ShapeDtypeStruct((B, G, NC, HEAD_DIM), BF16)] * 2,
        compiler_params=_params(("arbitrary", "arbitrary")),
        name="compress",
    )(a_k, a_v, *consts)


def _nsa_kernel(qn_ref, qr_ref, kc_ref, vc_ref, ks_ref, vs_ref, kw_ref, vw_ref, gt_ref, ov_ref, ex_ref, o_ref,
                vst_sc, vwt_sc, m_sc, l_sc, acc_sc):
    qi = pl.program_id(2)
    tq = qn_ref.shape[0]
    hpg = HEADS_PER_GROUP
    m_cols = hpg * tq
    seq = ks_ref.shape[0]
    n_blk = seq // SEL_BLOCK

    @pl.when(qi == 0)
    def _():
        for kt in range(seq // KEY_TILE):
            rows = slice(kt * KEY_TILE, (kt + 1) * KEY_TILE)
            vst_sc[:, rows] = vs_ref[rows, :].astype(F32).T.astype(BF16)
            vwt_sc[:, rows] = vw_ref[rows, :].astype(F32).T.astype(BF16)

    def stack(ref):
        return jnp.concatenate([ref[:, h * HEAD_DIM:(h + 1) * HEAD_DIM] for h in range(hpg)], axis=0)

    def rep(v):
        return jnp.concatenate([v] * hpg, axis=1)

    qn = stack(qn_ref)
    qr = stack(qr_ref)
    row = lax.broadcasted_iota(jnp.int32, (KEY_TILE, tq), 0)
    col = lax.broadcasted_iota(jnp.int32, (KEY_TILE, tq), 1)
    t_q = qi * tq + col

    cmask = rep(row * CMP_STRIDE + (CMP_BLOCK - 1) <= t_q)
    s = jnp.where(cmask, _dot_nt(kc_ref[...], qn), NEG_INF)
    e = jnp.exp(s - jnp.max(s, axis=0, keepdims=True))
    p = jnp.where(cmask, e * (1.0 / jnp.sum(e, axis=0, keepdims=True)), 0.0)
    o_cmp = _dot(vc_ref[...], p.astype(BF16))

    psum = p[:, 0:tq]
    for h in range(1, hpg):
        psum = psum + p[:, h * tq:(h + 1) * tq]
    imp = _dot(ov_ref[...], psum, precision=lax.Precision.HIGHEST)
    blk = row.astype(F32)
    cur = (t_q // SEL_BLOCK).astype(F32)
    forced = (blk == 0.0) | (blk == cur) | (blk == cur - 1.0)
    score = jnp.where(forced, 1e6, jnp.where(blk <= cur, imp, -1e6))
    score = jnp.where(row < n_blk, score, LOWEST)
    bias = jnp.full((KEY_TILE, tq), NEG_INF, F32)
    for _ in range(min(N_SELECT, n_blk)):
        best = jnp.max(score, axis=0, keepdims=True)
        idx = jnp.min(jnp.where(score == best, blk, float(LANES)), axis=0, keepdims=True)
        hit = blk == idx
        bias = jnp.where(hit, 0.0, bias)
        score = jnp.where(hit, LOWEST, score)
    q_aug = jnp.concatenate([qr, jnp.concatenate([bias.T.astype(BF16)] * hpg, axis=0)], axis=1)

    m_sc[...] = jnp.full(m_sc.shape, NEG_INF, F32)
    l_sc[...] = jnp.zeros(l_sc.shape, F32)
    acc_sc[...] = jnp.zeros(acc_sc.shape, F32)

    def slc_tile(j, causal):
        rows = pl.ds(pl.multiple_of(j * SLC_TILE, SLC_TILE), SLC_TILE)
        k_aug = jnp.concatenate([ks_ref[rows, :], ex_ref[j]], axis=1)
        sc = _dot_nt(k_aug, q_aug)
        if causal:
            kpos = j * SLC_TILE + lax.broadcasted_iota(jnp.int32, (SLC_TILE, tq), 0)
            t_s = qi * tq + lax.broadcasted_iota(jnp.int32, (SLC_TILE, tq), 1)
            sc = sc + rep(jnp.where(kpos <= t_s, 0.0, NEG_INF))
        m_old = m_sc[...]
        m_new = jnp.maximum(m_old, jnp.max(sc, axis=0, keepdims=True))
        alpha = jnp.exp(m_old - m_new)
        pr = jnp.exp(sc - m_new)
        l_sc[...] = alpha * l_sc[...] + jnp.sum(pr, axis=0, keepdims=True)
        acc_sc[...] = alpha * acc_sc[...] + _dot(vst_sc[:, rows], pr.astype(BF16))
        m_sc[...] = m_new

    n_full = (qi * tq) // SLC_TILE
    lax.fori_loop(0, n_full, lambda j, c: (slc_tile(j, False), c)[1], 0)
    slc_tile(n_full, True)
    o_slc = acc_sc[...] * (1.0 / l_sc[...])

    span = WINDOW + tq
    w0 = pl.multiple_of(jnp.maximum(qi * tq - WINDOW, 0), KEY_TILE)
    kpos = w0 + lax.broadcasted_iota(jnp.int32, (span, tq), 0)
    t_w = qi * tq + lax.broadcasted_iota(jnp.int32, (span, tq), 1)
    in_window = (kpos <= t_w) & (kpos > t_w - WINDOW)
    sc = _dot_nt(kw_ref[pl.ds(w0, span), :], qr) + rep(jnp.where(in_window, 0.0, NEG_INF))
    pr = jnp.exp(sc - jnp.max(sc, axis=0, keepdims=True))
    o_win = _dot(vwt_sc[:, pl.ds(w0, span)], pr.astype(BF16)) * (1.0 / jnp.sum(pr, axis=0, keepdims=True))

    gt = gt_ref[...].T
    for h in range(hpg):
        cols = slice(h * tq, (h + 1) * tq)
        c0 = h * N_NSA_BRANCHES
        o = gt[c0:c0 + 1, :] * o_cmp[:, cols] + gt[c0 + 1:c0 + 2, :] * o_slc[:, cols] + gt[c0 + 2:c0 + 3, :] * o_win[:, cols]
        o_ref[:, h * HEAD_DIM:(h + 1) * HEAD_DIM] = o.T.astype(BF16)


def _nsa(qn, qr, kc, vc, kvr, gates, overlap, expand):
    B, S, _ = qn.shape
    G = N_KV_GROUPS
    tq = Q_TILE
    gw = HEADS_PER_GROUP * HEAD_DIM
    qspec = pl.BlockSpec((None, tq, gw), lambda b, g, i: (b, i, g))
    cspec = pl.BlockSpec((None, None) + kc.shape[2:], lambda b, g, i: (b, g, 0, 0))
    kv = lambda part: pl.BlockSpec((None, S, HEAD_DIM), lambda b, g, i: (b, 0, part * G + g))
    m = HEADS_PER_GROUP * tq
    return pl.pallas_call(
        _nsa_kernel,
        grid=(B, G, S // tq),
        in_specs=[qspec, qspec, cspec, cspec, kv(0), kv(1), kv(2), kv(3),
                  pl.BlockSpec((None, tq, LANES), lambda b, g, i: (b, i, g)), _full(overlap.shape), _full(expand.shape)],
        out_specs=qspec,
        out_shape=jax.ShapeDtypeStruct(qn.shape, BF16),
        scratch_shapes=[pltpu.VMEM((HEAD_DIM, S), BF16), pltpu.VMEM((HEAD_DIM, S), BF16), pltpu.VMEM((1, m), F32),
                        pltpu.VMEM((1, m), F32), pltpu.VMEM((HEAD_DIM, m), F32)],
        compiler_params=_params(("arbitrary", "arbitrary", "arbitrary")),
        name="nsa",
    )(qn, qr, kc, vc, kvr, kvr, kvr, kvr, gates, overlap, expand)


def _lru_kernel(lx_ref, lg_ref, cw_ref, cb_ref, wa_ref, ba_ref, wx_ref, bx_ref, lam_ref, o_ref,
                xs_sc, a_sc, u_sc, h_sc, carry_sc):
    ts = lx_ref.shape[0]
    width = lx_ref.shape[1]
    bw = width // LRU_BLOCKS

    @pl.when(pl.program_id(1) == 0)
    def _():
        xs_sc[0:SUBLANES, :] = jnp.zeros((SUBLANES, width), F32)
        carry_sc[...] = jnp.zeros(carry_sc.shape, F32)

    xs_sc[SUBLANES:SUBLANES + ts, :] = lx_ref[...].astype(F32)
    xc = cb_ref[...] + cw_ref[CONV_WIDTH - 1:CONV_WIDTH, :] * xs_sc[SUBLANES:SUBLANES + ts, :]
    for d in range(1, CONV_WIDTH):
        w = cw_ref[CONV_WIDTH - 1 - d:CONV_WIDTH - d, :]
        xc = xc + w * xs_sc[SUBLANES - d:SUBLANES - d + ts, :]
    xs_sc[0:SUBLANES, :] = xs_sc[ts:ts + SUBLANES, :]

    lam = -lam_ref[...]
    neg_c_softplus = -LRU_C * (jnp.maximum(lam, 0.0) + jnp.log1p(jnp.exp(-jnp.abs(lam))))
    xcb = xc.astype(BF16)
    for blk in range(LRU_BLOCKS):
        sl = slice(blk * bw, (blk + 1) * bw)
        r = _sigmoid(_dot(xcb[:, sl], wa_ref[blk]) + ba_ref[:, sl])
        i = _sigmoid(_dot(xcb[:, sl], wx_ref[blk]) + bx_ref[:, sl])
        log_a = r * neg_c_softplus[:, sl]
        a = jnp.exp(log_a)
        a_sc[:, sl] = a
        u_sc[:, sl] = jnp.sqrt(-jnp.tanh(log_a) * (a * a + 1.0)) * (i * xc[:, sl])

    row = lax.broadcasted_iota(jnp.int32, (SUBLANES, width), 0)

    def chunk(c, h_prev):
        r0 = pl.multiple_of(c * SUBLANES, SUBLANES)
        a = a_sc[pl.ds(r0, SUBLANES), :]
        u = u_sc[pl.ds(r0, SUBLANES), :]
        for d in (1, 2, 4):
            keep = row >= d
            u = jnp.where(keep, a * pltpu.roll(u, d, 0) + u, u)
            a = jnp.where(keep, a * pltpu.roll(a, d, 0), a)
        h = a * h_prev + u
        h_sc[pl.ds(r0, SUBLANES), :] = h
        return jnp.broadcast_to(h[SUBLANES - 1:SUBLANES, :], h.shape)

    carry_sc[...] = lax.fori_loop(0, ts // SUBLANES, chunk, carry_sc[...])
    o_ref[...] = (_gelu_tanh(lg_ref[...].astype(F32)) * h_sc[...]).astype(BF16)


def _lru(lx, lg, conv_w, conv_b, wa, ba, wx, bx, lam):
    B, S, W = lx.shape
    ts = min(LRU_ROWS, S)
    row = pl.BlockSpec((None, ts, W), lambda b, s: (b, s, 0))
    consts = (conv_w, conv_b.reshape(1, W), wa, ba.reshape(1, W), wx, bx.reshape(1, W), lam.reshape(1, W))
    return pl.pallas_call(
        _lru_kernel,
        grid=(B, S // ts),
        in_specs=[row, row] + [_full(a.shape) for a in consts],
        out_specs=row,
        out_shape=jax.ShapeDtypeStruct((B, S, W), BF16),
        scratch_shapes=[pltpu.VMEM((ts + SUBLANES, W), F32), pltpu.VMEM((ts, W), F32), pltpu.VMEM((ts, W), F32),
                        pltpu.VMEM((ts, W), F32), pltpu.VMEM((SUBLANES, W), F32)],
        compiler_params=_params(("arbitrary", "arbitrary")),
        name="lru",
    )(lx, lg, *consts)


def _merge_kernel(on_ref, ol_ref, mg_ref, x_ref, mod_ref, g2_ref, wn_ref, wl_ref, wo_ref, rw_ref, rb_ref, tri_ref,
                  x1_ref, h2_ref, rt_ref, cnt_ref, carry_sc):
    first = (pl.program_id(0) == 0) & (pl.program_id(1) == 0)

    @pl.when(first)
    def _():
        carry_sc[...] = jnp.zeros(carry_sc.shape, F32)

    d = x_ref.shape[1]
    y_nsa = _dot(on_ref[...], wn_ref[...])
    y_lru = _dot(ol_ref[...], wl_ref[...])
    merged = mg_ref[:, 0:d].astype(F32) * y_nsa + mg_ref[:, d:2 * d].astype(F32) * y_lru
    gate1 = mod_ref[2:3, :]
    shift2 = mod_ref[3:4, :]
    scale2 = mod_ref[4:5, :]
    x1 = x_ref[...] + gate1 * _dot(merged.astype(BF16), wo_ref[...])
    x1_ref[...] = x1
    h2 = _rms(x1, g2_ref[...]) * (1.0 + scale2) + shift2
    h2_ref[...] = h2.astype(BF16)

    logits = _dot(h2, rw_ref[...], precision=lax.Precision.HIGHEST) + rb_ref[...]
    lane = lax.broadcasted_iota(jnp.int32, logits.shape, 1)
    lane_f = lane.astype(F32)
    score = jnp.where(lane < N_EXPERTS, logits, LOWEST)
    picks = []
    onehot = jnp.zeros(logits.shape, F32)
    for _ in range(TOP_K):
        best = jnp.max(score, axis=-1, keepdims=True)
        idx = jnp.min(jnp.where(score == best, lane_f, float(LANES)), axis=-1, keepdims=True)
        hit = lane_f == idx
        picks.append((idx, best, hit))
        onehot = jnp.where(hit, 1.0, onehot)
        score = jnp.where(hit, LOWEST, score)
    ew = [jnp.exp(v - picks[0][1]) for _, v, _ in picks]
    den = ew[0]
    for v in ew[1:]:
        den = den + v

    before = _dot(tri_ref[...], onehot.astype(BF16)) + carry_sc[0:1, :]
    carry_sc[...] = carry_sc[...] + jnp.sum(onehot, axis=0, keepdims=True)
    cnt_ref[...] = carry_sc[...]

    out = jnp.zeros(logits.shape, F32)
    for k, (idx, _, hit) in enumerate(picks):
        rank = jnp.sum(jnp.where(hit, before, 0.0), axis=-1, keepdims=True)
        out = jnp.where(lane == k, idx, out)
        out = jnp.where(lane == TOP_K + k, rank, out)
        out = jnp.where(lane == 2 * TOP_K + k, ew[k] / den, out)
    rt_ref[...] = out


def _merge(o_nsa, o_lru, mg, x, mod3, norm2_g, wn, wl, wo, rw, rb, tri):
    B, S, D = x.shape
    tm = tri.shape[0]
    row = lambda w: pl.BlockSpec((None, tm, w), lambda b, s: (b, s, 0))
    consts = (norm2_g.reshape(1, D), wn, wl, wo, rw, rb, tri)
    return pl.pallas_call(
        _merge_kernel,
        grid=(B, S // tm),
        in_specs=[row(D), row(D), row(2 * D), row(D), pl.BlockSpec((None, 6, D), lambda b, s: (b, 0, 0))]
                 + [_full(a.shape) for a in consts],
        out_specs=[row(D), row(D), row(LANES), _full((SUBLANES, LANES))],
        out_shape=[jax.ShapeDtypeStruct((B, S, D), F32), jax.ShapeDtypeStruct((B, S, D), BF16),
                   jax.ShapeDtypeStruct((B, S, LANES), F32), jax.ShapeDtypeStruct((SUBLANES, LANES), F32)],
        scratch_shapes=[pltpu.VMEM((SUBLANES, LANES), F32)],
        compiler_params=_params(("arbitrary", "arbitrary")),
        name="merge",
    )(o_nsa, o_lru, mg, x, mod3, *consts)


def _expert_kernel(be_ref, nb_ref, x_ref, w1_ref, b1_ref, w2_ref, b2_ref, pick_ref, y_ref, w1_sc, w2_sc):
    i = pl.program_id(0)
    ff = w2_ref.shape[0]
    chunk = pick_ref.shape[0]

    @pl.when((i == 0) | (be_ref[i] != be_ref[jnp.maximum(i - 1, 0)]))
    def _():
        for c in range(2 * ff // chunk):
            r = _dot(w1_ref[:, c * chunk:(c + 1) * chunk].astype(BF16), pick_ref[...])
            half = chunk // 2
            w1_sc[:, c * half:(c + 1) * half] = r[:, 0:half].astype(BF16)
            w1_sc[:, ff + c * half:ff + (c + 1) * half] = r[:, half:chunk].astype(BF16)
        w2_sc[...] = w2_ref[...].astype(BF16)

    @pl.when(i < nb_ref[0])
    def _():
        h = _dot(x_ref[...], w1_sc[...]) + b1_ref[...]
        x_glu = jnp.minimum(h[:, 0:ff], SWIGLU_LIMIT)
        x_lin = jnp.clip(h[:, ff:2 * ff], -SWIGLU_LIMIT, SWIGLU_LIMIT)
        act = x_glu * _sigmoid(SWIGLU_ALPHA * x_glu) * (x_lin + 1.0)
        y_ref[...] = (_dot(act.astype(BF16), w2_sc[...]) + b2_ref[...]).astype(BF16)

    @pl.when(i >= nb_ref[0])
    def _():
        y_ref[...] = jnp.zeros(y_ref.shape, BF16)


def _experts(blk_e, n_used, buf, w1, b1, w2, b2):
    n_rows, D = buf.shape
    F = w2.shape[1]
    tm = EXPERT_ROWS
    chunk = 2 * LANES
    pick = np.zeros((chunk, chunk), np.float32)
    pick[2 * np.arange(LANES), np.arange(LANES)] = 1.0
    pick[2 * np.arange(LANES) + 1, LANES + np.arange(LANES)] = 1.0
    wspec = lambda a, b: pl.BlockSpec((None, a, b), lambda i, be, nb: (be[i], 0, 0))
    return pl.pallas_call(
        _expert_kernel,
        grid_spec=pltpu.PrefetchScalarGridSpec(
            num_scalar_prefetch=2,
            grid=(n_rows // tm,),
            in_specs=[pl.BlockSpec((tm, D), lambda i, be, nb: (i, 0)), wspec(D, 2 * F), wspec(1, 2 * F), wspec(F, D),
                      wspec(1, D), pl.BlockSpec((chunk, chunk), lambda i, be, nb: (0, 0))],
            out_specs=pl.BlockSpec((tm, D), lambda i, be, nb: (i, 0)),
            scratch_shapes=[pltpu.VMEM((D, 2 * F), BF16), pltpu.VMEM((F, D), BF16)],
        ),
        out_shape=jax.ShapeDtypeStruct((n_rows, D), BF16),
        compiler_params=_params(("arbitrary",)),
        name="experts",
    )(blk_e, n_used, buf, w1, b1, w2, b2, jnp.asarray(pick, dtype=BF16))


def _combine_kernel(x1_ref, yg_ref, rt_ref, mod_ref, o_ref):
    gate2 = mod_ref[5:6, :]
    rt = rt_ref[...]
    acc = rt[:, 2 * TOP_K:2 * TOP_K + 1] * yg_ref[0].astype(F32)
    for k in range(1, TOP_K):
        acc = acc + rt[:, 2 * TOP_K + k:2 * TOP_K + k + 1] * yg_ref[k].astype(F32)
    o_ref[...] = x1_ref[...] + gate2 * acc


def _combine(x1, yg, rt, mod3):
    B, S, D = x1.shape
    tm = min(COMBINE_ROWS, S)
    row = lambda w: pl.BlockSpec((None, tm, w), lambda b, s: (b, s, 0))
    return pl.pallas_call(
        _combine_kernel,
        grid=(B, S // tm),
        in_specs=[row(D), pl.BlockSpec((TOP_K, None, tm, D), lambda b, s: (0, b, s, 0)), row(LANES),
                  pl.BlockSpec((None, 6, D), lambda b, s: (b, 0, 0))],
        out_specs=row(D),
        out_shape=jax.ShapeDtypeStruct((B, S, D), F32),
        compiler_params=_params(("arbitrary", "arbitrary")),
        name="combine",
    )(x1, yg, rt, mod3)


def _overlap_matrix(n_cmp_pad, n_blk):
    cs = np.arange(n_cmp_pad)[:, None] * CMP_STRIDE
    js = np.arange(LANES)[None, :] * SEL_BLOCK
    m = (cs <= js + SEL_BLOCK - 1) & (cs + CMP_BLOCK - 1 >= js) & (np.arange(LANES)[None, :] < n_blk)
    return jnp.asarray(m.astype(np.float32).T)


def _block_onehot(seq):
    kt = np.arange(seq // SLC_TILE)[:, None, None]
    c = np.arange(SLC_TILE)[None, :, None]
    j = np.arange(LANES)[None, None, :]
    return jnp.asarray((j == (kt * SLC_TILE + c) // SEL_BLOCK).astype(np.float32), dtype=BF16)


def _layer(x, c, ang, ada_w, ada_b, norm1_g, w_in, q_norm_g, k_norm_g, cmp_pe_k, cmp_pe_v, cmp_wk1, cmp_wk2, cmp_wv1,
           cmp_wv2, conv_w, conv_b, lru_wa, lru_ba, lru_wx, lru_bx, lru_lambda, w_branch_out, w_out, norm2_g,
           router_w, router_b, moe_w1, moe_b1, moe_w2, moe_b2):
    B, S, D = x.shape
    T = B * S
    G = N_KV_GROUPS
    nsa_w = N_HEADS * HEAD_DIM
    kv_w = G * HEAD_DIM

    mod3 = _ada(c, ada_w, ada_b).reshape(B, 6, D)

    o = 0
    wq = w_in[:, o:o + nsa_w]; o += nsa_w
    wkc = w_in[:, o:o + 2 * kv_w]; o += 2 * kv_w
    wkr = w_in[:, o:o + 4 * kv_w]; o += 4 * kv_w
    n_gate = N_HEADS * N_NSA_BRANCHES
    wg_raw = w_in[:, o:o + n_gate].reshape(D, G, n_gate // G); o += n_gate
    wg = jnp.pad(wg_raw, ((0, 0), (0, 0), (0, LANES - n_gate // G))).reshape(D, G * LANES)
    wlx = w_in[:, o:o + D]; o += D
    wlg = w_in[:, o:o + D]; o += D
    wm = w_in[:, o:o + 2 * D]
    bf = lambda a: a.astype(BF16)

    qn, qr, kvc, kvr, gates, lx, lg, mg = _in_proj(x, mod3, norm1_g, ang, q_norm_g, k_norm_g, bf(wq), bf(wkc), bf(wkr),
                                                   bf(wg), bf(wlx), bf(wlg), bf(wm))

    nc = S // CMP_STRIDE
    a = kvc.reshape(B, nc, CMP_STRIDE, 2, G, HEAD_DIM).transpose(3, 0, 4, 1, 2, 5).reshape(2, B, G, nc, CMP_STRIDE * HEAD_DIM)
    pe = lambda p: jnp.broadcast_to(p.reshape(1, CMP_BLOCK * HEAD_DIM), (SUBLANES, CMP_BLOCK * HEAD_DIM)).astype(BF16)
    kc, vc = _compress(a[0], a[1], pe(cmp_pe_k), pe(cmp_pe_v), bf(cmp_wk1), bf(cmp_wk2), bf(cmp_wv1), bf(cmp_wv2),
                       k_norm_g)

    o_nsa = _nsa(qn, qr, kc, vc, kvr, gates, _overlap_matrix(nc, S // SEL_BLOCK), _block_onehot(S))
    o_lru = _lru(lx, lg, conv_w, conv_b, bf(lru_wa), lru_ba, bf(lru_wx), lru_bx, lru_lambda)

    tm = min(IN_ROWS, S)
    tri = jnp.asarray(np.tril(np.ones((tm, tm), np.float32), -1), dtype=BF16)
    rw = jnp.pad(router_w, ((0, 0), (0, LANES - N_EXPERTS)))
    rb = jnp.pad(router_b, (0, LANES - N_EXPERTS)).reshape(1, LANES)
    x1, h2, rt, cnt = _merge(o_nsa, o_lru, mg, x, mod3, norm2_g, bf(w_branch_out[:nsa_w]), bf(w_branch_out[nsa_w:]),
                             bf(w_out), rw, rb, tri)

    rt2 = rt.reshape(T, LANES)
    top_e = rt2[:, 0:TOP_K].astype(jnp.int32)
    rank = rt2[:, TOP_K:2 * TOP_K].astype(jnp.int32)
    counts = cnt[0, :N_EXPERTS].astype(jnp.int32)
    padded = (counts + EXPERT_ROWS - 1) // EXPERT_ROWS * EXPERT_ROWS
    ends = jnp.cumsum(padded)
    start = ends - padded
    dest = start[top_e] + rank
    n_rows = -(-(T * TOP_K + N_EXPERTS * (EXPERT_ROWS - 1)) // EXPERT_ROWS) * EXPERT_ROWS
    n_blocks = n_rows // EXPERT_ROWS
    blk_first = jnp.arange(n_blocks, dtype=jnp.int32) * EXPERT_ROWS
    blk_e = jnp.minimum(jnp.sum(ends[None, :] <= blk_first[:, None], axis=1), N_EXPERTS - 1).astype(jnp.int32)
    n_used = (ends[-1:] // EXPERT_ROWS).astype(jnp.int32)

    h2f = h2.reshape(T, D)
    flat_dest = dest.reshape(-1)
    src_tok = jnp.zeros((n_rows,), jnp.int32).at[flat_dest].set(jnp.arange(T * TOP_K, dtype=jnp.int32) // TOP_K)
    used = jnp.zeros((n_rows,), jnp.bool_).at[flat_dest].set(True)
    buf = jnp.where(used[:, None], h2f[src_tok], jnp.zeros((), BF16))

    b1 = jnp.concatenate([moe_b1[:, 0::2], moe_b1[:, 1::2]], axis=1)
    ybuf = _experts(blk_e, n_used, buf, moe_w1, b1.reshape(N_EXPERTS, 1, -1), moe_w2, moe_b2.reshape(N_EXPERTS, 1, D))
    yg = ybuf[dest.T].reshape(TOP_K, B, S, D)
    return _combine(x1, yg, rt, mod3)


def kernel(x, c, positions, ada_w, ada_b, norm1_g, w_in, q_norm_g, k_norm_g, cmp_pe_k, cmp_pe_v, cmp_wk1, cmp_wk2, cmp_wv1, cmp_wv2, conv_w, conv_b, lru_wa, lru_ba, lru_wx, lru_bx, lru_lambda, w_branch_out, w_out, norm2_g, router_w, router_b, moe_w1, moe_b1, moe_w2, moe_b2):
    inv = ROPE_THETA ** (-jnp.arange(0, HEAD_DIM, 2, dtype=F32) / HEAD_DIM)
    ang = positions.astype(F32)[..., None] * inv
    ang = jnp.concatenate([ang, ang], axis=-1)
    for l in range(ada_w.shape[0]):
        x = _layer(x, c, ang, ada_w[l], ada_b[l], norm1_g[l], w_in[l], q_norm_g[l], k_norm_g[l], cmp_pe_k[l],
                   cmp_pe_v[l], cmp_wk1[l], cmp_wk2[l], cmp_wv1[l], cmp_wv2[l], conv_w[l], conv_b[l], lru_wa[l],
                   lru_ba[l], lru_wx[l], lru_bx[l], lru_lambda[l], w_branch_out[l], w_out[l], norm2_g[l], router_w[l],
                   router_b[l], moe_w1[l], moe_b1[l], moe_w2[l], moe_b2[l])
    return x
```

```python
import functools
import math

import jax
import jax.numpy as jnp
import numpy as np
from jax import lax
from jax.experimental import pallas as pl
from jax.experimental.pallas import tpu as pltpu
from jax.experimental.pallas import tpu_sc as plsc

F32 = jnp.float32
BF16 = jnp.bfloat16

N_HEADS = 8
HEAD_DIM = 128
N_KV_GROUPS = 2
HEADS_PER_GROUP = N_HEADS // N_KV_GROUPS
N_NSA_BRANCHES = 3
CMP_BLOCK = 32
CMP_STRIDE = 16
CMP_HIDDEN = 256
SEL_BLOCK = 64
N_SELECT = 8
WINDOW = 512
ROPE_THETA = 10000.0
LRU_BLOCKS = 4
CONV_WIDTH = 4
LRU_C = 8.0
N_EXPERTS = 32
TOP_K = 4
SWIGLU_LIMIT = 7.0
SWIGLU_ALPHA = 1.702
RMS_EPS = 1e-6
NEG_INF = -1e30
LOWEST = -3.0e38

LANES = 128
SUBLANES = 8
VMEM_LIMIT = 56 * 1024 * 1024
SC_CORES = 2
SC_SUBCORES = 16
SC_CHUNK = 64

Q_TILE = 128
KEY_TILE = 128
SLC_TILE = 512
IN_ROWS = 512
LRU_ROWS = 256
EXPERT_ROWS = 512
COMBINE_ROWS = 256


def _sigmoid(v):
    return 1.0 / (1.0 + jnp.exp(-v))


def _gelu_tanh(v):
    return 0.5 * v * (1.0 + jnp.tanh(math.sqrt(2.0 / math.pi) * (v + 0.044715 * (v * v * v))))


def _rms(v, g):
    return v * lax.rsqrt(jnp.mean(v * v, axis=-1, keepdims=True) + RMS_EPS) * g


def _dot(a, b, **kw):
    return jnp.dot(a, b, preferred_element_type=F32, **kw)


def _dot_nt(a, b):
    return lax.dot_general(a, b, (((1,), (1,)), ((), ())), preferred_element_type=F32)


def _pack_bf16_pairs(v):
    n = v.shape[1] // 2
    lo = lax.bitcast_convert_type(v[:, 0:n].astype(BF16).astype(F32), jnp.int32)
    hi = lax.bitcast_convert_type(v[:, n:2 * n].astype(BF16).astype(F32), jnp.int32)
    return lax.shift_right_logical(lo, 16) | hi


def _unpack_bf16_pairs(w):
    lo = lax.bitcast_convert_type(lax.shift_left(w, 16), F32)
    hi = lax.bitcast_convert_type(w & jnp.int32(-65536), F32)
    return jnp.concatenate([lo, hi], axis=1)


def _full(shape):
    nd = len(shape)
    return pl.BlockSpec(shape, lambda *_: (0,) * nd)


def _params(sem):
    return pltpu.CompilerParams(dimension_semantics=sem, vmem_limit_bytes=VMEM_LIMIT)


def _ada_kernel(c_ref, w_ref, b_ref, o_ref):
    c = c_ref[...]
    o_ref[...] = _dot(c * _sigmoid(c), w_ref[...], precision=lax.Precision.HIGHEST) + b_ref[...]


def _ada(c, ada_w, ada_b):
    B, D = c.shape
    N = ada_w.shape[1]
    return pl.pallas_call(
        _ada_kernel,
        grid=(N // D,),
        in_specs=[_full((B, D)), pl.BlockSpec((D, D), lambda j: (0, j)), pl.BlockSpec((1, D), lambda j: (0, j))],
        out_specs=pl.BlockSpec((B, D), lambda j: (0, j)),
        out_shape=jax.ShapeDtypeStruct((B, N), F32),
        compiler_params=_params(("arbitrary",)),
        name="ada",
    )(c, ada_w, ada_b.reshape(1, N))


def _in_kernel(x_ref, mod_ref, g1_ref, ang_ref, qg_ref, kg_ref, wq_ref, wkc_ref, wkr_ref, wg_ref, wlx_ref, wlg_ref,
               wm_ref, qn_ref, qr_ref, kvc_ref, kvr_ref, gt_ref, lx_ref, lg_ref, mg_ref):
    x = x_ref[...]
    shift1 = mod_ref[0:1, :]
    scale1 = mod_ref[1:2, :]
    h = _rms(x, g1_ref[...]) * (1.0 + scale1) + shift1
    hb = h.astype(BF16)

    ang = ang_ref[...]
    cos = jnp.cos(ang)
    sin = jnp.sin(ang)
    lane = lax.broadcasted_iota(jnp.int32, ang.shape, 1)
    sin_signed = jnp.where(lane < HEAD_DIM // 2, -sin, sin)

    def rope(v):
        return v * cos + pltpu.roll(v, HEAD_DIM // 2, 1) * sin_signed

    q = _dot(hb, wq_ref[...])
    for hh in range(N_HEADS):
        sl = slice(hh * HEAD_DIM, (hh + 1) * HEAD_DIM)
        qh = _rms(q[:, sl], qg_ref[...])
        qn_ref[:, sl] = (qh * HEAD_DIM ** -0.5).astype(BF16)
        qr_ref[:, sl] = (rope(qh) * HEAD_DIM ** -0.5).astype(BF16)

    kvc_ref[...] = _dot(hb, wkc_ref[...]).astype(BF16)

    kvr = _dot(hb, wkr_ref[...])
    kvw = N_KV_GROUPS * HEAD_DIM
    for part in range(4):
        for gg in range(N_KV_GROUPS):
            sl = slice(part * kvw + gg * HEAD_DIM, part * kvw + (gg + 1) * HEAD_DIM)
            v = kvr[:, sl]
            if part % 2 == 0:
                row = 1 + part // 2
                v = rope(_rms(v, kg_ref[row:row + 1, :]))
            kvr_ref[:, sl] = v.astype(BF16)

    gt_ref[...] = _sigmoid(_dot(hb, wg_ref[...]))
    lx_ref[...] = _dot(hb, wlx_ref[...]).astype(BF16)
    lg_ref[...] = _dot(hb, wlg_ref[...]).astype(BF16)
    mg_ref[...] = _sigmoid(_dot(hb, wm_ref[...])).astype(BF16)


def _in_proj(x, mod3, norm1_g, ang, q_norm_g, k_norm_g, wq, wkc, wkr, wg, wlx, wlg, wm):
    B, S, D = x.shape
    tm = min(IN_ROWS, S)
    row = lambda w: pl.BlockSpec((None, tm, w), lambda b, s: (b, s, 0))
    widths = (wq.shape[1], wq.shape[1], wkc.shape[1], wkr.shape[1], wg.shape[1], wlx.shape[1], wlg.shape[1], wm.shape[1])
    dtypes = (BF16, BF16, BF16, BF16, F32, BF16, BF16, BF16)
    weights = (wq, wkc, wkr, wg, wlx, wlg, wm)
    return pl.pallas_call(
        _in_kernel,
        grid=(B, S // tm),
        in_specs=[row(D), pl.BlockSpec((None, 6, D), lambda b, s: (b, 0, 0)), _full((1, D)), row(HEAD_DIM),
                  _full((1, HEAD_DIM)), _full(k_norm_g.shape)] + [_full(w.shape) for w in weights],
        out_specs=[row(w) for w in widths],
        out_shape=[jax.ShapeDtypeStruct((B, S, w), dt) for w, dt in zip(widths, dtypes)],
        compiler_params=_params(("arbitrary", "arbitrary")),
        name="in_proj",
    )(x, mod3, norm1_g.reshape(1, D), ang, q_norm_g.reshape(1, HEAD_DIM), k_norm_g, *weights)


def _cmp_kernel(ak_ref, av_ref, pek_ref, pev_ref, wk1_ref, wk2_ref, wv1_ref, wv2_ref, kg_ref, kc_ref, vc_ref):
    half = CMP_STRIDE * HEAD_DIM

    def compress(a_ref, pe_ref, w1_ref, w2_ref):
        a = a_ref[...]
        u = _dot(a, w1_ref[0:half, :])
        v = _dot(a, w1_ref[half:2 * half, :])
        pw = _dot(pe_ref[...], w1_ref[...])
        pre = u + pltpu.roll(v, v.shape[0] - 1, 0) + pw[0:1, :]
        return _dot(_gelu_tanh(pre).astype(BF16), w2_ref[...])

    kc_ref[...] = _rms(compress(ak_ref, pek_ref, wk1_ref, wk2_ref), kg_ref[0:1, :]).astype(BF16)
    vc_ref[...] = compress(av_ref, pev_ref, wv1_ref, wv2_ref).T.astype(BF16)


def _compress(a_k, a_v, pe_k, pe_v, wk1, wk2, wv1, wv2, k_norm_g):
    B, G, NC, W = a_k.shape
    blk = pl.BlockSpec((None, None, NC, W), lambda b, g: (b, g, 0, 0))
    out = pl.BlockSpec((None, None, NC, HEAD_DIM), lambda b, g: (b, g, 0, 0))
    consts = (pe_k, pe_v, wk1, wk2, wv1, wv2, k_norm_g)
    return pl.pallas_call(
        _cmp_kernel,
        grid=(B, G),
        in_specs=[blk, blk] + [_full(a.shape) for a in consts],
        out_specs=[out, out],
        out_shape=[jax.ShapeDtypeStruct((B, G, NC, HEAD_DIM), BF16)] * 2,
        compiler_params=_params(("arbitrary", "arbitrary")),
        name="compress",
    )(a_k, a_v, *consts)


def _nsa_kernel(qn_ref, qr_ref, kc_ref, vc_ref, ks_ref, vs_ref, kw_ref, vw_ref, gt_ref, ov_ref, ex_ref, o_ref,
                vst_sc, vwt_sc, m_sc, l_sc, acc_sc):
    qi = pl.program_id(2)
    tq = qn_ref.shape[0]
    hpg = HEADS_PER_GROUP
    m_cols = hpg * tq
    seq = ks_ref.shape[0]
    n_blk = seq // SEL_BLOCK

    @pl.when(qi == 0)
    def _():
        for kt in range(seq // KEY_TILE):
            rows = slice(kt * KEY_TILE, (kt + 1) * KEY_TILE)
            vst_sc[:, rows] = vs_ref[rows, :].astype(F32).T.astype(BF16)
            vwt_sc[:, rows] = vw_ref[rows, :].astype(F32).T.astype(BF16)

    def stack(ref):
        return jnp.concatenate([ref[:, h * HEAD_DIM:(h + 1) * HEAD_DIM] for h in range(hpg)], axis=0)

    def rep(v):
        return jnp.concatenate([v] * hpg, axis=1)

    qn = stack(qn_ref)
    qr = stack(qr_ref)
    row = lax.broadcasted_iota(jnp.int32, (KEY_TILE, tq), 0)
    col = lax.broadcasted_iota(jnp.int32, (KEY_TILE, tq), 1)
    t_q = qi * tq + col

    cmask = rep(row * CMP_STRIDE + (CMP_BLOCK - 1) <= t_q)
    s = jnp.where(cmask, _dot_nt(kc_ref[...], qn), NEG_INF)
    e = jnp.exp(s - jnp.max(s, axis=0, keepdims=True))
    p = jnp.where(cmask, e * (1.0 / jnp.sum(e, axis=0, keepdims=True)), 0.0)
    o_cmp = _dot(vc_ref[...], p.astype(BF16))

    psum = p[:, 0:tq]
    for h in range(1, hpg):
        psum = psum + p[:, h * tq:(h + 1) * tq]
    imp = _dot(ov_ref[...], psum, precision=lax.Precision.HIGHEST)
    blk = row.astype(F32)
    cur = (t_q // SEL_BLOCK).astype(F32)
    forced = (blk == 0.0) | (blk == cur) | (blk == cur - 1.0)
    score = jnp.where(forced, 1e6, jnp.where(blk <= cur, imp, -1e6))
    score = jnp.where(row < n_blk, score, LOWEST)
    bias = jnp.full((KEY_TILE, tq), NEG_INF, F32)
    for _ in range(min(N_SELECT, n_blk)):
        best = jnp.max(score, axis=0, keepdims=True)
        idx = jnp.min(jnp.where(score == best, blk, float(LANES)), axis=0, keepdims=True)
        hit = blk == idx
        bias = jnp.where(hit, 0.0, bias)
        score = jnp.where(hit, LOWEST, score)
    q_aug = jnp.concatenate([qr, jnp.concatenate([bias.T.astype(BF16)] * hpg, axis=0)], axis=1)

    m_sc[...] = jnp.full(m_sc.shape, NEG_INF, F32)
    l_sc[...] = jnp.zeros(l_sc.shape, F32)
    acc_sc[...] = jnp.zeros(acc_sc.shape, F32)

    def slc_tile(j, causal):
        rows = pl.ds(pl.multiple_of(j * SLC_TILE, SLC_TILE), SLC_TILE)
        k_aug = jnp.concatenate([ks_ref[rows, :], ex_ref[j]], axis=1)
        sc = _dot_nt(k_aug, q_aug)
        if causal:
            kpos = j * SLC_TILE + lax.broadcasted_iota(jnp.int32, (SLC_TILE, tq), 0)
            t_s = qi * tq + lax.broadcasted_iota(jnp.int32, (SLC_TILE, tq), 1)
            sc = sc + rep(jnp.where(kpos <= t_s, 0.0, NEG_INF))
        m_old = m_sc[...]
        m_new = jnp.maximum(m_old, jnp.max(sc, axis=0, keepdims=True))
        alpha = jnp.exp(m_old - m_new)
        pr = jnp.exp(sc - m_new)
        l_sc[...] = alpha * l_sc[...] + jnp.sum(pr, axis=0, keepdims=True)
        acc_sc[...] = alpha * acc_sc[...] + _dot(vst_sc[:, rows], pr.astype(BF16))
        m_sc[...] = m_new

    n_full = (qi * tq) // SLC_TILE
    lax.fori_loop(0, n_full, lambda j, c: (slc_tile(j, False), c)[1], 0)
    slc_tile(n_full, True)
    o_slc = acc_sc[...] * (1.0 / l_sc[...])

    span = WINDOW + tq
    w0 = pl.multiple_of(jnp.maximum(qi * tq - WINDOW, 0), KEY_TILE)
    kpos = w0 + lax.broadcasted_iota(jnp.int32, (span, tq), 0)
    t_w = qi * tq + lax.broadcasted_iota(jnp.int32, (span, tq), 1)
    in_window = (kpos <= t_w) & (kpos > t_w - WINDOW)
    sc = _dot_nt(kw_ref[pl.ds(w0, span), :], qr) + rep(jnp.where(in_window, 0.0, NEG_INF))
    pr = jnp.exp(sc - jnp.max(sc, axis=0, keepdims=True))
    o_win = _dot(vwt_sc[:, pl.ds(w0, span)], pr.astype(BF16)) * (1.0 / jnp.sum(pr, axis=0, keepdims=True))

    gt = gt_ref[...].T
    for h in range(hpg):
        cols = slice(h * tq, (h + 1) * tq)
        c0 = h * N_NSA_BRANCHES
        o = gt[c0:c0 + 1, :] * o_cmp[:, cols] + gt[c0 + 1:c0 + 2, :] * o_slc[:, cols] + gt[c0 + 2:c0 + 3, :] * o_win[:, cols]
        o_ref[:, h * HEAD_DIM:(h + 1) * HEAD_DIM] = o.T.astype(BF16)


def _nsa(qn, qr, kc, vc, kvr, gates, overlap, expand):
    B, S, _ = qn.shape
    G = N_KV_GROUPS
    tq = Q_TILE
    gw = HEADS_PER_GROUP * HEAD_DIM
    qspec = pl.BlockSpec((None, tq, gw), lambda b, g, i: (b, i, g))
    cspec = pl.BlockSpec((None, None) + kc.shape[2:], lambda b, g, i: (b, g, 0, 0))
    kv = lambda part: pl.BlockSpec((None, S, HEAD_DIM), lambda b, g, i: (b, 0, part * G + g))
    m = HEADS_PER_GROUP * tq
    return pl.pallas_call(
        _nsa_kernel,
        grid=(B, G, S // tq),
        in_specs=[qspec, qspec, cspec, cspec, kv(0), kv(1), kv(2), kv(3),
                  pl.BlockSpec((None, tq, LANES), lambda b, g, i: (b, i, g)), _full(overlap.shape), _full(expand.shape)],
        out_specs=qspec,
        out_shape=jax.ShapeDtypeStruct(qn.shape, BF16),
        scratch_shapes=[pltpu.VMEM((HEAD_DIM, S), BF16), pltpu.VMEM((HEAD_DIM, S), BF16), pltpu.VMEM((1, m), F32),
                        pltpu.VMEM((1, m), F32), pltpu.VMEM((HEAD_DIM, m), F32)],
        compiler_params=_params(("arbitrary", "arbitrary", "arbitrary")),
        name="nsa",
    )(qn, qr, kc, vc, kvr, kvr, kvr, kvr, gates, overlap, expand)


def _lru_kernel(lx_ref, lg_ref, cw_ref, cb_ref, wa_ref, ba_ref, wx_ref, bx_ref, lam_ref, o_ref,
                xs_sc, a_sc, u_sc, h_sc, carry_sc):
    ts = lx_ref.shape[0]
    width = lx_ref.shape[1]
    bw = width // LRU_BLOCKS

    @pl.when(pl.program_id(1) == 0)
    def _():
        xs_sc[0:SUBLANES, :] = jnp.zeros((SUBLANES, width), F32)
        carry_sc[...] = jnp.zeros(carry_sc.shape, F32)

    xs_sc[SUBLANES:SUBLANES + ts, :] = lx_ref[...].astype(F32)
    xc = cb_ref[...] + cw_ref[CONV_WIDTH - 1:CONV_WIDTH, :] * xs_sc[SUBLANES:SUBLANES + ts, :]
    for d in range(1, CONV_WIDTH):
        w = cw_ref[CONV_WIDTH - 1 - d:CONV_WIDTH - d, :]
        xc = xc + w * xs_sc[SUBLANES - d:SUBLANES - d + ts, :]
    xs_sc[0:SUBLANES, :] = xs_sc[ts:ts + SUBLANES, :]

    lam = -lam_ref[...]
    neg_c_softplus = -LRU_C * (jnp.maximum(lam, 0.0) + jnp.log1p(jnp.exp(-jnp.abs(lam))))
    xcb = xc.astype(BF16)
    for blk in range(LRU_BLOCKS):
        sl = slice(blk * bw, (blk + 1) * bw)
        r = _sigmoid(_dot(xcb[:, sl], wa_ref[blk]) + ba_ref[:, sl])
        i = _sigmoid(_dot(xcb[:, sl], wx_ref[blk]) + bx_ref[:, sl])
        log_a = r * neg_c_softplus[:, sl]
        a = jnp.exp(log_a)
        a_sc[:, sl] = a
        u_sc[:, sl] = jnp.sqrt(-jnp.tanh(log_a) * (a * a + 1.0)) * (i * xc[:, sl])

    row = lax.broadcasted_iota(jnp.int32, (SUBLANES, width), 0)

    def chunk(c, h_prev):
        r0 = pl.multiple_of(c * SUBLANES, SUBLANES)
        a = a_sc[pl.ds(r0, SUBLANES), :]
        u = u_sc[pl.ds(r0, SUBLANES), :]
        for d in (1, 2, 4):
            keep = row >= d
            u = jnp.where(keep, a * pltpu.roll(u, d, 0) + u, u)
            a = jnp.where(keep, a * pltpu.roll(a, d, 0), a)
        h = a * h_prev + u
        h_sc[pl.ds(r0, SUBLANES), :] = h
        return jnp.broadcast_to(h[SUBLANES - 1:SUBLANES, :], h.shape)

    carry_sc[...] = lax.fori_loop(0, ts // SUBLANES, chunk, carry_sc[...])
    o_ref[...] = (_gelu_tanh(lg_ref[...].astype(F32)) * h_sc[...]).astype(BF16)


def _lru(lx, lg, conv_w, conv_b, wa, ba, wx, bx, lam):
    B, S, W = lx.shape
    ts = min(LRU_ROWS, S)
    row = pl.BlockSpec((None, ts, W), lambda b, s: (b, s, 0))
    consts = (conv_w, conv_b.reshape(1, W), wa, ba.reshape(1, W), wx, bx.reshape(1, W), lam.reshape(1, W))
    return pl.pallas_call(
        _lru_kernel,
        grid=(B, S // ts),
        in_specs=[row, row] + [_full(a.shape) for a in consts],
        out_specs=row,
        out_shape=jax.ShapeDtypeStruct((B, S, W), BF16),
        scratch_shapes=[pltpu.VMEM((ts + SUBLANES, W), F32), pltpu.VMEM((ts, W), F32), pltpu.VMEM((ts, W), F32),
                        pltpu.VMEM((ts, W), F32), pltpu.VMEM((SUBLANES, W), F32)],
        compiler_params=_params(("arbitrary", "arbitrary")),
        name="lru",
    )(lx, lg, *consts)


def _merge_kernel(on_ref, ol_ref, mg_ref, x_ref, mod_ref, g2_ref, wn_ref, wl_ref, wo_ref, rw_ref, rb_ref, tri_ref,
                  x1_ref, h2_ref, rt_ref, cnt_ref, carry_sc):
    first = (pl.program_id(0) == 0) & (pl.program_id(1) == 0)

    @pl.when(first)
    def _():
        carry_sc[...] = jnp.zeros(carry_sc.shape, F32)

    d = x_ref.shape[1]
    y_nsa = _dot(on_ref[...], wn_ref[...])
    y_lru = _dot(ol_ref[...], wl_ref[...])
    merged = mg_ref[:, 0:d].astype(F32) * y_nsa + mg_ref[:, d:2 * d].astype(F32) * y_lru
    gate1 = mod_ref[2:3, :]
    shift2 = mod_ref[3:4, :]
    scale2 = mod_ref[4:5, :]
    x1 = x_ref[...] + gate1 * _dot(merged.astype(BF16), wo_ref[...])
    x1_ref[...] = x1
    h2 = _rms(x1, g2_ref[...]) * (1.0 + scale2) + shift2
    h2_ref[...] = _pack_bf16_pairs(h2)

    logits = _dot(h2, rw_ref[...], precision=lax.Precision.HIGHEST) + rb_ref[...]
    lane = lax.broadcasted_iota(jnp.int32, logits.shape, 1)
    lane_f = lane.astype(F32)
    score = jnp.where(lane < N_EXPERTS, logits, LOWEST)
    picks = []
    onehot = jnp.zeros(logits.shape, F32)
    for _ in range(TOP_K):
        best = jnp.max(score, axis=-1, keepdims=True)
        idx = jnp.min(jnp.where(score == best, lane_f, float(LANES)), axis=-1, keepdims=True)
        hit = lane_f == idx
        picks.append((idx, best, hit))
        onehot = jnp.where(hit, 1.0, onehot)
        score = jnp.where(hit, LOWEST, score)
    ew = [jnp.exp(v - picks[0][1]) for _, v, _ in picks]
    den = ew[0]
    for v in ew[1:]:
        den = den + v

    before = _dot(tri_ref[...], onehot.astype(BF16)) + carry_sc[0:1, :]
    carry_sc[...] = carry_sc[...] + jnp.sum(onehot, axis=0, keepdims=True)
    cnt_ref[...] = carry_sc[...]

    out = jnp.zeros(logits.shape, F32)
    for k, (idx, _, hit) in enumerate(picks):
        rank = jnp.sum(jnp.where(hit, before, 0.0), axis=-1, keepdims=True)
        out = jnp.where(lane == k, idx, out)
        out = jnp.where(lane == TOP_K + k, rank, out)
        out = jnp.where(lane == 2 * TOP_K + k, ew[k] / den, out)
    rt_ref[...] = out


def _merge(o_nsa, o_lru, mg, x, mod3, norm2_g, wn, wl, wo, rw, rb, tri):
    B, S, D = x.shape
    tm = tri.shape[0]
    row = lambda w: pl.BlockSpec((None, tm, w), lambda b, s: (b, s, 0))
    consts = (norm2_g.reshape(1, D), wn, wl, wo, rw, rb, tri)
    return pl.pallas_call(
        _merge_kernel,
        grid=(B, S // tm),
        in_specs=[row(D), row(D), row(2 * D), row(D), pl.BlockSpec((None, 6, D), lambda b, s: (b, 0, 0))]
                 + [_full(a.shape) for a in consts],
        out_specs=[row(D), row(D // 2), row(LANES), _full((SUBLANES, LANES))],
        out_shape=[jax.ShapeDtypeStruct((B, S, D), F32), jax.ShapeDtypeStruct((B, S, D // 2), jnp.int32),
                   jax.ShapeDtypeStruct((B, S, LANES), F32), jax.ShapeDtypeStruct((SUBLANES, LANES), F32)],
        scratch_shapes=[pltpu.VMEM((SUBLANES, LANES), F32)],
        compiler_params=_params(("arbitrary", "arbitrary")),
        name="merge",
    )(o_nsa, o_lru, mg, x, mod3, *consts)


def _expert_kernel(be_ref, nb_ref, x_ref, w1_ref, b1_ref, w2_ref, b2_ref, pick_ref, y_ref, w1_sc, w2_sc):
    i = pl.program_id(0)
    ff = w2_ref.shape[0]
    chunk = pick_ref.shape[0]

    @pl.when((i == 0) | (be_ref[i] != be_ref[jnp.maximum(i - 1, 0)]))
    def _():
        for c in range(2 * ff // chunk):
            r = _dot(w1_ref[:, c * chunk:(c + 1) * chunk].astype(BF16), pick_ref[...])
            half = chunk // 2
            w1_sc[:, c * half:(c + 1) * half] = r[:, 0:half].astype(BF16)
            w1_sc[:, ff + c * half:ff + (c + 1) * half] = r[:, half:chunk].astype(BF16)
        w2_sc[...] = w2_ref[...].astype(BF16)

    @pl.when(i < nb_ref[0])
    def _():
        h = _dot(_unpack_bf16_pairs(x_ref[...]).astype(BF16), w1_sc[...]) + b1_ref[...]
        x_glu = jnp.minimum(h[:, 0:ff], SWIGLU_LIMIT)
        x_lin = jnp.clip(h[:, ff:2 * ff], -SWIGLU_LIMIT, SWIGLU_LIMIT)
        act = x_glu * _sigmoid(SWIGLU_ALPHA * x_glu) * (x_lin + 1.0)
        y_ref[...] = _pack_bf16_pairs(_dot(act.astype(BF16), w2_sc[...]) + b2_ref[...])

    @pl.when(i >= nb_ref[0])
    def _():
        y_ref[...] = jnp.zeros(y_ref.shape, jnp.int32)


def _experts(blk_e, n_used, buf, w1, b1, w2, b2):
    n_rows = buf.shape[0]
    F, D = w2.shape[1], w2.shape[2]
    tm = EXPERT_ROWS
    chunk = 2 * LANES
    pick = np.zeros((chunk, chunk), np.float32)
    pick[2 * np.arange(LANES), np.arange(LANES)] = 1.0
    pick[2 * np.arange(LANES) + 1, LANES + np.arange(LANES)] = 1.0
    wspec = lambda a, b: pl.BlockSpec((None, a, b), lambda i, be, nb: (be[i], 0, 0))
    return pl.pallas_call(
        _expert_kernel,
        grid_spec=pltpu.PrefetchScalarGridSpec(
            num_scalar_prefetch=2,
            grid=(n_rows // tm,),
            in_specs=[pl.BlockSpec((tm, D // 2), lambda i, be, nb: (i, 0)), wspec(D, 2 * F), wspec(1, 2 * F), wspec(F, D),
                      wspec(1, D), pl.BlockSpec((chunk, chunk), lambda i, be, nb: (0, 0))],
            out_specs=pl.BlockSpec((tm, D // 2), lambda i, be, nb: (i, 0)),
            scratch_shapes=[pltpu.VMEM((D, 2 * F), BF16), pltpu.VMEM((F, D), BF16)],
        ),
        out_shape=jax.ShapeDtypeStruct((n_rows, D // 2), jnp.int32),
        compiler_params=_params(("arbitrary",)),
        name="experts",
    )(blk_e, n_used, buf, w1, b1, w2, b2, jnp.asarray(pick, dtype=BF16))


def _sc_mesh():
    return plsc.VectorSubcoreMesh(core_axis_name="c", subcore_axis_name="s")


def _sc_worker():
    return lax.axis_index("s") * SC_CORES + lax.axis_index("c")


def _sc_scatter_rows(rows, dest, n_rows):
    T, W = rows.shape
    K = dest.shape[0]
    n_workers = SC_CORES * SC_SUBCORES
    per_w = T // n_workers
    ch = SC_CHUNK
    n_ch = per_w // ch
    assert per_w * n_workers == T and n_ch * ch == per_w and n_ch % 2 == 0
    dest4 = dest.reshape(K, n_workers, n_ch, ch).transpose(1, 2, 0, 3)

    @functools.partial(
        pl.kernel, mesh=_sc_mesh(), out_type=jax.ShapeDtypeStruct((n_rows, W), rows.dtype),
        scratch_types=[pltpu.VMEM((n_ch, K, ch), jnp.int32), pltpu.VMEM((2, ch, W), rows.dtype),
                       pltpu.SemaphoreType.DMA((2,)), pltpu.SemaphoreType.DMA((2,))])
    def scatter(rows_hbm, dest_hbm, out_hbm, idx_v, rows_v, read_sem, write_sem):
        wid = _sc_worker()
        base = wid * per_w
        pltpu.sync_copy(dest_hbm.at[wid], idx_v)

        def read(c, b):
            src = rows_hbm.at[pl.ds(pl.multiple_of(base + c * ch, ch), ch)]
            return pltpu.make_async_copy(src, rows_v.at[b], read_sem.at[b])

        def write(c, b, k):
            return pltpu.make_async_copy(rows_v.at[b], out_hbm.at[idx_v.at[c, k]], write_sem.at[b])

        read(0, 0).start()

        @pl.loop(0, n_ch, step=2)
        def _(i):
            for b in range(2):
                c = i + b
                read(c, b).wait()

                @pl.when(c >= 1)
                def _():
                    for k in range(K):
                        write(c - 1, 1 - b, k).wait()

                @pl.when(c + 1 < n_ch)
                def _():
                    read(c + 1, 1 - b).start()

                for k in range(K):
                    write(c, b, k).start()

        for k in range(K):
            write(n_ch - 1, 1, k).wait()

    return scatter(rows, dest4)


def _sc_gather_rows(table, idx):
    N = idx.shape[0]
    W = table.shape[1]
    n_workers = SC_CORES * SC_SUBCORES
    per_w = N // n_workers
    ch = SC_CHUNK
    n_ch = per_w // ch
    assert per_w * n_workers == N and n_ch * ch == per_w and n_ch % 2 == 0
    idx3 = idx.reshape(n_workers, n_ch, ch)

    @functools.partial(
        pl.kernel, mesh=_sc_mesh(), out_type=jax.ShapeDtypeStruct((N, W), table.dtype),
        scratch_types=[pltpu.VMEM((n_ch, ch), jnp.int32), pltpu.VMEM((2, ch, W), table.dtype),
                       pltpu.SemaphoreType.DMA((2,)), pltpu.SemaphoreType.DMA((2,))])
    def gather(table_hbm, idx_hbm, out_hbm, idx_v, rows_v, gather_sem, write_sem):
        wid = _sc_worker()
        base = wid * per_w
        pltpu.sync_copy(idx_hbm.at[wid], idx_v)

        def fetch(c, b):
            return pltpu.make_async_copy(table_hbm.at[idx_v.at[c]], rows_v.at[b], gather_sem.at[b])

        def write(c, b):
            dst = out_hbm.at[pl.ds(pl.multiple_of(base + c * ch, ch), ch)]
            return pltpu.make_async_copy(rows_v.at[b], dst, write_sem.at[b])

        fetch(0, 0).start()

        @pl.loop(0, n_ch, step=2)
        def _(i):
            for b in range(2):
                c = i + b
                fetch(c, b).wait()

                @pl.when(c >= 1)
                def _():
                    write(c - 1, 1 - b).wait()

                @pl.when(c + 1 < n_ch)
                def _():
                    fetch(c + 1, 1 - b).start()

                write(c, b).start()

        write(n_ch - 1, 1).wait()

    return gather(table, idx3)


def _combine_kernel(x1_ref, yg_ref, rt_ref, mod_ref, o_ref):
    gate2 = mod_ref[5:6, :]
    rt = rt_ref[...]
    acc = rt[:, 2 * TOP_K:2 * TOP_K + 1] * _unpack_bf16_pairs(yg_ref[0])
    for k in range(1, TOP_K):
        acc = acc + rt[:, 2 * TOP_K + k:2 * TOP_K + k + 1] * _unpack_bf16_pairs(yg_ref[k])
    o_ref[...] = x1_ref[...] + gate2 * acc


def _combine(x1, yg, rt, mod3):
    B, S, D = x1.shape
    tm = min(COMBINE_ROWS, S)
    row = lambda w: pl.BlockSpec((None, tm, w), lambda b, s: (b, s, 0))
    return pl.pallas_call(
        _combine_kernel,
        grid=(B, S // tm),
        in_specs=[row(D), pl.BlockSpec((TOP_K, None, tm, D // 2), lambda b, s: (0, b, s, 0)), row(LANES),
                  pl.BlockSpec((None, 6, D), lambda b, s: (b, 0, 0))],
        out_specs=row(D),
        out_shape=jax.ShapeDtypeStruct((B, S, D), F32),
        compiler_params=_params(("arbitrary", "arbitrary")),
        name="combine",
    )(x1, yg, rt, mod3)


def _overlap_matrix(n_cmp_pad, n_blk):
    cs = np.arange(n_cmp_pad)[:, None] * CMP_STRIDE
    js = np.arange(LANES)[None, :] * SEL_BLOCK
    m = (cs <= js + SEL_BLOCK - 1) & (cs + CMP_BLOCK - 1 >= js) & (np.arange(LANES)[None, :] < n_blk)
    return jnp.asarray(m.astype(np.float32).T)


def _block_onehot(seq):
    kt = np.arange(seq // SLC_TILE)[:, None, None]
    c = np.arange(SLC_TILE)[None, :, None]
    j = np.arange(LANES)[None, None, :]
    return jnp.asarray((j == (kt * SLC_TILE + c) // SEL_BLOCK).astype(np.float32), dtype=BF16)


def _layer(x, c, ang, ada_w, ada_b, norm1_g, w_in, q_norm_g, k_norm_g, cmp_pe_k, cmp_pe_v, cmp_wk1, cmp_wk2, cmp_wv1,
           cmp_wv2, conv_w, conv_b, lru_wa, lru_ba, lru_wx, lru_bx, lru_lambda, w_branch_out, w_out, norm2_g,
           router_w, router_b, moe_w1, moe_b1, moe_w2, moe_b2):
    B, S, D = x.shape
    T = B * S
    G = N_KV_GROUPS
    nsa_w = N_HEADS * HEAD_DIM
    kv_w = G * HEAD_DIM

    mod3 = _ada(c, ada_w, ada_b).reshape(B, 6, D)

    o = 0
    wq = w_in[:, o:o + nsa_w]; o += nsa_w
    wkc = w_in[:, o:o + 2 * kv_w]; o += 2 * kv_w
    wkr = w_in[:, o:o + 4 * kv_w]; o += 4 * kv_w
    n_gate = N_HEADS * N_NSA_BRANCHES
    wg_raw = w_in[:, o:o + n_gate].reshape(D, G, n_gate // G); o += n_gate
    wg = jnp.pad(wg_raw, ((0, 0), (0, 0), (0, LANES - n_gate // G))).reshape(D, G * LANES)
    wlx = w_in[:, o:o + D]; o += D
    wlg = w_in[:, o:o + D]; o += D
    wm = w_in[:, o:o + 2 * D]
    bf = lambda a: a.astype(BF16)

    qn, qr, kvc, kvr, gates, lx, lg, mg = _in_proj(x, mod3, norm1_g, ang, q_norm_g, k_norm_g, bf(wq), bf(wkc), bf(wkr),
                                                   bf(wg), bf(wlx), bf(wlg), bf(wm))

    nc = S // CMP_STRIDE
    a = kvc.reshape(B, nc, CMP_STRIDE, 2, G, HEAD_DIM).transpose(3, 0, 4, 1, 2, 5).reshape(2, B, G, nc, CMP_STRIDE * HEAD_DIM)
    pe = lambda p: jnp.broadcast_to(p.reshape(1, CMP_BLOCK * HEAD_DIM), (SUBLANES, CMP_BLOCK * HEAD_DIM)).astype(BF16)
    kc, vc = _compress(a[0], a[1], pe(cmp_pe_k), pe(cmp_pe_v), bf(cmp_wk1), bf(cmp_wk2), bf(cmp_wv1), bf(cmp_wv2),
                       k_norm_g)

    o_nsa = _nsa(qn, qr, kc, vc, kvr, gates, _overlap_matrix(nc, S // SEL_BLOCK), _block_onehot(S))
    o_lru = _lru(lx, lg, conv_w, conv_b, bf(lru_wa), lru_ba, bf(lru_wx), lru_bx, lru_lambda)

    tm = min(IN_ROWS, S)
    tri = jnp.asarray(np.tril(np.ones((tm, tm), np.float32), -1), dtype=BF16)
    rw = jnp.pad(router_w, ((0, 0), (0, LANES - N_EXPERTS)))
    rb = jnp.pad(router_b, (0, LANES - N_EXPERTS)).reshape(1, LANES)
    x1, h2, rt, cnt = _merge(o_nsa, o_lru, mg, x, mod3, norm2_g, bf(w_branch_out[:nsa_w]), bf(w_branch_out[nsa_w:]),
                             bf(w_out), rw, rb, tri)

    rt2 = rt.reshape(T, LANES)
    top_e = rt2[:, 0:TOP_K].astype(jnp.int32)
    rank = rt2[:, TOP_K:2 * TOP_K].astype(jnp.int32)
    counts = cnt[0, :N_EXPERTS].astype(jnp.int32)
    padded = (counts + EXPERT_ROWS - 1) // EXPERT_ROWS * EXPERT_ROWS
    ends = jnp.cumsum(padded)
    start = ends - padded
    dest = start[top_e] + rank
    n_rows = -(-(T * TOP_K + N_EXPERTS * (EXPERT_ROWS - 1)) // EXPERT_ROWS) * EXPERT_ROWS
    n_blocks = n_rows // EXPERT_ROWS
    blk_first = jnp.arange(n_blocks, dtype=jnp.int32) * EXPERT_ROWS
    blk_e = jnp.minimum(jnp.sum(ends[None, :] <= blk_first[:, None], axis=1), N_EXPERTS - 1).astype(jnp.int32)
    n_used = (ends[-1:] // EXPERT_ROWS).astype(jnp.int32)

    dest_kt = dest.T
    buf = _sc_scatter_rows(h2.reshape(T, D // 2), dest_kt, n_rows)

    b1 = jnp.concatenate([moe_b1[:, 0::2], moe_b1[:, 1::2]], axis=1)
    ybuf = _experts(blk_e, n_used, buf, moe_w1, b1.reshape(N_EXPERTS, 1, -1), moe_w2, moe_b2.reshape(N_EXPERTS, 1, D))
    yg = _sc_gather_rows(ybuf, dest_kt.reshape(-1)).reshape(TOP_K, B, S, D // 2)
    return _combine(x1, yg, rt, mod3)


def kernel(x, c, positions, ada_w, ada_b, norm1_g, w_in, q_norm_g, k_norm_g, cmp_pe_k, cmp_pe_v, cmp_wk1, cmp_wk2, cmp_wv1, cmp_wv2, conv_w, conv_b, lru_wa, lru_ba, lru_wx, lru_bx, lru_lambda, w_branch_out, w_out, norm2_g, router_w, router_b, moe_w1, moe_b1, moe_w2, moe_b2):
    inv = ROPE_THETA ** (-jnp.arange(0, HEAD_DIM, 2, dtype=F32) / HEAD_DIM)
    ang = positions.astype(F32)[..., None] * inv
    ang = jnp.concatenate([ang, ang], axis=-1)
    for l in range(ada_w.shape[0]):
        x = _layer(x, c, ang, ada_w[l], ada_b[l], norm1_g[l], w_in[l], q_norm_g[l], k_norm_g[l], cmp_pe_k[l],
                   cmp_pe_v[l], cmp_wk1[l], cmp_wk2[l], cmp_wv1[l], cmp_wv2[l], conv_w[l], conv_b[l], lru_wa[l],
                   lru_ba[l], lru_wx[l], lru_bx[l], lru_lambda[l], w_branch_out[l], w_out[l], norm2_g[l], router_w[l],
                   router_b[l], moe_w1[l], moe_b1[l], moe_w2[l], moe_b2[l])
    return x
```

```python
import functools
import math

import jax
import jax.numpy as jnp
import numpy as np
from jax import lax
from jax.experimental import pallas as pl
from jax.experimental.pallas import tpu as pltpu
from jax.experimental.pallas import tpu_sc as plsc

F32 = jnp.float32
BF16 = jnp.bfloat16

N_HEADS = 8
HEAD_DIM = 128
N_KV_GROUPS = 2
HEADS_PER_GROUP = N_HEADS // N_KV_GROUPS
N_NSA_BRANCHES = 3
CMP_BLOCK = 32
CMP_STRIDE = 16
CMP_HIDDEN = 256
SEL_BLOCK = 64
N_SELECT = 8
WINDOW = 512
ROPE_THETA = 10000.0
LRU_BLOCKS = 4
CONV_WIDTH = 4
LRU_C = 8.0
N_EXPERTS = 32
TOP_K = 4
SWIGLU_LIMIT = 7.0
SWIGLU_ALPHA = 1.702
RMS_EPS = 1e-6
NEG_INF = -1e30
LOWEST = -3.0e38

LANES = 128
SUBLANES = 8
VMEM_LIMIT = 56 * 1024 * 1024
SC_CORES = 2
SC_SUBCORES = 16
SC_CHUNK = 64

Q_TILE = 256
KEY_TILE = 128
SLC_TILE = 512
ONES_ROWS = 16
IN_ROWS = 512
LRU_ROWS = 256
EXPERT_ROWS = 512
COMBINE_ROWS = 256


def _sigmoid(v):
    return 1.0 / (1.0 + jnp.exp(-v))


def _gelu_tanh(v):
    return 0.5 * v * (1.0 + jnp.tanh(math.sqrt(2.0 / math.pi) * (v + 0.044715 * (v * v * v))))


def _rms(v, g):
    return v * lax.rsqrt(jnp.mean(v * v, axis=-1, keepdims=True) + RMS_EPS) * g


def _dot(a, b, **kw):
    return jnp.dot(a, b, preferred_element_type=F32, **kw)


def _dot_nt(a, b, **kw):
    return lax.dot_general(a, b, (((1,), (1,)), ((), ())), preferred_element_type=F32, **kw)


def _pack_bf16_pairs(v):
    n = v.shape[1] // 2
    lo = lax.bitcast_convert_type(v[:, 0:n].astype(BF16).astype(F32), jnp.int32)
    hi = lax.bitcast_convert_type(v[:, n:2 * n].astype(BF16).astype(F32), jnp.int32)
    return lax.shift_right_logical(lo, 16) | hi


def _unpack_bf16_pairs(w):
    lo = lax.bitcast_convert_type(lax.shift_left(w, 16), F32)
    hi = lax.bitcast_convert_type(w & jnp.int32(-65536), F32)
    return jnp.concatenate([lo, hi], axis=1)


def _full(shape):
    nd = len(shape)
    return pl.BlockSpec(shape, lambda *_: (0,) * nd)


def _params(sem):
    return pltpu.CompilerParams(dimension_semantics=sem, vmem_limit_bytes=VMEM_LIMIT)


def _ada_kernel(c_ref, w_ref, b_ref, o_ref):
    c = c_ref[...]
    o_ref[...] = _dot(c * _sigmoid(c), w_ref[...], precision=lax.Precision.HIGHEST) + b_ref[...]


def _ada(c, ada_w, ada_b):
    B, D = c.shape
    N = ada_w.shape[1]
    return pl.pallas_call(
        _ada_kernel,
        grid=(N // D,),
        in_specs=[_full((B, D)), pl.BlockSpec((D, D), lambda j: (0, j)), pl.BlockSpec((1, D), lambda j: (0, j))],
        out_specs=pl.BlockSpec((B, D), lambda j: (0, j)),
        out_shape=jax.ShapeDtypeStruct((B, N), F32),
        compiler_params=_params(("arbitrary",)),
        name="ada",
    )(c, ada_w, ada_b.reshape(1, N))


def _in_kernel(x_ref, mod_ref, g1_ref, ang_ref, qg_ref, kg_ref, wq_ref, wkc_ref, wkr_ref, wg_ref, wlx_ref, wlg_ref,
               wm_ref, qn_ref, qr_ref, kvc_ref, kvr_ref, gt_ref, lx_ref, lg_ref, mg_ref):
    x = x_ref[...]
    shift1 = mod_ref[0:1, :]
    scale1 = mod_ref[1:2, :]
    h = _rms(x, g1_ref[...]) * (1.0 + scale1) + shift1
    hb = h.astype(BF16)

    ang = ang_ref[...]
    cos = jnp.cos(ang)
    sin = jnp.sin(ang)
    lane = lax.broadcasted_iota(jnp.int32, ang.shape, 1)
    sin_signed = jnp.where(lane < HEAD_DIM // 2, -sin, sin)

    def rope(v):
        return v * cos + pltpu.roll(v, HEAD_DIM // 2, 1) * sin_signed

    q = _dot(hb, wq_ref[...])
    for hh in range(N_HEADS):
        sl = slice(hh * HEAD_DIM, (hh + 1) * HEAD_DIM)
        qh = _rms(q[:, sl], qg_ref[...])
        qn_ref[:, sl] = (qh * HEAD_DIM ** -0.5).astype(BF16)
        qr_ref[:, sl] = (rope(qh) * HEAD_DIM ** -0.5).astype(BF16)

    kvc_ref[...] = _dot(hb, wkc_ref[...]).astype(BF16)

    kvr = _dot(hb, wkr_ref[...])
    kvw = N_KV_GROUPS * HEAD_DIM
    for part in range(4):
        for gg in range(N_KV_GROUPS):
            sl = slice(part * kvw + gg * HEAD_DIM, part * kvw + (gg + 1) * HEAD_DIM)
            v = kvr[:, sl]
            if part % 2 == 0:
                row = 1 + part // 2
                v = rope(_rms(v, kg_ref[row:row + 1, :]))
            kvr_ref[:, sl] = v.astype(BF16)

    gt_ref[...] = _sigmoid(_dot(hb, wg_ref[...]))
    lx_ref[...] = _dot(hb, wlx_ref[...]).astype(BF16)
    lg_ref[...] = _dot(hb, wlg_ref[...]).astype(BF16)
    mg_ref[...] = _sigmoid(_dot(hb, wm_ref[...])).astype(BF16)


def _in_proj(x, mod3, norm1_g, ang, q_norm_g, k_norm_g, wq, wkc, wkr, wg, wlx, wlg, wm):
    B, S, D = x.shape
    tm = min(IN_ROWS, S)
    row = lambda w: pl.BlockSpec((None, tm, w), lambda b, s: (b, s, 0))
    widths = (wq.shape[1], wq.shape[1], wkc.shape[1], wkr.shape[1], wg.shape[1], wlx.shape[1], wlg.shape[1], wm.shape[1])
    dtypes = (BF16, BF16, BF16, BF16, F32, BF16, BF16, BF16)
    weights = (wq, wkc, wkr, wg, wlx, wlg, wm)
    return pl.pallas_call(
        _in_kernel,
        grid=(B, S // tm),
        in_specs=[row(D), pl.BlockSpec((None, 6, D), lambda b, s: (b, 0, 0)), _full((1, D)), row(HEAD_DIM),
                  _full((1, HEAD_DIM)), _full(k_norm_g.shape)] + [_full(w.shape) for w in weights],
        out_specs=[row(w) for w in widths],
        out_shape=[jax.ShapeDtypeStruct((B, S, w), dt) for w, dt in zip(widths, dtypes)],
        compiler_params=_params(("arbitrary", "arbitrary")),
        name="in_proj",
    )(x, mod3, norm1_g.reshape(1, D), ang, q_norm_g.reshape(1, HEAD_DIM), k_norm_g, *weights)


def _cmp_kernel(ak_ref, av_ref, pek_ref, pev_ref, wk1_ref, wk2_ref, wv1_ref, wv2_ref, kg_ref, kc_ref, vc_ref):
    half = CMP_STRIDE * HEAD_DIM

    def compress(a_ref, pe_ref, w1_ref, w2_ref):
        a = a_ref[...]
        u = _dot(a, w1_ref[0:half, :])
        v = _dot(a, w1_ref[half:2 * half, :])
        pw = _dot(pe_ref[...], w1_ref[...])
        pre = u + pltpu.roll(v, v.shape[0] - 1, 0) + pw[0:1, :]
        return _dot(_gelu_tanh(pre).astype(BF16), w2_ref[...])

    kc_ref[...] = _rms(compress(ak_ref, pek_ref, wk1_ref, wk2_ref), kg_ref[0:1, :]).astype(BF16)
    vc_ref[...] = compress(av_ref, pev_ref, wv1_ref, wv2_ref).T.astype(BF16)


def _compress(a_k, a_v, pe_k, pe_v, wk1, wk2, wv1, wv2, k_norm_g):
    B, G, NC, W = a_k.shape
    blk = pl.BlockSpec((None, None, NC, W), lambda b, g: (b, g, 0, 0))
    out = pl.BlockSpec((None, None, NC, HEAD_DIM), lambda b, g: (b, g, 0, 0))
    consts = (pe_k, pe_v, wk1, wk2, wv1, wv2, k_norm_g)
    return pl.pallas_call(
        _cmp_kernel,
        grid=(B, G),
        in_specs=[blk, blk] + [_full(a.shape) for a in consts],
        out_specs=[out, out],
        out_shape=[jax.ShapeDtypeStruct((B, G, NC, HEAD_DIM), BF16)] * 2,
        compiler_params=_params(("arbitrary", "arbitrary")),
        name="compress",
    )(a_k, a_v, *consts)


def _nsa_kernel(qn_ref, qr_ref, kc_ref, vc_ref, ks_ref, vs_ref, kw_ref, vw_ref, gt_ref, ov_ref, ex_ref, o_ref,
                vst_sc, vwt_sc, oslc_sc):
    qi = pl.program_id(2)
    tq = qn_ref.shape[0]
    hpg = HEADS_PER_GROUP
    seq = ks_ref.shape[0]
    n_blk = seq // SEL_BLOCK

    @pl.when(qi == 0)
    def _():
        for kt in range(seq // KEY_TILE):
            rows = slice(kt * KEY_TILE, (kt + 1) * KEY_TILE)
            vst_sc[0:HEAD_DIM, rows] = vs_ref[rows, :].astype(F32).T.astype(BF16)
            vwt_sc[0:HEAD_DIM, rows] = vw_ref[rows, :].astype(F32).T.astype(BF16)
        ones = jnp.ones((vst_sc.shape[0] - HEAD_DIM, seq), BF16)
        vst_sc[HEAD_DIM:, :] = ones
        vwt_sc[HEAD_DIM:, :] = ones

    def stack(ref):
        return jnp.concatenate([ref[:, h * HEAD_DIM:(h + 1) * HEAD_DIM] for h in range(hpg)], axis=0)

    def rep(v):
        return jnp.concatenate([v] * hpg, axis=1)

    qn = stack(qn_ref)
    qr = stack(qr_ref)
    row = lax.broadcasted_iota(jnp.int32, (KEY_TILE, tq), 0)
    col = lax.broadcasted_iota(jnp.int32, (KEY_TILE, tq), 1)
    t_q = qi * tq + col

    cmask = rep(row * CMP_STRIDE + (CMP_BLOCK - 1) <= t_q)
    s = jnp.where(cmask, _dot_nt(kc_ref[...], qn), NEG_INF)
    e = jnp.exp(s - jnp.max(s, axis=0, keepdims=True))
    p = jnp.where(cmask, e * (1.0 / jnp.sum(e, axis=0, keepdims=True)), 0.0)
    o_cmp = _dot(vc_ref[...], p.astype(BF16))

    psum = p[:, 0:tq]
    for h in range(1, hpg):
        psum = psum + p[:, h * tq:(h + 1) * tq]
    imp = _dot(ov_ref[...], psum, precision=lax.Precision.HIGHEST)
    blk = row.astype(F32)
    cur = (t_q // SEL_BLOCK).astype(F32)
    forced = (blk == 0.0) | (blk == cur) | (blk == cur - 1.0)
    score = jnp.where(forced, 1e6, jnp.where(blk <= cur, imp, -1e6))
    score = jnp.where(row < n_blk, score, LOWEST)
    bias = jnp.full((KEY_TILE, tq), NEG_INF, F32)
    for _ in range(min(N_SELECT, n_blk)):
        best = jnp.max(score, axis=0, keepdims=True)
        idx = jnp.min(jnp.where(score == best, blk, float(LANES)), axis=0, keepdims=True)
        hit = blk == idx
        bias = jnp.where(hit, 0.0, bias)
        score = jnp.where(hit, LOWEST, score)
    q_aug = jnp.concatenate([qr, jnp.concatenate([bias.T.astype(BF16)] * hpg, axis=0)], axis=1)

    def softmax_pv(sc, vt):
        pr = jnp.exp((sc - jnp.max(sc, axis=0, keepdims=True)).astype(BF16))
        o = _dot(vt, pr)
        return o[0:HEAD_DIM] * (1.0 / o[HEAD_DIM:HEAD_DIM + 1])

    n_full = (qi * tq) // SLC_TILE
    for v in range(seq // SLC_TILE):
        @pl.when(n_full == v)
        def _():
            ext = (v + 1) * SLC_TILE
            k_aug = jnp.concatenate([ks_ref[0:ext, :], ex_ref[0:ext, :]], axis=1)
            kpos = v * SLC_TILE + lax.broadcasted_iota(jnp.int32, (SLC_TILE, tq), 0)
            t_s = qi * tq + lax.broadcasted_iota(jnp.int32, (SLC_TILE, tq), 1)
            causal = jnp.where(kpos <= t_s, 0.0, NEG_INF)
            for h in range(hpg):
                sc = _dot_nt(k_aug, q_aug[h * tq:(h + 1) * tq])
                last = sc[ext - SLC_TILE:ext] + causal
                sc = last if v == 0 else jnp.concatenate([sc[0:ext - SLC_TILE], last], axis=0)
                oslc_sc[:, h * tq:(h + 1) * tq] = softmax_pv(sc, vst_sc[:, 0:ext])

    span = WINDOW + tq
    w0 = pl.multiple_of(jnp.maximum(qi * tq - WINDOW, 0), KEY_TILE)
    kpos = w0 + lax.broadcasted_iota(jnp.int32, (span, tq), 0)
    t_w = qi * tq + lax.broadcasted_iota(jnp.int32, (span, tq), 1)
    window_bias = jnp.where((kpos <= t_w) & (kpos > t_w - WINDOW), 0.0, NEG_INF)
    k_win = kw_ref[pl.ds(w0, span), :]
    vt_win = vwt_sc[:, pl.ds(w0, span)]

    gt = gt_ref[...].T
    for h in range(hpg):
        cols = slice(h * tq, (h + 1) * tq)
        o_win = softmax_pv(_dot_nt(k_win, qr[cols]) + window_bias, vt_win)
        c0 = h * N_NSA_BRANCHES
        o = gt[c0:c0 + 1, :] * o_cmp[:, cols] + gt[c0 + 1:c0 + 2, :] * oslc_sc[:, cols] + gt[c0 + 2:c0 + 3, :] * o_win
        o_ref[:, h * HEAD_DIM:(h + 1) * HEAD_DIM] = o.T.astype(BF16)


def _nsa(qn, qr, kc, vc, kvr, gates, overlap, expand):
    B, S, _ = qn.shape
    G = N_KV_GROUPS
    tq = Q_TILE
    gw = HEADS_PER_GROUP * HEAD_DIM
    qspec = pl.BlockSpec((None, tq, gw), lambda b, g, i: (b, i, g))
    cspec = pl.BlockSpec((None, None) + kc.shape[2:], lambda b, g, i: (b, g, 0, 0))
    kv = lambda part: pl.BlockSpec((None, S, HEAD_DIM), lambda b, g, i: (b, 0, part * G + g))
    m = HEADS_PER_GROUP * tq
    return pl.pallas_call(
        _nsa_kernel,
        grid=(B, G, S // tq),
        in_specs=[qspec, qspec, cspec, cspec, kv(0), kv(1), kv(2), kv(3),
                  pl.BlockSpec((None, tq, LANES), lambda b, g, i: (b, i, g)), _full(overlap.shape), _full(expand.shape)],
        out_specs=qspec,
        out_shape=jax.ShapeDtypeStruct(qn.shape, BF16),
        scratch_shapes=[pltpu.VMEM((HEAD_DIM + ONES_ROWS, S), BF16), pltpu.VMEM((HEAD_DIM + ONES_ROWS, S), BF16),
                        pltpu.VMEM((HEAD_DIM, m), F32)],
        compiler_params=_params(("arbitrary", "arbitrary", "arbitrary")),
        name="nsa",
    )(qn, qr, kc, vc, kvr, kvr, kvr, kvr, gates, overlap, expand)


def _lru_kernel(lx_ref, lg_ref, cw_ref, cb_ref, wa_ref, ba_ref, wx_ref, bx_ref, lam_ref, o_ref,
                xs_sc, a_sc, u_sc, h_sc, carry_sc):
    ts = lx_ref.shape[0]
    width = lx_ref.shape[1]
    bw = width // LRU_BLOCKS

    @pl.when(pl.program_id(1) == 0)
    def _():
        xs_sc[0:SUBLANES, :] = jnp.zeros((SUBLANES, width), F32)
        carry_sc[...] = jnp.zeros(carry_sc.shape, F32)

    xs_sc[SUBLANES:SUBLANES + ts, :] = lx_ref[...].astype(F32)
    xc = cb_ref[...] + cw_ref[CONV_WIDTH - 1:CONV_WIDTH, :] * xs_sc[SUBLANES:SUBLANES + ts, :]
    for d in range(1, CONV_WIDTH):
        w = cw_ref[CONV_WIDTH - 1 - d:CONV_WIDTH - d, :]
        xc = xc + w * xs_sc[SUBLANES - d:SUBLANES - d + ts, :]
    xs_sc[0:SUBLANES, :] = xs_sc[ts:ts + SUBLANES, :]

    lam = -lam_ref[...]
    neg_c_softplus = -LRU_C * (jnp.maximum(lam, 0.0) + jnp.log1p(jnp.exp(-jnp.abs(lam))))
    xcb = xc.astype(BF16)
    for blk in range(LRU_BLOCKS):
        sl = slice(blk * bw, (blk + 1) * bw)
        r = _sigmoid(_dot(xcb[:, sl], wa_ref[blk]) + ba_ref[:, sl])
        i = _sigmoid(_dot(xcb[:, sl], wx_ref[blk]) + bx_ref[:, sl])
        log_a = r * neg_c_softplus[:, sl]
        a = jnp.exp(log_a)
        a_sc[:, sl] = a
        u_sc[:, sl] = jnp.sqrt(-jnp.tanh(log_a) * (a * a + 1.0)) * (i * xc[:, sl])

    row = lax.broadcasted_iota(jnp.int32, (SUBLANES, width), 0)

    def chunk(c, h_prev):
        r0 = pl.multiple_of(c * SUBLANES, SUBLANES)
        a = a_sc[pl.ds(r0, SUBLANES), :]
        u = u_sc[pl.ds(r0, SUBLANES), :]
        for d in (1, 2, 4):
            keep = row >= d
            u = jnp.where(keep, a * pltpu.roll(u, d, 0) + u, u)
            a = jnp.where(keep, a * pltpu.roll(a, d, 0), a)
        h = a * h_prev + u
        h_sc[pl.ds(r0, SUBLANES), :] = h
        return jnp.broadcast_to(h[SUBLANES - 1:SUBLANES, :], h.shape)

    carry_sc[...] = lax.fori_loop(0, ts // SUBLANES, chunk, carry_sc[...])
    o_ref[...] = (_gelu_tanh(lg_ref[...].astype(F32)) * h_sc[...]).astype(BF16)


def _lru(lx, lg, conv_w, conv_b, wa, ba, wx, bx, lam):
    B, S, W = lx.shape
    ts = min(LRU_ROWS, S)
    row = pl.BlockSpec((None, ts, W), lambda b, s: (b, s, 0))
    consts = (conv_w, conv_b.reshape(1, W), wa, ba.reshape(1, W), wx, bx.reshape(1, W), lam.reshape(1, W))
    return pl.pallas_call(
        _lru_kernel,
        grid=(B, S // ts),
        in_specs=[row, row] + [_full(a.shape) for a in consts],
        out_specs=row,
        out_shape=jax.ShapeDtypeStruct((B, S, W), BF16),
        scratch_shapes=[pltpu.VMEM((ts + SUBLANES, W), F32), pltpu.VMEM((ts, W), F32), pltpu.VMEM((ts, W), F32),
                        pltpu.VMEM((ts, W), F32), pltpu.VMEM((SUBLANES, W), F32)],
        compiler_params=_params(("arbitrary", "arbitrary")),
        name="lru",
    )(lx, lg, *consts)


def _merge_kernel(on_ref, ol_ref, mg_ref, x_ref, mod_ref, g2_ref, wn_ref, wl_ref, wo_ref, rw_ref, rb_ref, tri_ref,
                  x1_ref, h2_ref, rt_ref, cnt_ref, carry_sc):
    first = (pl.program_id(0) == 0) & (pl.program_id(1) == 0)

    @pl.when(first)
    def _():
        carry_sc[...] = jnp.zeros(carry_sc.shape, F32)

    d = x_ref.shape[1]
    y_nsa = _dot(on_ref[...], wn_ref[...])
    y_lru = _dot(ol_ref[...], wl_ref[...])
    merged = mg_ref[:, 0:d].astype(F32) * y_nsa + mg_ref[:, d:2 * d].astype(F32) * y_lru
    gate1 = mod_ref[2:3, :]
    shift2 = mod_ref[3:4, :]
    scale2 = mod_ref[4:5, :]
    x1 = x_ref[...] + gate1 * _dot(merged.astype(BF16), wo_ref[...])
    x1_ref[...] = x1
    h2 = _rms(x1, g2_ref[...]) * (1.0 + scale2) + shift2
    h2_ref[...] = _pack_bf16_pairs(h2)

    logits = _dot_nt(rw_ref[...], h2, precision=lax.Precision.HIGHEST)[0:N_EXPERTS] + rb_ref[...]
    tm = logits.shape[1]
    eid = lax.broadcasted_iota(jnp.int32, logits.shape, 0).astype(F32)
    score = logits
    picks = []
    onehot = jnp.zeros(logits.shape, F32)
    for _ in range(TOP_K):
        best = jnp.max(score, axis=0, keepdims=True)
        idx = jnp.min(jnp.where(score == best, eid, float(LANES)), axis=0, keepdims=True)
        hit = eid == idx
        picks.append((idx, best, hit))
        onehot = jnp.where(hit, 1.0, onehot)
        score = jnp.where(hit, LOWEST, score)
    ew = [jnp.exp(v - picks[0][1]) for _, v, _ in picks]
    den = ew[0]
    for v in ew[1:]:
        den = den + v
    inv_den = 1.0 / den

    before = _dot(onehot.astype(BF16), tri_ref[...]) + carry_sc[:, 0:1]
    carry_sc[...] = carry_sc[...] + jnp.sum(onehot, axis=1, keepdims=True)
    cnt_ref[...] = carry_sc[...]

    rows = [idx for idx, _, _ in picks]
    rows += [jnp.sum(jnp.where(hit, before, 0.0), axis=0, keepdims=True) for _, _, hit in picks]
    rows += [e * inv_den for e in ew]
    rows.append(jnp.zeros((LANES - len(rows), tm), F32))
    rt_ref[...] = jnp.concatenate(rows, axis=0).T


def _merge(o_nsa, o_lru, mg, x, mod3, norm2_g, wn, wl, wo, rw, rb, tri):
    B, S, D = x.shape
    tm = tri.shape[0]
    row = lambda w: pl.BlockSpec((None, tm, w), lambda b, s: (b, s, 0))
    consts = (norm2_g.reshape(1, D), wn, wl, wo, rw, rb, tri)
    return pl.pallas_call(
        _merge_kernel,
        grid=(B, S // tm),
        in_specs=[row(D), row(D), row(2 * D), row(D), pl.BlockSpec((None, 6, D), lambda b, s: (b, 0, 0))]
                 + [_full(a.shape) for a in consts],
        out_specs=[row(D), row(D // 2), row(LANES), _full((N_EXPERTS, LANES))],
        out_shape=[jax.ShapeDtypeStruct((B, S, D), F32), jax.ShapeDtypeStruct((B, S, D // 2), jnp.int32),
                   jax.ShapeDtypeStruct((B, S, LANES), F32), jax.ShapeDtypeStruct((N_EXPERTS, LANES), F32)],
        scratch_shapes=[pltpu.VMEM((N_EXPERTS, LANES), F32)],
        compiler_params=_params(("arbitrary", "arbitrary")),
        name="merge",
    )(o_nsa, o_lru, mg, x, mod3, *consts)


def _expert_kernel(be_ref, nb_ref, x_ref, w1_ref, b1_ref, w2_ref, b2_ref, pick_ref, y_ref, w1_sc, w2_sc):
    i = pl.program_id(0)
    ff = w2_ref.shape[0]
    chunk = pick_ref.shape[0]

    @pl.when((i == 0) | (be_ref[i] != be_ref[jnp.maximum(i - 1, 0)]))
    def _():
        for c in range(2 * ff // chunk):
            r = _dot(w1_ref[:, c * chunk:(c + 1) * chunk].astype(BF16), pick_ref[...])
            half = chunk // 2
            w1_sc[:, c * half:(c + 1) * half] = r[:, 0:half].astype(BF16)
            w1_sc[:, ff + c * half:ff + (c + 1) * half] = r[:, half:chunk].astype(BF16)
        w2_sc[...] = w2_ref[...].astype(BF16)

    @pl.when(i < nb_ref[0])
    def _():
        h = _dot(_unpack_bf16_pairs(x_ref[...]).astype(BF16), w1_sc[...]) + b1_ref[...]
        x_glu = jnp.minimum(h[:, 0:ff], SWIGLU_LIMIT)
        x_lin = jnp.clip(h[:, ff:2 * ff], -SWIGLU_LIMIT, SWIGLU_LIMIT)
        act = x_glu * _sigmoid(SWIGLU_ALPHA * x_glu) * (x_lin + 1.0)
        y_ref[...] = _pack_bf16_pairs(_dot(act.astype(BF16), w2_sc[...]) + b2_ref[...])

    @pl.when(i >= nb_ref[0])
    def _():
        y_ref[...] = jnp.zeros(y_ref.shape, jnp.int32)


def _experts(blk_e, n_used, buf, w1, b1, w2, b2):
    n_rows = buf.shape[0]
    F, D = w2.shape[1], w2.shape[2]
    tm = EXPERT_ROWS
    chunk = 2 * LANES
    pick = np.zeros((chunk, chunk), np.float32)
    pick[2 * np.arange(LANES), np.arange(LANES)] = 1.0
    pick[2 * np.arange(LANES) + 1, LANES + np.arange(LANES)] = 1.0
    wspec = lambda a, b: pl.BlockSpec((None, a, b), lambda i, be, nb: (be[i], 0, 0))
    return pl.pallas_call(
        _expert_kernel,
        grid_spec=pltpu.PrefetchScalarGridSpec(
            num_scalar_prefetch=2,
            grid=(n_rows // tm,),
            in_specs=[pl.BlockSpec((tm, D // 2), lambda i, be, nb: (i, 0)), wspec(D, 2 * F), wspec(1, 2 * F), wspec(F, D),
                      wspec(1, D), pl.BlockSpec((chunk, chunk), lambda i, be, nb: (0, 0))],
            out_specs=pl.BlockSpec((tm, D // 2), lambda i, be, nb: (i, 0)),
            scratch_shapes=[pltpu.VMEM((D, 2 * F), BF16), pltpu.VMEM((F, D), BF16)],
        ),
        out_shape=jax.ShapeDtypeStruct((n_rows, D // 2), jnp.int32),
        compiler_params=_params(("arbitrary",)),
        name="experts",
    )(blk_e, n_used, buf, w1, b1, w2, b2, jnp.asarray(pick, dtype=BF16))


def _sc_mesh():
    return plsc.VectorSubcoreMesh(core_axis_name="c", subcore_axis_name="s")


def _sc_worker():
    return lax.axis_index("s") * SC_CORES + lax.axis_index("c")


def _sc_scatter_rows(rows, dest, n_rows):
    T, W = rows.shape
    K = dest.shape[0]
    n_workers = SC_CORES * SC_SUBCORES
    per_w = T // n_workers
    ch = SC_CHUNK
    n_ch = per_w // ch
    assert per_w * n_workers == T and n_ch * ch == per_w and n_ch % 2 == 0
    dest4 = dest.reshape(K, n_workers, n_ch, ch).transpose(1, 2, 0, 3)

    @functools.partial(
        pl.kernel, mesh=_sc_mesh(), out_type=jax.ShapeDtypeStruct((n_rows, W), rows.dtype),
        scratch_types=[pltpu.VMEM((n_ch, K, ch), jnp.int32), pltpu.VMEM((2, ch, W), rows.dtype),
                       pltpu.SemaphoreType.DMA((2,)), pltpu.SemaphoreType.DMA((2,))])
    def scatter(rows_hbm, dest_hbm, out_hbm, idx_v, rows_v, read_sem, write_sem):
        wid = _sc_worker()
        base = wid * per_w
        pltpu.sync_copy(dest_hbm.at[wid], idx_v)

        def read(c, b):
            src = rows_hbm.at[pl.ds(pl.multiple_of(base + c * ch, ch), ch)]
            return pltpu.make_async_copy(src, rows_v.at[b], read_sem.at[b])

        def write(c, b, k):
            return pltpu.make_async_copy(rows_v.at[b], out_hbm.at[idx_v.at[c, k]], write_sem.at[b])

        read(0, 0).start()

        @pl.loop(0, n_ch, step=2)
        def _(i):
            for b in range(2):
                c = i + b
                read(c, b).wait()

                @pl.when(c >= 1)
                def _():
                    for k in range(K):
                        write(c - 1, 1 - b, k).wait()

                @pl.when(c + 1 < n_ch)
                def _():
                    read(c + 1, 1 - b).start()

                for k in range(K):
                    write(c, b, k).start()

        for k in range(K):
            write(n_ch - 1, 1, k).wait()

    return scatter(rows, dest4)


def _sc_gather_rows(table, idx):
    N = idx.shape[0]
    W = table.shape[1]
    n_workers = SC_CORES * SC_SUBCORES
    per_w = N // n_workers
    ch = SC_CHUNK
    n_ch = per_w // ch
    assert per_w * n_workers == N and n_ch * ch == per_w and n_ch % 2 == 0
    idx3 = idx.reshape(n_workers, n_ch, ch)

    @functools.partial(
        pl.kernel, mesh=_sc_mesh(), out_type=jax.ShapeDtypeStruct((N, W), table.dtype),
        scratch_types=[pltpu.VMEM((n_ch, ch), jnp.int32), pltpu.VMEM((2, ch, W), table.dtype),
                       pltpu.SemaphoreType.DMA((2,)), pltpu.SemaphoreType.DMA((2,))])
    def gather(table_hbm, idx_hbm, out_hbm, idx_v, rows_v, gather_sem, write_sem):
        wid = _sc_worker()
        base = wid * per_w
        pltpu.sync_copy(idx_hbm.at[wid], idx_v)

        def fetch(c, b):
            return pltpu.make_async_copy(table_hbm.at[idx_v.at[c]], rows_v.at[b], gather_sem.at[b])

        def write(c, b):
            dst = out_hbm.at[pl.ds(pl.multiple_of(base + c * ch, ch), ch)]
            return pltpu.make_async_copy(rows_v.at[b], dst, write_sem.at[b])

        fetch(0, 0).start()

        @pl.loop(0, n_ch, step=2)
        def _(i):
            for b in range(2):
                c = i + b
                fetch(c, b).wait()

                @pl.when(c >= 1)
                def _():
                    write(c - 1, 1 - b).wait()

                @pl.when(c + 1 < n_ch)
                def _():
                    fetch(c + 1, 1 - b).start()

                write(c, b).start()

        write(n_ch - 1, 1).wait()

    return gather(table, idx3)


def _combine_kernel(x1_ref, yg_ref, rt_ref, mod_ref, o_ref):
    gate2 = mod_ref[5:6, :]
    rt = rt_ref[...]
    acc = rt[:, 2 * TOP_K:2 * TOP_K + 1] * _unpack_bf16_pairs(yg_ref[0])
    for k in range(1, TOP_K):
        acc = acc + rt[:, 2 * TOP_K + k:2 * TOP_K + k + 1] * _unpack_bf16_pairs(yg_ref[k])
    o_ref[...] = x1_ref[...] + gate2 * acc


def _combine(x1, yg, rt, mod3):
    B, S, D = x1.shape
    tm = min(COMBINE_ROWS, S)
    row = lambda w: pl.BlockSpec((None, tm, w), lambda b, s: (b, s, 0))
    return pl.pallas_call(
        _combine_kernel,
        grid=(B, S // tm),
        in_specs=[row(D), pl.BlockSpec((TOP_K, None, tm, D // 2), lambda b, s: (0, b, s, 0)), row(LANES),
                  pl.BlockSpec((None, 6, D), lambda b, s: (b, 0, 0))],
        out_specs=row(D),
        out_shape=jax.ShapeDtypeStruct((B, S, D), F32),
        compiler_params=_params(("arbitrary", "arbitrary")),
        name="combine",
    )(x1, yg, rt, mod3)


def _overlap_matrix(n_cmp_pad, n_blk):
    cs = np.arange(n_cmp_pad)[:, None] * CMP_STRIDE
    js = np.arange(LANES)[None, :] * SEL_BLOCK
    m = (cs <= js + SEL_BLOCK - 1) & (cs + CMP_BLOCK - 1 >= js) & (np.arange(LANES)[None, :] < n_blk)
    return jnp.asarray(m.astype(np.float32).T)


def _block_onehot(seq):
    key = np.arange(seq)[:, None]
    j = np.arange(LANES)[None, :]
    return jnp.asarray((j == key // SEL_BLOCK).astype(np.float32), dtype=BF16)


def _layer(x, c, ang, ada_w, ada_b, norm1_g, w_in, q_norm_g, k_norm_g, cmp_pe_k, cmp_pe_v, cmp_wk1, cmp_wk2, cmp_wv1,
           cmp_wv2, conv_w, conv_b, lru_wa, lru_ba, lru_wx, lru_bx, lru_lambda, w_branch_out, w_out, norm2_g,
           router_w, router_b, moe_w1, moe_b1, moe_w2, moe_b2):
    B, S, D = x.shape
    T = B * S
    G = N_KV_GROUPS
    nsa_w = N_HEADS * HEAD_DIM
    kv_w = G * HEAD_DIM

    mod3 = _ada(c, ada_w, ada_b).reshape(B, 6, D)

    o = 0
    wq = w_in[:, o:o + nsa_w]; o += nsa_w
    wkc = w_in[:, o:o + 2 * kv_w]; o += 2 * kv_w
    wkr = w_in[:, o:o + 4 * kv_w]; o += 4 * kv_w
    n_gate = N_HEADS * N_NSA_BRANCHES
    wg_raw = w_in[:, o:o + n_gate].reshape(D, G, n_gate // G); o += n_gate
    wg = jnp.pad(wg_raw, ((0, 0), (0, 0), (0, LANES - n_gate // G))).reshape(D, G * LANES)
    wlx = w_in[:, o:o + D]; o += D
    wlg = w_in[:, o:o + D]; o += D
    wm = w_in[:, o:o + 2 * D]
    bf = lambda a: a.astype(BF16)

    qn, qr, kvc, kvr, gates, lx, lg, mg = _in_proj(x, mod3, norm1_g, ang, q_norm_g, k_norm_g, bf(wq), bf(wkc), bf(wkr),
                                                   bf(wg), bf(wlx), bf(wlg), bf(wm))

    nc = S // CMP_STRIDE
    a = kvc.reshape(B, nc, CMP_STRIDE, 2, G, HEAD_DIM).transpose(3, 0, 4, 1, 2, 5).reshape(2, B, G, nc, CMP_STRIDE * HEAD_DIM)
    pe = lambda p: jnp.broadcast_to(p.reshape(1, CMP_BLOCK * HEAD_DIM), (SUBLANES, CMP_BLOCK * HEAD_DIM)).astype(BF16)
    kc, vc = _compress(a[0], a[1], pe(cmp_pe_k), pe(cmp_pe_v), bf(cmp_wk1), bf(cmp_wk2), bf(cmp_wv1), bf(cmp_wv2),
                       k_norm_g)

    o_nsa = _nsa(qn, qr, kc, vc, kvr, gates, _overlap_matrix(nc, S // SEL_BLOCK), _block_onehot(S))
    o_lru = _lru(lx, lg, conv_w, conv_b, bf(lru_wa), lru_ba, bf(lru_wx), lru_bx, lru_lambda)

    tm = min(IN_ROWS, S)
    tri = jnp.asarray(np.triu(np.ones((tm, tm), np.float32), 1), dtype=BF16)
    rw = jnp.pad(router_w.T, ((0, LANES - N_EXPERTS), (0, 0)))
    rb = router_b.reshape(N_EXPERTS, 1)
    x1, h2, rt, cnt = _merge(o_nsa, o_lru, mg, x, mod3, norm2_g, bf(w_branch_out[:nsa_w]), bf(w_branch_out[nsa_w:]),
                             bf(w_out), rw, rb, tri)

    rt2 = rt.reshape(T, LANES)
    top_e = rt2[:, 0:TOP_K].astype(jnp.int32)
    rank = rt2[:, TOP_K:2 * TOP_K].astype(jnp.int32)
    counts = cnt[:, 0].astype(jnp.int32)
    padded = (counts + EXPERT_ROWS - 1) // EXPERT_ROWS * EXPERT_ROWS
    ends = jnp.cumsum(padded)
    start = ends - padded
    dest = start[top_e] + rank
    n_rows = -(-(T * TOP_K + N_EXPERTS * (EXPERT_ROWS - 1)) // EXPERT_ROWS) * EXPERT_ROWS
    n_blocks = n_rows // EXPERT_ROWS
    blk_first = jnp.arange(n_blocks, dtype=jnp.int32) * EXPERT_ROWS
    blk_e = jnp.minimum(jnp.sum(ends[None, :] <= blk_first[:, None], axis=1), N_EXPERTS - 1).astype(jnp.int32)
    n_used = (ends[-1:] // EXPERT_ROWS).astype(jnp.int32)

    dest_kt = dest.T
    buf = _sc_scatter_rows(h2.reshape(T, D // 2), dest_kt, n_rows)

    b1 = jnp.concatenate([moe_b1[:, 0::2], moe_b1[:, 1::2]], axis=1)
    ybuf = _experts(blk_e, n_used, buf, moe_w1, b1.reshape(N_EXPERTS, 1, -1), moe_w2, moe_b2.reshape(N_EXPERTS, 1, D))
    yg = _sc_gather_rows(ybuf, dest_kt.reshape(-1)).reshape(TOP_K, B, S, D // 2)
    return _combine(x1, yg, rt, mod3)


def kernel(x, c, positions, ada_w, ada_b, norm1_g, w_in, q_norm_g, k_norm_g, cmp_pe_k, cmp_pe_v, cmp_wk1, cmp_wk2, cmp_wv1, cmp_wv2, conv_w, conv_b, lru_wa, lru_ba, lru_wx, lru_bx, lru_lambda, w_branch_out, w_out, norm2_g, router_w, router_b, moe_w1, moe_b1, moe_w2, moe_b2):
    inv = ROPE_THETA ** (-jnp.arange(0, HEAD_DIM, 2, dtype=F32) / HEAD_DIM)
    ang = positions.astype(F32)[..., None] * inv
    ang = jnp.concatenate([ang, ang], axis=-1)
    for l in range(ada_w.shape[0]):
        x = _layer(x, c, ang, ada_w[l], ada_b[l], norm1_g[l], w_in[l], q_norm_g[l], k_norm_g[l], cmp_pe_k[l],
                   cmp_pe_v[l], cmp_wk1[l], cmp_wk2[l], cmp_wv1[l], cmp_wv2[l], conv_w[l], conv_b[l], lru_wa[l],
                   lru_ba[l], lru_wx[l], lru_bx[l], lru_lambda[l], w_branch_out[l], w_out[l], norm2_g[l], router_w[l],
                   router_b[l], moe_w1[l], moe_b1[l], moe_w2[l], moe_b2[l])
    return x
```

```python
import functools
import math

import jax
import jax.numpy as jnp
import numpy as np
from jax import lax
from jax.experimental import pallas as pl
from jax.experimental.pallas import tpu as pltpu
from jax.experimental.pallas import tpu_sc as plsc

F32 = jnp.float32
BF16 = jnp.bfloat16

N_HEADS = 8
HEAD_DIM = 128
N_KV_GROUPS = 2
HEADS_PER_GROUP = N_HEADS // N_KV_GROUPS
N_NSA_BRANCHES = 3
CMP_BLOCK = 32
CMP_STRIDE = 16
CMP_HIDDEN = 256
SEL_BLOCK = 64
N_SELECT = 8
WINDOW = 512
ROPE_THETA = 10000.0
LRU_BLOCKS = 4
CONV_WIDTH = 4
LRU_C = 8.0
N_EXPERTS = 32
TOP_K = 4
SWIGLU_LIMIT = 7.0
SWIGLU_ALPHA = 1.702
RMS_EPS = 1e-6
NEG_INF = -1e30
LOWEST = -3.0e38
LOG2_E = 1.4426950408889634

LANES = 128
SUBLANES = 8
VMEM_LIMIT = 56 * 1024 * 1024
SC_CORES = 2
SC_SUBCORES = 16
SC_CHUNK = 64

Q_TILE = 256
KEY_TILE = 128
SLC_TILE = 512
ONES_ROWS = 16
HEAD_PAIR = 2
IN_ROWS = 512
LRU_ROWS = 256
EXPERT_ROWS = 512
COMBINE_ROWS = 256


def _sigmoid(v):
    return 1.0 / (1.0 + jnp.exp(-v))


def _gelu_tanh(v):
    return 0.5 * v * (1.0 + jnp.tanh(math.sqrt(2.0 / math.pi) * (v + 0.044715 * (v * v * v))))


def _rms(v, g):
    return v * lax.rsqrt(jnp.mean(v * v, axis=-1, keepdims=True) + RMS_EPS) * g


def _dot(a, b, **kw):
    return jnp.dot(a, b, preferred_element_type=F32, **kw)


def _dot_nt(a, b, **kw):
    return lax.dot_general(a, b, (((1,), (1,)), ((), ())), preferred_element_type=F32, **kw)


def _pack_bf16_pairs(v):
    n = v.shape[1] // 2
    lo = lax.bitcast_convert_type(v[:, 0:n].astype(BF16).astype(F32), jnp.int32)
    hi = lax.bitcast_convert_type(v[:, n:2 * n].astype(BF16).astype(F32), jnp.int32)
    return lax.shift_right_logical(lo, 16) | hi


def _unpack_bf16_pairs(w):
    lo = lax.bitcast_convert_type(lax.shift_left(w, 16), F32)
    hi = lax.bitcast_convert_type(w & jnp.int32(-65536), F32)
    return jnp.concatenate([lo, hi], axis=1)


def _full(shape):
    nd = len(shape)
    return pl.BlockSpec(shape, lambda *_: (0,) * nd)


def _params(sem):
    return pltpu.CompilerParams(dimension_semantics=sem, vmem_limit_bytes=VMEM_LIMIT)


def _ada_kernel(c_ref, w_ref, b_ref, o_ref):
    c = c_ref[...]
    o_ref[...] = _dot(c * _sigmoid(c), w_ref[...], precision=lax.Precision.HIGHEST) + b_ref[...]


def _ada(c, ada_w, ada_b):
    B, D = c.shape
    N = ada_w.shape[1]
    return pl.pallas_call(
        _ada_kernel,
        grid=(N // D,),
        in_specs=[_full((B, D)), pl.BlockSpec((D, D), lambda j: (0, j)), pl.BlockSpec((1, D), lambda j: (0, j))],
        out_specs=pl.BlockSpec((B, D), lambda j: (0, j)),
        out_shape=jax.ShapeDtypeStruct((B, N), F32),
        compiler_params=_params(("arbitrary",)),
        name="ada",
    )(c, ada_w, ada_b.reshape(1, N))


def _in_kernel(x_ref, mod_ref, g1_ref, ang_ref, qg_ref, kg_ref, wq_ref, wkc_ref, wkr_ref, wg_ref, wlx_ref, wlg_ref,
               wm_ref, qn_ref, qr_ref, kvc_ref, kvr_ref, gt_ref, lx_ref, lg_ref, mg_ref, kvc_sc):
    x = x_ref[...]
    shift1 = mod_ref[0:1, :]
    scale1 = mod_ref[1:2, :]
    h = _rms(x, g1_ref[...]) * (1.0 + scale1) + shift1
    hb = h.astype(BF16)

    ang = ang_ref[...]
    cos = jnp.cos(ang)
    sin = jnp.sin(ang)
    lane = lax.broadcasted_iota(jnp.int32, ang.shape, 1)
    sin_signed = jnp.where(lane < HEAD_DIM // 2, -sin, sin)

    def rope(v):
        return v * cos + pltpu.roll(v, HEAD_DIM // 2, 1) * sin_signed

    q = _dot(hb, wq_ref[...])
    for hh in range(N_HEADS):
        sl = slice(hh * HEAD_DIM, (hh + 1) * HEAD_DIM)
        qh = _rms(q[:, sl], qg_ref[...])
        qn_ref[:, sl] = (qh * HEAD_DIM ** -0.5).astype(BF16)
        qr_ref[:, sl] = (rope(qh) * (HEAD_DIM ** -0.5 * LOG2_E)).astype(BF16)

    kvc = _dot(hb, wkc_ref[...])
    for part in range(kvc_ref.shape[0]):
        kvc_sc[part] = kvc[:, part * HEAD_DIM:(part + 1) * HEAD_DIM]
        for tok in range(CMP_STRIDE):
            piece = kvc_sc[part, pl.ds(tok, kvc_ref.shape[1], stride=CMP_STRIDE), :]
            kvc_ref[part, :, tok * HEAD_DIM:(tok + 1) * HEAD_DIM] = piece.astype(BF16)

    kvr = _dot(hb, wkr_ref[...])
    kvw = N_KV_GROUPS * HEAD_DIM
    for part in range(4):
        for gg in range(N_KV_GROUPS):
            sl = slice(part * kvw + gg * HEAD_DIM, part * kvw + (gg + 1) * HEAD_DIM)
            v = kvr[:, sl]
            if part % 2 == 0:
                row = 1 + part // 2
                v = rope(_rms(v, kg_ref[row:row + 1, :]))
            kvr_ref[:, sl] = v.astype(BF16)

    gt_ref[...] = _sigmoid(_dot(hb, wg_ref[...]))
    lx_ref[...] = _dot(hb, wlx_ref[...]).astype(BF16)
    lg_ref[...] = _dot(hb, wlg_ref[...]).astype(BF16)
    mg_ref[...] = _sigmoid(_dot(hb, wm_ref[...])).astype(BF16)


def _in_proj(x, mod3, norm1_g, ang, q_norm_g, k_norm_g, wq, wkc, wkr, wg, wlx, wlg, wm):
    B, S, D = x.shape
    tm = min(IN_ROWS, S)
    row = lambda w: pl.BlockSpec((None, tm, w), lambda b, s: (b, s, 0))
    widths = (wq.shape[1], wq.shape[1], wkc.shape[1], wkr.shape[1], wg.shape[1], wlx.shape[1], wlg.shape[1], wm.shape[1])
    dtypes = (BF16, BF16, BF16, BF16, F32, BF16, BF16, BF16)
    weights = (wq, wkc, wkr, wg, wlx, wlg, wm)
    out_specs = [row(w) for w in widths]
    out_shape = [jax.ShapeDtypeStruct((B, S, w), dt) for w, dt in zip(widths, dtypes)]
    n_part = wkc.shape[1] // HEAD_DIM
    out_specs[2] = pl.BlockSpec((None, n_part, tm // CMP_STRIDE, CMP_STRIDE * HEAD_DIM), lambda b, s: (b, 0, s, 0))
    out_shape[2] = jax.ShapeDtypeStruct((B, n_part, S // CMP_STRIDE, CMP_STRIDE * HEAD_DIM), BF16)
    return pl.pallas_call(
        _in_kernel,
        grid=(B, S // tm),
        in_specs=[row(D), pl.BlockSpec((None, 6, D), lambda b, s: (b, 0, 0)), _full((1, D)), row(HEAD_DIM),
                  _full((1, HEAD_DIM)), _full(k_norm_g.shape)] + [_full(w.shape) for w in weights],
        out_specs=out_specs,
        out_shape=out_shape,
        scratch_shapes=[pltpu.VMEM((n_part, tm, HEAD_DIM), F32)],
        compiler_params=_params(("arbitrary", "arbitrary")),
        name="in_proj",
    )(x, mod3, norm1_g.reshape(1, D), ang, q_norm_g.reshape(1, HEAD_DIM), k_norm_g, *weights)


def _cmp_kernel(ak_ref, av_ref, pek_ref, pev_ref, wk1_ref, wk2_ref, wv1_ref, wv2_ref, kg_ref, kc_ref, vc_ref):
    half = CMP_STRIDE * HEAD_DIM

    def compress(a_ref, pe_ref, w1_ref, w2_ref):
        a = a_ref[...]
        u = _dot(a, w1_ref[0:half, :])
        v = _dot(a, w1_ref[half:2 * half, :])
        pw = _dot(pe_ref[...], w1_ref[...])
        pre = u + pltpu.roll(v, v.shape[0] - 1, 0) + pw[0:1, :]
        return _dot(_gelu_tanh(pre).astype(BF16), w2_ref[...])

    kc_ref[...] = _rms(compress(ak_ref, pek_ref, wk1_ref, wk2_ref), kg_ref[0:1, :]).astype(BF16)
    vc_ref[...] = compress(av_ref, pev_ref, wv1_ref, wv2_ref).T.astype(BF16)


def _compress(a, pe_k, pe_v, wk1, wk2, wv1, wv2, k_norm_g):
    B, _, NC, W = a.shape
    G = N_KV_GROUPS
    out = pl.BlockSpec((None, None, NC, HEAD_DIM), lambda b, g: (b, g, 0, 0))
    consts = (pe_k, pe_v, wk1, wk2, wv1, wv2, k_norm_g)
    return pl.pallas_call(
        _cmp_kernel,
        grid=(B, G),
        in_specs=[pl.BlockSpec((None, None, NC, W), lambda b, g: (b, g, 0, 0)),
                  pl.BlockSpec((None, None, NC, W), lambda b, g: (b, G + g, 0, 0))] + [_full(c.shape) for c in consts],
        out_specs=[out, out],
        out_shape=[jax.ShapeDtypeStruct((B, G, NC, HEAD_DIM), BF16)] * 2,
        compiler_params=_params(("arbitrary", "arbitrary")),
        name="compress",
    )(a, a, *consts)


def _nsa_kernel(qn_ref, qr_ref, kc_ref, vc_ref, ks_ref, vs_ref, kw_ref, vw_ref, gt_ref, ov_ref, ex_ref, o_ref,
                vst_sc, vwt_sc, oslc_sc):
    qi = pl.program_id(2)
    tq = qn_ref.shape[0]
    hpg = HEADS_PER_GROUP
    seq = ks_ref.shape[0]
    n_blk = seq // SEL_BLOCK

    @pl.when(qi == 0)
    def _():
        for kt in range(seq // KEY_TILE):
            rows = slice(kt * KEY_TILE, (kt + 1) * KEY_TILE)
            vst_sc[0:HEAD_DIM, rows] = vs_ref[rows, :].astype(F32).T.astype(BF16)
            vwt_sc[0:HEAD_DIM, rows] = vw_ref[rows, :].astype(F32).T.astype(BF16)
        ones = jnp.ones((vst_sc.shape[0] - HEAD_DIM, seq), BF16)
        vst_sc[HEAD_DIM:, :] = ones
        vwt_sc[HEAD_DIM:, :] = ones

    def stack(ref):
        return jnp.concatenate([ref[:, h * HEAD_DIM:(h + 1) * HEAD_DIM] for h in range(hpg)], axis=0)

    def rep(v):
        return jnp.concatenate([v] * hpg, axis=1)

    def pair(v):
        return jnp.concatenate([v] * HEAD_PAIR, axis=1)

    qn = stack(qn_ref)
    qr = stack(qr_ref)
    pair_cols = [slice(hp * HEAD_PAIR * tq, (hp + 1) * HEAD_PAIR * tq) for hp in range(hpg // HEAD_PAIR)]

    span = WINDOW + tq
    w0 = pl.multiple_of(jnp.maximum(qi * tq - WINDOW, 0), KEY_TILE)
    kpos = w0 + lax.broadcasted_iota(jnp.int32, (span, tq), 0)
    t_w = qi * tq + lax.broadcasted_iota(jnp.int32, (span, tq), 1)
    window_bias = pair(jnp.where((kpos <= t_w) & (kpos > t_w - WINDOW), 0.0, NEG_INF))
    k_win = kw_ref[pl.ds(w0, span), :]
    win_scores = [_dot_nt(k_win, qr[cols]) + window_bias for cols in pair_cols]

    row = lax.broadcasted_iota(jnp.int32, (KEY_TILE, tq), 0)
    col = lax.broadcasted_iota(jnp.int32, (KEY_TILE, tq), 1)
    t_q = qi * tq + col

    cmask = rep(row * CMP_STRIDE + (CMP_BLOCK - 1) <= t_q)
    s = jnp.where(cmask, _dot_nt(kc_ref[...], qn), NEG_INF)
    e = jnp.exp(s - jnp.max(s, axis=0, keepdims=True))
    p = jnp.where(cmask, e * (1.0 / jnp.sum(e, axis=0, keepdims=True)), 0.0)
    o_cmp = _dot(vc_ref[...], p.astype(BF16))

    psum = p[:, 0:tq]
    for h in range(1, hpg):
        psum = psum + p[:, h * tq:(h + 1) * tq]
    imp = _dot(ov_ref[...], psum, precision=lax.Precision.HIGHEST)[0:n_blk]
    blk = lax.broadcasted_iota(jnp.int32, (n_blk, tq), 0).astype(F32)
    cur = ((qi * tq + lax.broadcasted_iota(jnp.int32, (n_blk, tq), 1)) // SEL_BLOCK).astype(F32)
    forced = (blk == 0.0) | (blk == cur) | (blk == cur - 1.0)
    score = jnp.where(forced, 1e6, jnp.where(blk <= cur, imp, -1e6))
    bias = jnp.full((n_blk, tq), NEG_INF, F32)
    for _ in range(min(N_SELECT, n_blk)):
        best = jnp.max(score, axis=0, keepdims=True)
        idx = jnp.min(jnp.where(score == best, blk, float(LANES)), axis=0, keepdims=True)
        hit = blk == idx
        bias = jnp.where(hit, 0.0, bias)
        score = jnp.where(hit, LOWEST, score)
    bias = jnp.concatenate([bias, jnp.zeros((LANES - n_blk, tq), F32)], axis=0)
    q_aug = jnp.concatenate([qr, jnp.concatenate([bias.T.astype(BF16)] * hpg, axis=0)], axis=1)

    def softmax_pv(sc, vt):
        pr = jnp.exp2((sc - jnp.max(sc, axis=0, keepdims=True)).astype(BF16))
        o = _dot(vt, pr)
        return o[0:HEAD_DIM] * (1.0 / o[HEAD_DIM:HEAD_DIM + 1])

    n_full = (qi * tq) // SLC_TILE
    for v in range(seq // SLC_TILE):
        @pl.when(n_full == v)
        def _():
            ext = (v + 1) * SLC_TILE
            k_aug = jnp.concatenate([ks_ref[0:ext, :], ex_ref[0:ext, :]], axis=1)
            kpos = v * SLC_TILE + lax.broadcasted_iota(jnp.int32, (SLC_TILE, tq), 0)
            t_s = qi * tq + lax.broadcasted_iota(jnp.int32, (SLC_TILE, tq), 1)
            causal = pair(jnp.where(kpos <= t_s, 0.0, NEG_INF))
            scores = []
            for cols in pair_cols:
                sc = _dot_nt(k_aug, q_aug[cols])
                last = sc[ext - SLC_TILE:ext] + causal
                scores.append(last if v == 0 else jnp.concatenate([sc[0:ext - SLC_TILE], last], axis=0))
            for cols, sc in zip(pair_cols, scores):
                oslc_sc[:, cols] = softmax_pv(sc, vst_sc[:, 0:ext])

    vt_win = vwt_sc[:, pl.ds(w0, span)]
    gt = gt_ref[...].T
    for hp, sc in enumerate(win_scores):
        o_win = softmax_pv(sc, vt_win)
        for j in range(HEAD_PAIR):
            h = hp * HEAD_PAIR + j
            cols = slice(h * tq, (h + 1) * tq)
            c0 = h * N_NSA_BRANCHES
            o = (gt[c0:c0 + 1, :] * o_cmp[:, cols] + gt[c0 + 1:c0 + 2, :] * oslc_sc[:, cols]
                 + gt[c0 + 2:c0 + 3, :] * o_win[:, j * tq:(j + 1) * tq])
            o_ref[:, h * HEAD_DIM:(h + 1) * HEAD_DIM] = o.T.astype(BF16)


def _nsa(qn, qr, kc, vc, kvr, gates, overlap, expand):
    B, S, _ = qn.shape
    G = N_KV_GROUPS
    tq = Q_TILE
    gw = HEADS_PER_GROUP * HEAD_DIM
    qspec = pl.BlockSpec((None, tq, gw), lambda b, g, i: (b, i, g))
    cspec = pl.BlockSpec((None, None) + kc.shape[2:], lambda b, g, i: (b, g, 0, 0))
    kv = lambda part: pl.BlockSpec((None, S, HEAD_DIM), lambda b, g, i: (b, 0, part * G + g))
    m = HEADS_PER_GROUP * tq
    return pl.pallas_call(
        _nsa_kernel,
        grid=(B, G, S // tq),
        in_specs=[qspec, qspec, cspec, cspec, kv(0), kv(1), kv(2), kv(3),
                  pl.BlockSpec((None, tq, LANES), lambda b, g, i: (b, i, g)), _full(overlap.shape), _full(expand.shape)],
        out_specs=qspec,
        out_shape=jax.ShapeDtypeStruct(qn.shape, BF16),
        scratch_shapes=[pltpu.VMEM((HEAD_DIM + ONES_ROWS, S), BF16), pltpu.VMEM((HEAD_DIM + ONES_ROWS, S), BF16),
                        pltpu.VMEM((HEAD_DIM, m), F32)],
        compiler_params=_params(("arbitrary", "arbitrary", "arbitrary")),
        name="nsa",
    )(qn, qr, kc, vc, kvr, kvr, kvr, kvr, gates, overlap, expand)


def _lru_kernel(lx_ref, lg_ref, cw_ref, cb_ref, wa_ref, ba_ref, wx_ref, bx_ref, lam_ref, o_ref,
                xs_sc, a_sc, u_sc, h_sc, carry_sc):
    ts = lx_ref.shape[0]
    width = lx_ref.shape[1]
    bw = width // LRU_BLOCKS

    @pl.when(pl.program_id(1) == 0)
    def _():
        xs_sc[0:SUBLANES, :] = jnp.zeros((SUBLANES, width), F32)
        carry_sc[...] = jnp.zeros(carry_sc.shape, F32)

    xs_sc[SUBLANES:SUBLANES + ts, :] = lx_ref[...].astype(F32)
    xc = cb_ref[...] + cw_ref[CONV_WIDTH - 1:CONV_WIDTH, :] * xs_sc[SUBLANES:SUBLANES + ts, :]
    for d in range(1, CONV_WIDTH):
        w = cw_ref[CONV_WIDTH - 1 - d:CONV_WIDTH - d, :]
        xc = xc + w * xs_sc[SUBLANES - d:SUBLANES - d + ts, :]
    xs_sc[0:SUBLANES, :] = xs_sc[ts:ts + SUBLANES, :]

    lam = -lam_ref[...]
    neg_c_softplus = -LRU_C * (jnp.maximum(lam, 0.0) + jnp.log1p(jnp.exp(-jnp.abs(lam))))
    xcb = xc.astype(BF16)
    for blk in range(LRU_BLOCKS):
        sl = slice(blk * bw, (blk + 1) * bw)
        r = _sigmoid(_dot(xcb[:, sl], wa_ref[blk]) + ba_ref[:, sl])
        i = _sigmoid(_dot(xcb[:, sl], wx_ref[blk]) + bx_ref[:, sl])
        log_a = r * neg_c_softplus[:, sl]
        a = jnp.exp(log_a)
        a_sc[:, sl] = a
        u_sc[:, sl] = jnp.sqrt(-jnp.tanh(log_a) * (a * a + 1.0)) * (i * xc[:, sl])

    row = lax.broadcasted_iota(jnp.int32, (SUBLANES, width), 0)

    def chunk(c, h_prev):
        r0 = pl.multiple_of(c * SUBLANES, SUBLANES)
        a = a_sc[pl.ds(r0, SUBLANES), :]
        u = u_sc[pl.ds(r0, SUBLANES), :]
        for d in (1, 2, 4):
            keep = row >= d
            u = jnp.where(keep, a * pltpu.roll(u, d, 0) + u, u)
            a = jnp.where(keep, a * pltpu.roll(a, d, 0), a)
        h = a * h_prev + u
        h_sc[pl.ds(r0, SUBLANES), :] = h
        return jnp.broadcast_to(h[SUBLANES - 1:SUBLANES, :], h.shape)

    carry_sc[...] = lax.fori_loop(0, ts // SUBLANES, chunk, carry_sc[...])
    o_ref[...] = (_gelu_tanh(lg_ref[...].astype(F32)) * h_sc[...]).astype(BF16)


def _lru(lx, lg, conv_w, conv_b, wa, ba, wx, bx, lam):
    B, S, W = lx.shape
    ts = min(LRU_ROWS, S)
    row = pl.BlockSpec((None, ts, W), lambda b, s: (b, s, 0))
    consts = (conv_w, conv_b.reshape(1, W), wa, ba.reshape(1, W), wx, bx.reshape(1, W), lam.reshape(1, W))
    return pl.pallas_call(
        _lru_kernel,
        grid=(B, S // ts),
        in_specs=[row, row] + [_full(a.shape) for a in consts],
        out_specs=row,
        out_shape=jax.ShapeDtypeStruct((B, S, W), BF16),
        scratch_shapes=[pltpu.VMEM((ts + SUBLANES, W), F32), pltpu.VMEM((ts, W), F32), pltpu.VMEM((ts, W), F32),
                        pltpu.VMEM((ts, W), F32), pltpu.VMEM((SUBLANES, W), F32)],
        compiler_params=_params(("arbitrary", "arbitrary")),
        name="lru",
    )(lx, lg, *consts)


def _merge_kernel(on_ref, ol_ref, mg_ref, x_ref, mod_ref, g2_ref, wn_ref, wl_ref, wo_ref, rw_ref, rb_ref, tri_ref,
                  x1_ref, h2_ref, rt_ref, cnt_ref, carry_sc):
    first = (pl.program_id(0) == 0) & (pl.program_id(1) == 0)

    @pl.when(first)
    def _():
        carry_sc[...] = jnp.zeros(carry_sc.shape, F32)

    d = x_ref.shape[1]
    y_nsa = _dot(on_ref[...], wn_ref[...])
    y_lru = _dot(ol_ref[...], wl_ref[...])
    merged = mg_ref[:, 0:d].astype(F32) * y_nsa + mg_ref[:, d:2 * d].astype(F32) * y_lru
    gate1 = mod_ref[2:3, :]
    shift2 = mod_ref[3:4, :]
    scale2 = mod_ref[4:5, :]
    x1 = x_ref[...] + gate1 * _dot(merged.astype(BF16), wo_ref[...])
    x1_ref[...] = x1
    h2 = _rms(x1, g2_ref[...]) * (1.0 + scale2) + shift2
    h2_ref[...] = _pack_bf16_pairs(h2)

    logits = _dot_nt(rw_ref[...], h2, precision=lax.Precision.HIGHEST)[0:N_EXPERTS] + rb_ref[...]
    tm = logits.shape[1]
    eid = lax.broadcasted_iota(jnp.int32, logits.shape, 0).astype(F32)
    score = logits
    picks = []
    onehot = jnp.zeros(logits.shape, F32)
    for _ in range(TOP_K):
        best = jnp.max(score, axis=0, keepdims=True)
        idx = jnp.min(jnp.where(score == best, eid, float(LANES)), axis=0, keepdims=True)
        hit = eid == idx
        picks.append((idx, best, hit))
        onehot = jnp.where(hit, 1.0, onehot)
        score = jnp.where(hit, LOWEST, score)
    ew = [jnp.exp(v - picks[0][1]) for _, v, _ in picks]
    den = ew[0]
    for v in ew[1:]:
        den = den + v
    inv_den = 1.0 / den

    before = _dot(onehot.astype(BF16), tri_ref[...]) + carry_sc[:, 0:1]
    carry_sc[...] = carry_sc[...] + jnp.sum(onehot, axis=1, keepdims=True)
    cnt_ref[...] = carry_sc[...]

    rows = [idx for idx, _, _ in picks]
    rows += [jnp.sum(jnp.where(hit, before, 0.0), axis=0, keepdims=True) for _, _, hit in picks]
    rows += [e * inv_den for e in ew]
    rows.append(jnp.zeros((LANES - len(rows), tm), F32))
    rt_ref[...] = jnp.concatenate(rows, axis=0).T


def _merge(o_nsa, o_lru, mg, x, mod3, norm2_g, wn, wl, wo, rw, rb, tri):
    B, S, D = x.shape
    tm = tri.shape[0]
    row = lambda w: pl.BlockSpec((None, tm, w), lambda b, s: (b, s, 0))
    consts = (norm2_g.reshape(1, D), wn, wl, wo, rw, rb, tri)
    return pl.pallas_call(
        _merge_kernel,
        grid=(B, S // tm),
        in_specs=[row(D), row(D), row(2 * D), row(D), pl.BlockSpec((None, 6, D), lambda b, s: (b, 0, 0))]
                 + [_full(a.shape) for a in consts],
        out_specs=[row(D), row(D // 2), row(LANES), _full((N_EXPERTS, LANES))],
        out_shape=[jax.ShapeDtypeStruct((B, S, D), F32), jax.ShapeDtypeStruct((B, S, D // 2), jnp.int32),
                   jax.ShapeDtypeStruct((B, S, LANES), F32), jax.ShapeDtypeStruct((N_EXPERTS, LANES), F32)],
        scratch_shapes=[pltpu.VMEM((N_EXPERTS, LANES), F32)],
        compiler_params=_params(("arbitrary", "arbitrary")),
        name="merge",
    )(o_nsa, o_lru, mg, x, mod3, *consts)


def _expert_kernel(be_ref, nb_ref, x_ref, w1_ref, b1_ref, w2_ref, b2_ref, pick_ref, y_ref, w1_sc, w2_sc):
    i = pl.program_id(0)
    ff = w2_ref.shape[0]
    chunk = pick_ref.shape[0]

    @pl.when((i == 0) | (be_ref[i] != be_ref[jnp.maximum(i - 1, 0)]))
    def _():
        for c in range(2 * ff // chunk):
            r = _dot(w1_ref[:, c * chunk:(c + 1) * chunk].astype(BF16), pick_ref[...])
            half = chunk // 2
            w1_sc[:, c * half:(c + 1) * half] = r[:, 0:half].astype(BF16)
            w1_sc[:, ff + c * half:ff + (c + 1) * half] = r[:, half:chunk].astype(BF16)
        w2_sc[...] = w2_ref[...].astype(BF16)

    @pl.when(i < nb_ref[0])
    def _():
        h = _dot(_unpack_bf16_pairs(x_ref[...]).astype(BF16), w1_sc[...]) + b1_ref[...]
        x_glu = jnp.minimum(h[:, 0:ff], SWIGLU_LIMIT)
        x_lin = jnp.clip(h[:, ff:2 * ff], -SWIGLU_LIMIT, SWIGLU_LIMIT)
        act = x_glu * _sigmoid(SWIGLU_ALPHA * x_glu) * (x_lin + 1.0)
        y_ref[...] = _pack_bf16_pairs(_dot(act.astype(BF16), w2_sc[...]) + b2_ref[...])

    @pl.when(i >= nb_ref[0])
    def _():
        y_ref[...] = jnp.zeros(y_ref.shape, jnp.int32)


def _experts(blk_e, n_used, buf, w1, b1, w2, b2):
    n_rows = buf.shape[0]
    F, D = w2.shape[1], w2.shape[2]
    tm = EXPERT_ROWS
    chunk = 2 * LANES
    pick = np.zeros((chunk, chunk), np.float32)
    pick[2 * np.arange(LANES), np.arange(LANES)] = 1.0
    pick[2 * np.arange(LANES) + 1, LANES + np.arange(LANES)] = 1.0
    wspec = lambda a, b: pl.BlockSpec((None, a, b), lambda i, be, nb: (be[i], 0, 0))
    return pl.pallas_call(
        _expert_kernel,
        grid_spec=pltpu.PrefetchScalarGridSpec(
            num_scalar_prefetch=2,
            grid=(n_rows // tm,),
            in_specs=[pl.BlockSpec((tm, D // 2), lambda i, be, nb: (i, 0)), wspec(D, 2 * F), wspec(1, 2 * F), wspec(F, D),
                      wspec(1, D), pl.BlockSpec((chunk, chunk), lambda i, be, nb: (0, 0))],
            out_specs=pl.BlockSpec((tm, D // 2), lambda i, be, nb: (i, 0)),
            scratch_shapes=[pltpu.VMEM((D, 2 * F), BF16), pltpu.VMEM((F, D), BF16)],
        ),
        out_shape=jax.ShapeDtypeStruct((n_rows, D // 2), jnp.int32),
        compiler_params=_params(("arbitrary",)),
        name="experts",
    )(blk_e, n_used, buf, w1, b1, w2, b2, jnp.asarray(pick, dtype=BF16))


def _sc_mesh():
    return plsc.VectorSubcoreMesh(core_axis_name="c", subcore_axis_name="s")


def _sc_worker():
    return lax.axis_index("s") * SC_CORES + lax.axis_index("c")


def _sc_scatter_rows(rows, dest, n_rows):
    T, W = rows.shape
    K = dest.shape[0]
    n_workers = SC_CORES * SC_SUBCORES
    per_w = T // n_workers
    ch = SC_CHUNK
    n_ch = per_w // ch
    assert per_w * n_workers == T and n_ch * ch == per_w and n_ch % 2 == 0
    dest4 = dest.reshape(K, n_workers, n_ch, ch).transpose(1, 2, 0, 3)

    @functools.partial(
        pl.kernel, mesh=_sc_mesh(), out_type=jax.ShapeDtypeStruct((n_rows, W), rows.dtype),
        scratch_types=[pltpu.VMEM((n_ch, K, ch), jnp.int32), pltpu.VMEM((2, ch, W), rows.dtype),
                       pltpu.SemaphoreType.DMA((2,)), pltpu.SemaphoreType.DMA((2,))])
    def scatter(rows_hbm, dest_hbm, out_hbm, idx_v, rows_v, read_sem, write_sem):
        wid = _sc_worker()
        base = wid * per_w
        pltpu.sync_copy(dest_hbm.at[wid], idx_v)

        def read(c, b):
            src = rows_hbm.at[pl.ds(pl.multiple_of(base + c * ch, ch), ch)]
            return pltpu.make_async_copy(src, rows_v.at[b], read_sem.at[b])

        def write(c, b, k):
            return pltpu.make_async_copy(rows_v.at[b], out_hbm.at[idx_v.at[c, k]], write_sem.at[b])

        read(0, 0).start()

        @pl.loop(0, n_ch, step=2)
        def _(i):
            for b in range(2):
                c = i + b
                read(c, b).wait()

                @pl.when(c >= 1)
                def _():
                    for k in range(K):
                        write(c - 1, 1 - b, k).wait()

                @pl.when(c + 1 < n_ch)
                def _():
                    read(c + 1, 1 - b).start()

                for k in range(K):
                    write(c, b, k).start()

        for k in range(K):
            write(n_ch - 1, 1, k).wait()

    return scatter(rows, dest4)


def _sc_gather_rows(table, idx):
    N = idx.shape[0]
    W = table.shape[1]
    n_workers = SC_CORES * SC_SUBCORES
    per_w = N // n_workers
    ch = SC_CHUNK
    n_ch = per_w // ch
    assert per_w * n_workers == N and n_ch * ch == per_w and n_ch % 2 == 0
    idx3 = idx.reshape(n_workers, n_ch, ch)

    @functools.partial(
        pl.kernel, mesh=_sc_mesh(), out_type=jax.ShapeDtypeStruct((N, W), table.dtype),
        scratch_types=[pltpu.VMEM((n_ch, ch), jnp.int32), pltpu.VMEM((2, ch, W), table.dtype),
                       pltpu.SemaphoreType.DMA((2,)), pltpu.SemaphoreType.DMA((2,))])
    def gather(table_hbm, idx_hbm, out_hbm, idx_v, rows_v, gather_sem, write_sem):
        wid = _sc_worker()
        base = wid * per_w
        pltpu.sync_copy(idx_hbm.at[wid], idx_v)

        def fetch(c, b):
            return pltpu.make_async_copy(table_hbm.at[idx_v.at[c]], rows_v.at[b], gather_sem.at[b])

        def write(c, b):
            dst = out_hbm.at[pl.ds(pl.multiple_of(base + c * ch, ch), ch)]
            return pltpu.make_async_copy(rows_v.at[b], dst, write_sem.at[b])

        fetch(0, 0).start()

        @pl.loop(0, n_ch, step=2)
        def _(i):
            for b in range(2):
                c = i + b
                fetch(c, b).wait()

                @pl.when(c >= 1)
                def _():
                    write(c - 1, 1 - b).wait()

                @pl.when(c + 1 < n_ch)
                def _():
                    fetch(c + 1, 1 - b).start()

                write(c, b).start()

        write(n_ch - 1, 1).wait()

    return gather(table, idx3)


def _combine_kernel(x1_ref, yg_ref, rt_ref, mod_ref, o_ref):
    gate2 = mod_ref[5:6, :]
    rt = rt_ref[...]
    acc = rt[:, 2 * TOP_K:2 * TOP_K + 1] * _unpack_bf16_pairs(yg_ref[0])
    for k in range(1, TOP_K):
        acc = acc + rt[:, 2 * TOP_K + k:2 * TOP_K + k + 1] * _unpack_bf16_pairs(yg_ref[k])
    o_ref[...] = x1_ref[...] + gate2 * acc


def _combine(x1, yg, rt, mod3):
    B, S, D = x1.shape
    tm = min(COMBINE_ROWS, S)
    row = lambda w: pl.BlockSpec((None, tm, w), lambda b, s: (b, s, 0))
    return pl.pallas_call(
        _combine_kernel,
        grid=(B, S // tm),
        in_specs=[row(D), pl.BlockSpec((TOP_K, None, tm, D // 2), lambda b, s: (0, b, s, 0)), row(LANES),
                  pl.BlockSpec((None, 6, D), lambda b, s: (b, 0, 0))],
        out_specs=row(D),
        out_shape=jax.ShapeDtypeStruct((B, S, D), F32),
        compiler_params=_params(("arbitrary", "arbitrary")),
        name="combine",
    )(x1, yg, rt, mod3)


def _overlap_matrix(n_cmp_pad, n_blk):
    cs = np.arange(n_cmp_pad)[:, None] * CMP_STRIDE
    js = np.arange(LANES)[None, :] * SEL_BLOCK
    m = (cs <= js + SEL_BLOCK - 1) & (cs + CMP_BLOCK - 1 >= js) & (np.arange(LANES)[None, :] < n_blk)
    return jnp.asarray(m.astype(np.float32).T)


def _block_onehot(seq):
    key = np.arange(seq)[:, None]
    j = np.arange(LANES)[None, :]
    return jnp.asarray((j == key // SEL_BLOCK).astype(np.float32), dtype=BF16)


def _layer(x, c, ang, ada_w, ada_b, norm1_g, w_in, q_norm_g, k_norm_g, cmp_pe_k, cmp_pe_v, cmp_wk1, cmp_wk2, cmp_wv1,
           cmp_wv2, conv_w, conv_b, lru_wa, lru_ba, lru_wx, lru_bx, lru_lambda, w_branch_out, w_out, norm2_g,
           router_w, router_b, moe_w1, moe_b1, moe_w2, moe_b2):
    B, S, D = x.shape
    T = B * S
    G = N_KV_GROUPS
    nsa_w = N_HEADS * HEAD_DIM
    kv_w = G * HEAD_DIM

    mod3 = _ada(c, ada_w, ada_b).reshape(B, 6, D)

    o = 0
    wq = w_in[:, o:o + nsa_w]; o += nsa_w
    wkc = w_in[:, o:o + 2 * kv_w]; o += 2 * kv_w
    wkr = w_in[:, o:o + 4 * kv_w]; o += 4 * kv_w
    n_gate = N_HEADS * N_NSA_BRANCHES
    wg_raw = w_in[:, o:o + n_gate].reshape(D, G, n_gate // G); o += n_gate
    wg = jnp.pad(wg_raw, ((0, 0), (0, 0), (0, LANES - n_gate // G))).reshape(D, G * LANES)
    wlx = w_in[:, o:o + D]; o += D
    wlg = w_in[:, o:o + D]; o += D
    wm = w_in[:, o:o + 2 * D]
    bf = lambda a: a.astype(BF16)

    qn, qr, kvc, kvr, gates, lx, lg, mg = _in_proj(x, mod3, norm1_g, ang, q_norm_g, k_norm_g, bf(wq), bf(wkc), bf(wkr),
                                                   bf(wg), bf(wlx), bf(wlg), bf(wm))

    nc = S // CMP_STRIDE
    pe = lambda p: jnp.broadcast_to(p.reshape(1, CMP_BLOCK * HEAD_DIM), (SUBLANES, CMP_BLOCK * HEAD_DIM)).astype(BF16)
    kc, vc = _compress(kvc, pe(cmp_pe_k), pe(cmp_pe_v), bf(cmp_wk1), bf(cmp_wk2), bf(cmp_wv1), bf(cmp_wv2), k_norm_g)

    o_nsa = _nsa(qn, qr, kc, vc, kvr, gates, _overlap_matrix(nc, S // SEL_BLOCK), _block_onehot(S))
    o_lru = _lru(lx, lg, conv_w, conv_b, bf(lru_wa), lru_ba, bf(lru_wx), lru_bx, lru_lambda)

    tm = min(IN_ROWS, S)
    tri = jnp.asarray(np.triu(np.ones((tm, tm), np.float32), 1), dtype=BF16)
    rw = jnp.pad(router_w.T, ((0, LANES - N_EXPERTS), (0, 0)))
    rb = router_b.reshape(N_EXPERTS, 1)
    x1, h2, rt, cnt = _merge(o_nsa, o_lru, mg, x, mod3, norm2_g, bf(w_branch_out[:nsa_w]), bf(w_branch_out[nsa_w:]),
                             bf(w_out), rw, rb, tri)

    rt2 = rt.reshape(T, LANES)
    top_e = rt2[:, 0:TOP_K].astype(jnp.int32)
    rank = rt2[:, TOP_K:2 * TOP_K].astype(jnp.int32)
    counts = cnt[:, 0].astype(jnp.int32)
    padded = (counts + EXPERT_ROWS - 1) // EXPERT_ROWS * EXPERT_ROWS
    ends = jnp.cumsum(padded)
    start = ends - padded
    dest = start[top_e] + rank
    n_rows = -(-(T * TOP_K + N_EXPERTS * (EXPERT_ROWS - 1)) // EXPERT_ROWS) * EXPERT_ROWS
    n_blocks = n_rows // EXPERT_ROWS
    blk_first = jnp.arange(n_blocks, dtype=jnp.int32) * EXPERT_ROWS
    blk_e = jnp.minimum(jnp.sum(ends[None, :] <= blk_first[:, None], axis=1), N_EXPERTS - 1).astype(jnp.int32)
    n_used = (ends[-1:] // EXPERT_ROWS).astype(jnp.int32)

    dest_kt = dest.T
    buf = _sc_scatter_rows(h2.reshape(T, D // 2), dest_kt, n_rows)

    b1 = jnp.concatenate([moe_b1[:, 0::2], moe_b1[:, 1::2]], axis=1)
    ybuf = _experts(blk_e, n_used, buf, moe_w1, b1.reshape(N_EXPERTS, 1, -1), moe_w2, moe_b2.reshape(N_EXPERTS, 1, D))
    yg = _sc_gather_rows(ybuf, dest_kt.reshape(-1)).reshape(TOP_K, B, S, D // 2)
    return _combine(x1, yg, rt, mod3)


def kernel(x, c, positions, ada_w, ada_b, norm1_g, w_in, q_norm_g, k_norm_g, cmp_pe_k, cmp_pe_v, cmp_wk1, cmp_wk2, cmp_wv1, cmp_wv2, conv_w, conv_b, lru_wa, lru_ba, lru_wx, lru_bx, lru_lambda, w_branch_out, w_out, norm2_g, router_w, router_b, moe_w1, moe_b1, moe_w2, moe_b2):
    inv = ROPE_THETA ** (-jnp.arange(0, HEAD_DIM, 2, dtype=F32) / HEAD_DIM)
    ang = positions.astype(F32)[..., None] * inv
    ang = jnp.concatenate([ang, ang], axis=-1)
    for l in range(ada_w.shape[0]):
        x = _layer(x, c, ang, ada_w[l], ada_b[l], norm1_g[l], w_in[l], q_norm_g[l], k_norm_g[l], cmp_pe_k[l],
                   cmp_pe_v[l], cmp_wk1[l], cmp_wk2[l], cmp_wv1[l], cmp_wv2[l], conv_w[l], conv_b[l], lru_wa[l],
                   lru_ba[l], lru_wx[l], lru_bx[l], lru_lambda[l], w_branch_out[l], w_out[l], norm2_g[l], router_w[l],
                   router_b[l], moe_w1[l], moe_b1[l], moe_w2[l], moe_b2[l])
    return x
```

```python
import functools
import math

import jax
import jax.numpy as jnp
import numpy as np
from jax import lax
from jax.experimental import pallas as pl
from jax.experimental.pallas import tpu as pltpu
from jax.experimental.pallas import tpu_sc as plsc

F32 = jnp.float32
BF16 = jnp.bfloat16

N_HEADS = 8
HEAD_DIM = 128
N_KV_GROUPS = 2
HEADS_PER_GROUP = N_HEADS // N_KV_GROUPS
N_NSA_BRANCHES = 3
CMP_BLOCK = 32
CMP_STRIDE = 16
CMP_HIDDEN = 256
SEL_BLOCK = 64
N_SELECT = 8
WINDOW = 512
ROPE_THETA = 10000.0
LRU_BLOCKS = 4
CONV_WIDTH = 4
LRU_C = 8.0
N_EXPERTS = 32
TOP_K = 4
SWIGLU_LIMIT = 7.0
SWIGLU_ALPHA = 1.702
RMS_EPS = 1e-6
NEG_INF = -1e30
LOWEST = -3.0e38
LOG2_E = 1.4426950408889634

LANES = 128
SUBLANES = 8
VMEM_LIMIT = 56 * 1024 * 1024
SC_CORES = 2
SC_SUBCORES = 16
SC_CHUNK = 64

Q_TILE = 256
KEY_TILE = 128
SLC_TILE = 512
ONES_ROWS = 16
HEAD_PAIR = 2
IN_ROWS = 512
LRU_ROWS = 256
EXPERT_ROWS = 512
COMBINE_ROWS = 256


def _sigmoid(v):
    return 0.5 * jnp.tanh(0.5 * v) + 0.5


def _gelu_tanh(v):
    c = math.sqrt(2.0 / math.pi)
    half = 0.5 * v
    return half + half * jnp.tanh(v * (c + (c * 0.044715) * (v * v)))


def _rms(v, g):
    return v * lax.rsqrt(jnp.mean(v * v, axis=-1, keepdims=True) + RMS_EPS) * g


def _dot(a, b, **kw):
    return jnp.dot(a, b, preferred_element_type=F32, **kw)


def _dot_nt(a, b, **kw):
    return lax.dot_general(a, b, (((1,), (1,)), ((), ())), preferred_element_type=F32, **kw)


def _pack_bf16_pairs(v):
    n = v.shape[1] // 2
    lo = lax.bitcast_convert_type(v[:, 0:n].astype(BF16).astype(F32), jnp.int32)
    hi = lax.bitcast_convert_type(v[:, n:2 * n].astype(BF16).astype(F32), jnp.int32)
    return lax.shift_right_logical(lo, 16) | hi


def _unpack_bf16_pairs(w):
    lo = lax.bitcast_convert_type(lax.shift_left(w, 16), F32)
    hi = lax.bitcast_convert_type(w & jnp.int32(-65536), F32)
    return jnp.concatenate([lo, hi], axis=1)


def _full(shape):
    nd = len(shape)
    return pl.BlockSpec(shape, lambda *_: (0,) * nd)


def _params(sem):
    return pltpu.CompilerParams(dimension_semantics=sem, vmem_limit_bytes=VMEM_LIMIT)


def _ada_kernel(c_ref, w_ref, b_ref, o_ref):
    c = c_ref[...]
    o_ref[...] = _dot(c * _sigmoid(c), w_ref[...], precision=lax.Precision.HIGHEST) + b_ref[...]


def _ada(c, ada_w, ada_b):
    B, D = c.shape
    N = ada_w.shape[1]
    return pl.pallas_call(
        _ada_kernel,
        grid=(N // D,),
        in_specs=[_full((B, D)), pl.BlockSpec((D, D), lambda j: (0, j)), pl.BlockSpec((1, D), lambda j: (0, j))],
        out_specs=pl.BlockSpec((B, D), lambda j: (0, j)),
        out_shape=jax.ShapeDtypeStruct((B, N), F32),
        compiler_params=_params(("arbitrary",)),
        name="ada",
    )(c, ada_w, ada_b.reshape(1, N))


def _in_kernel(x_ref, mod_ref, g1_ref, ang_ref, qg_ref, kg_ref, wq_ref, wkc_ref, wkr_ref, wg_ref, wlx_ref, wlg_ref,
               wm_ref, qn_ref, qr_ref, kvc_ref, kvr_ref, gt_ref, lx_ref, lg_ref, mg_ref, kvc_sc):
    x = x_ref[...]
    shift1 = mod_ref[0:1, :]
    scale1 = mod_ref[1:2, :]
    h = _rms(x, g1_ref[...]) * (1.0 + scale1) + shift1
    hb = h.astype(BF16)

    ang = ang_ref[...]
    cos = jnp.cos(ang)
    sin = jnp.sin(ang)
    lane = lax.broadcasted_iota(jnp.int32, ang.shape, 1)
    sin_signed = jnp.where(lane < HEAD_DIM // 2, -sin, sin)

    def rope(v):
        return v * cos + pltpu.roll(v, HEAD_DIM // 2, 1) * sin_signed

    q = _dot(hb, wq_ref[...])
    for hh in range(N_HEADS):
        sl = slice(hh * HEAD_DIM, (hh + 1) * HEAD_DIM)
        qh = _rms(q[:, sl], qg_ref[...])
        qn_ref[:, sl] = (qh * HEAD_DIM ** -0.5).astype(BF16)
        qr_ref[:, sl] = (rope(qh) * (HEAD_DIM ** -0.5 * LOG2_E)).astype(BF16)

    kvc = _dot(hb, wkc_ref[...])
    for part in range(kvc_ref.shape[0]):
        kvc_sc[part] = kvc[:, part * HEAD_DIM:(part + 1) * HEAD_DIM]
        for tok in range(CMP_STRIDE):
            piece = kvc_sc[part, pl.ds(tok, kvc_ref.shape[1], stride=CMP_STRIDE), :]
            kvc_ref[part, :, tok * HEAD_DIM:(tok + 1) * HEAD_DIM] = piece.astype(BF16)

    kvr = _dot(hb, wkr_ref[...])
    kvw = N_KV_GROUPS * HEAD_DIM
    for part in range(4):
        for gg in range(N_KV_GROUPS):
            sl = slice(part * kvw + gg * HEAD_DIM, part * kvw + (gg + 1) * HEAD_DIM)
            v = kvr[:, sl]
            if part % 2 == 0:
                row = 1 + part // 2
                v = rope(_rms(v, kg_ref[row:row + 1, :]))
            kvr_ref[:, sl] = v.astype(BF16)

    gt_ref[...] = _sigmoid(_dot(hb, wg_ref[...]))
    lx_ref[...] = _dot(hb, wlx_ref[...]).astype(BF16)
    lg_ref[...] = _dot(hb, wlg_ref[...]).astype(BF16)
    mg_ref[...] = _sigmoid(_dot(hb, wm_ref[...])).astype(BF16)


def _in_proj(x, mod3, norm1_g, ang, q_norm_g, k_norm_g, wq, wkc, wkr, wg, wlx, wlg, wm):
    B, S, D = x.shape
    tm = min(IN_ROWS, S)
    row = lambda w: pl.BlockSpec((None, tm, w), lambda b, s: (b, s, 0))
    widths = (wq.shape[1], wq.shape[1], wkc.shape[1], wkr.shape[1], wg.shape[1], wlx.shape[1], wlg.shape[1], wm.shape[1])
    dtypes = (BF16, BF16, BF16, BF16, F32, BF16, BF16, BF16)
    weights = (wq, wkc, wkr, wg, wlx, wlg, wm)
    out_specs = [row(w) for w in widths]
    out_shape = [jax.ShapeDtypeStruct((B, S, w), dt) for w, dt in zip(widths, dtypes)]
    n_part = wkc.shape[1] // HEAD_DIM
    out_specs[2] = pl.BlockSpec((None, n_part, tm // CMP_STRIDE, CMP_STRIDE * HEAD_DIM), lambda b, s: (b, 0, s, 0))
    out_shape[2] = jax.ShapeDtypeStruct((B, n_part, S // CMP_STRIDE, CMP_STRIDE * HEAD_DIM), BF16)
    return pl.pallas_call(
        _in_kernel,
        grid=(B, S // tm),
        in_specs=[row(D), pl.BlockSpec((None, 6, D), lambda b, s: (b, 0, 0)), _full((1, D)), row(HEAD_DIM),
                  _full((1, HEAD_DIM)), _full(k_norm_g.shape)] + [_full(w.shape) for w in weights],
        out_specs=out_specs,
        out_shape=out_shape,
        scratch_shapes=[pltpu.VMEM((n_part, tm, HEAD_DIM), F32)],
        compiler_params=_params(("arbitrary", "arbitrary")),
        name="in_proj",
    )(x, mod3, norm1_g.reshape(1, D), ang, q_norm_g.reshape(1, HEAD_DIM), k_norm_g, *weights)


def _cmp_kernel(ak_ref, av_ref, pek_ref, pev_ref, wk1_ref, wk2_ref, wv1_ref, wv2_ref, kg_ref, kc_ref, vc_ref):
    half = CMP_STRIDE * HEAD_DIM

    def compress(a_ref, pe_ref, w1_ref, w2_ref):
        a = a_ref[...]
        u = _dot(a, w1_ref[0:half, :])
        v = _dot(a, w1_ref[half:2 * half, :])
        pw = _dot(pe_ref[...], w1_ref[...])
        pre = u + pltpu.roll(v, v.shape[0] - 1, 0) + pw[0:1, :]
        return _dot(_gelu_tanh(pre).astype(BF16), w2_ref[...])

    kc_ref[...] = _rms(compress(ak_ref, pek_ref, wk1_ref, wk2_ref), kg_ref[0:1, :]).astype(BF16)
    vc_ref[...] = compress(av_ref, pev_ref, wv1_ref, wv2_ref).T.astype(BF16)


def _compress(a, pe_k, pe_v, wk1, wk2, wv1, wv2, k_norm_g):
    B, _, NC, W = a.shape
    G = N_KV_GROUPS
    out = pl.BlockSpec((None, None, NC, HEAD_DIM), lambda b, g: (b, g, 0, 0))
    consts = (pe_k, pe_v, wk1, wk2, wv1, wv2, k_norm_g)
    return pl.pallas_call(
        _cmp_kernel,
        grid=(B, G),
        in_specs=[pl.BlockSpec((None, None, NC, W), lambda b, g: (b, g, 0, 0)),
                  pl.BlockSpec((None, None, NC, W), lambda b, g: (b, G + g, 0, 0))] + [_full(c.shape) for c in consts],
        out_specs=[out, out],
        out_shape=[jax.ShapeDtypeStruct((B, G, NC, HEAD_DIM), BF16)] * 2,
        compiler_params=_params(("arbitrary", "arbitrary")),
        name="compress",
    )(a, a, *consts)


def _nsa_kernel(qn_ref, qr_ref, kc_ref, vc_ref, ks_ref, vs_ref, kw_ref, vw_ref, gt_ref, ov_ref, ex_ref, o_ref,
                vst_sc, vwt_sc):
    qi = pl.program_id(2)
    tq = qn_ref.shape[0]
    hpg = HEADS_PER_GROUP
    seq = ks_ref.shape[0]
    n_blk = seq // SEL_BLOCK

    @pl.when(qi == 0)
    def _():
        for kt in range(seq // KEY_TILE):
            rows = slice(kt * KEY_TILE, (kt + 1) * KEY_TILE)
            vst_sc[0:HEAD_DIM, rows] = vs_ref[rows, :].astype(F32).T.astype(BF16)
            vwt_sc[0:HEAD_DIM, rows] = vw_ref[rows, :].astype(F32).T.astype(BF16)
        ones = jnp.ones((vst_sc.shape[0] - HEAD_DIM, seq), BF16)
        vst_sc[HEAD_DIM:, :] = ones
        vwt_sc[HEAD_DIM:, :] = ones

    def stack(ref):
        return jnp.concatenate([ref[:, h * HEAD_DIM:(h + 1) * HEAD_DIM] for h in range(hpg)], axis=0)

    def rep(v):
        return jnp.concatenate([v] * hpg, axis=1)

    def pair(v):
        return jnp.concatenate([v] * HEAD_PAIR, axis=1)

    qn = stack(qn_ref)
    qr = stack(qr_ref)
    pair_cols = [slice(hp * HEAD_PAIR * tq, (hp + 1) * HEAD_PAIR * tq) for hp in range(hpg // HEAD_PAIR)]

    span = WINDOW + tq
    w0 = pl.multiple_of(jnp.maximum(qi * tq - WINDOW, 0), KEY_TILE)
    kpos = w0 + lax.broadcasted_iota(jnp.int32, (span, tq), 0)
    t_w = qi * tq + lax.broadcasted_iota(jnp.int32, (span, tq), 1)
    window_bias = pair(jnp.where((kpos <= t_w) & (kpos > t_w - WINDOW), 0.0, NEG_INF))
    k_win = kw_ref[pl.ds(w0, span), :]
    win_scores = [_dot_nt(k_win, qr[cols]) + window_bias for cols in pair_cols]

    row = lax.broadcasted_iota(jnp.int32, (KEY_TILE, tq), 0)
    col = lax.broadcasted_iota(jnp.int32, (KEY_TILE, tq), 1)
    t_q = qi * tq + col

    cmask = rep(row * CMP_STRIDE + (CMP_BLOCK - 1) <= t_q)
    s = jnp.where(cmask, _dot_nt(kc_ref[...], qn), NEG_INF)
    e = jnp.exp(s - jnp.max(s, axis=0, keepdims=True))
    p = jnp.where(cmask, e * (1.0 / jnp.sum(e, axis=0, keepdims=True)), 0.0)
    o_cmp = _dot(vc_ref[...], p.astype(BF16))

    psum = p[:, 0:tq]
    for h in range(1, hpg):
        psum = psum + p[:, h * tq:(h + 1) * tq]
    imp = _dot(ov_ref[...], psum, precision=lax.Precision.HIGHEST)[0:n_blk]
    blk = lax.broadcasted_iota(jnp.int32, (n_blk, tq), 0).astype(F32)
    cur = ((qi * tq + lax.broadcasted_iota(jnp.int32, (n_blk, tq), 1)) // SEL_BLOCK).astype(F32)
    forced = (blk == 0.0) | (blk == cur) | (blk == cur - 1.0)
    score = jnp.where(forced, 1e6, jnp.where(blk <= cur, imp, -1e6))
    bias = jnp.full((n_blk, tq), NEG_INF, F32)
    for _ in range(min(N_SELECT, n_blk)):
        best = jnp.max(score, axis=0, keepdims=True)
        idx = jnp.min(jnp.where(score == best, blk, float(LANES)), axis=0, keepdims=True)
        hit = blk == idx
        bias = jnp.where(hit, 0.0, bias)
        score = jnp.where(hit, LOWEST, score)
    bias = jnp.concatenate([bias, jnp.zeros((LANES - n_blk, tq), F32)], axis=0)
    q_aug = jnp.concatenate([qr, jnp.concatenate([bias.T.astype(BF16)] * hpg, axis=0)], axis=1)

    def softmax_pv(sc, vt):
        pr = jnp.exp2((sc - jnp.max(sc, axis=0, keepdims=True)).astype(BF16))
        o = _dot(vt, pr)
        return o[0:HEAD_DIM] * (1.0 / o[HEAD_DIM:HEAD_DIM + 1])

    n_full = (qi * tq) // SLC_TILE
    vt_win = vwt_sc[:, pl.ds(w0, span)]
    gt = gt_ref[...].T
    for v in range(seq // SLC_TILE):
        @pl.when(n_full == v)
        def _():
            ext = (v + 1) * SLC_TILE
            k_aug = jnp.concatenate([ks_ref[0:ext, :], ex_ref[0:ext, :]], axis=1)
            kpos = v * SLC_TILE + lax.broadcasted_iota(jnp.int32, (SLC_TILE, tq), 0)
            t_s = qi * tq + lax.broadcasted_iota(jnp.int32, (SLC_TILE, tq), 1)
            causal = pair(jnp.where(kpos <= t_s, 0.0, NEG_INF))
            scores = []
            for cols in pair_cols:
                sc = _dot_nt(k_aug, q_aug[cols])
                last = sc[ext - SLC_TILE:ext] + causal
                scores.append(last if v == 0 else jnp.concatenate([sc[0:ext - SLC_TILE], last], axis=0))
            o_wins = [softmax_pv(sc, vt_win) for sc in win_scores]
            o_slcs = [softmax_pv(sc, vst_sc[:, 0:ext]) for sc in scores]
            for h in range(hpg):
                hp, j = divmod(h, HEAD_PAIR)
                sub = slice(j * tq, (j + 1) * tq)
                c0 = h * N_NSA_BRANCHES
                o = (gt[c0:c0 + 1, :] * o_cmp[:, h * tq:(h + 1) * tq] + gt[c0 + 1:c0 + 2, :] * o_slcs[hp][:, sub]
                     + gt[c0 + 2:c0 + 3, :] * o_wins[hp][:, sub])
                o_ref[:, h * HEAD_DIM:(h + 1) * HEAD_DIM] = o.T.astype(BF16)


def _nsa(qn, qr, kc, vc, kvr, gates, overlap, expand):
    B, S, _ = qn.shape
    G = N_KV_GROUPS
    tq = Q_TILE
    gw = HEADS_PER_GROUP * HEAD_DIM
    qspec = pl.BlockSpec((None, tq, gw), lambda b, g, i: (b, i, g))
    cspec = pl.BlockSpec((None, None) + kc.shape[2:], lambda b, g, i: (b, g, 0, 0))
    kv = lambda part: pl.BlockSpec((None, S, HEAD_DIM), lambda b, g, i: (b, 0, part * G + g))
    return pl.pallas_call(
        _nsa_kernel,
        grid=(B, G, S // tq),
        in_specs=[qspec, qspec, cspec, cspec, kv(0), kv(1), kv(2), kv(3),
                  pl.BlockSpec((None, tq, LANES), lambda b, g, i: (b, i, g)), _full(overlap.shape), _full(expand.shape)],
        out_specs=qspec,
        out_shape=jax.ShapeDtypeStruct(qn.shape, BF16),
        scratch_shapes=[pltpu.VMEM((HEAD_DIM + ONES_ROWS, S), BF16), pltpu.VMEM((HEAD_DIM + ONES_ROWS, S), BF16)],
        compiler_params=_params(("arbitrary", "arbitrary", "arbitrary")),
        name="nsa",
    )(qn, qr, kc, vc, kvr, kvr, kvr, kvr, gates, overlap, expand)


def _lru_kernel(lx_ref, lg_ref, cw_ref, cb_ref, wa_ref, ba_ref, wx_ref, bx_ref, lam_ref, o_ref,
                xs_sc, a_sc, u_sc, h_sc, carry_sc):
    ts = lx_ref.shape[0]
    width = lx_ref.shape[1]
    bw = width // LRU_BLOCKS

    @pl.when(pl.program_id(1) == 0)
    def _():
        xs_sc[0:SUBLANES, :] = jnp.zeros((SUBLANES, width), F32)
        carry_sc[...] = jnp.zeros(carry_sc.shape, F32)

    xs_sc[SUBLANES:SUBLANES + ts, :] = lx_ref[...].astype(F32)
    xc = cb_ref[...] + cw_ref[CONV_WIDTH - 1:CONV_WIDTH, :] * xs_sc[SUBLANES:SUBLANES + ts, :]
    for d in range(1, CONV_WIDTH):
        w = cw_ref[CONV_WIDTH - 1 - d:CONV_WIDTH - d, :]
        xc = xc + w * xs_sc[SUBLANES - d:SUBLANES - d + ts, :]
    xs_sc[0:SUBLANES, :] = xs_sc[ts:ts + SUBLANES, :]

    lam = -lam_ref[...]
    neg_c_softplus = -LRU_C * (jnp.maximum(lam, 0.0) + jnp.log1p(jnp.exp(-jnp.abs(lam))))
    xcb = xc.astype(BF16)
    for blk in range(LRU_BLOCKS):
        sl = slice(blk * bw, (blk + 1) * bw)
        r = _sigmoid(_dot(xcb[:, sl], wa_ref[blk]) + ba_ref[:, sl])
        i = _sigmoid(_dot(xcb[:, sl], wx_ref[blk]) + bx_ref[:, sl])
        log_a = r * neg_c_softplus[:, sl]
        a = jnp.exp(log_a)
        a_sc[:, sl] = a
        u_sc[:, sl] = jnp.sqrt(-jnp.tanh(log_a) * (a * a + 1.0)) * (i * xc[:, sl])

    row = lax.broadcasted_iota(jnp.int32, (SUBLANES, width), 0)

    def chunk(c, h_prev):
        r0 = pl.multiple_of(c * SUBLANES, SUBLANES)
        a = a_sc[pl.ds(r0, SUBLANES), :]
        u = u_sc[pl.ds(r0, SUBLANES), :]
        for d in (1, 2, 4):
            keep = row >= d
            u = jnp.where(keep, a * pltpu.roll(u, d, 0) + u, u)
            a = jnp.where(keep, a * pltpu.roll(a, d, 0), a)
        h = a * h_prev + u
        h_sc[pl.ds(r0, SUBLANES), :] = h
        return jnp.broadcast_to(h[SUBLANES - 1:SUBLANES, :], h.shape)

    carry_sc[...] = lax.fori_loop(0, ts // SUBLANES, chunk, carry_sc[...])
    o_ref[...] = (_gelu_tanh(lg_ref[...].astype(F32)) * h_sc[...]).astype(BF16)


def _lru(lx, lg, conv_w, conv_b, wa, ba, wx, bx, lam):
    B, S, W = lx.shape
    ts = min(LRU_ROWS, S)
    row = pl.BlockSpec((None, ts, W), lambda b, s: (b, s, 0))
    consts = (conv_w, conv_b.reshape(1, W), wa, ba.reshape(1, W), wx, bx.reshape(1, W), lam.reshape(1, W))
    return pl.pallas_call(
        _lru_kernel,
        grid=(B, S // ts),
        in_specs=[row, row] + [_full(a.shape) for a in consts],
        out_specs=row,
        out_shape=jax.ShapeDtypeStruct((B, S, W), BF16),
        scratch_shapes=[pltpu.VMEM((ts + SUBLANES, W), F32), pltpu.VMEM((ts, W), F32), pltpu.VMEM((ts, W), F32),
                        pltpu.VMEM((ts, W), F32), pltpu.VMEM((SUBLANES, W), F32)],
        compiler_params=_params(("arbitrary", "arbitrary")),
        name="lru",
    )(lx, lg, *consts)


def _merge_kernel(on_ref, ol_ref, mg_ref, x_ref, mod_ref, g2_ref, wn_ref, wl_ref, wo_ref, rw_ref, rb_ref, tri_ref,
                  x1_ref, h2_ref, rt_ref, cnt_ref, carry_sc):
    first = (pl.program_id(0) == 0) & (pl.program_id(1) == 0)

    @pl.when(first)
    def _():
        carry_sc[...] = jnp.zeros(carry_sc.shape, F32)

    d = x_ref.shape[1]
    y_nsa = _dot(on_ref[...], wn_ref[...])
    y_lru = _dot(ol_ref[...], wl_ref[...])
    merged = mg_ref[:, 0:d].astype(F32) * y_nsa + mg_ref[:, d:2 * d].astype(F32) * y_lru
    gate1 = mod_ref[2:3, :]
    shift2 = mod_ref[3:4, :]
    scale2 = mod_ref[4:5, :]
    x1 = x_ref[...] + gate1 * _dot(merged.astype(BF16), wo_ref[...])
    x1_ref[...] = x1
    h2 = _rms(x1, g2_ref[...]) * (1.0 + scale2) + shift2
    h2_ref[...] = _pack_bf16_pairs(h2)

    rw = rw_ref[...]
    rw_hi = rw.astype(BF16)
    rw_lo = (rw - rw_hi.astype(F32)).astype(BF16)
    h2_hi = h2.astype(BF16)
    h2_lo = (h2 - h2_hi.astype(F32)).astype(BF16)
    logits = _dot_nt(rw_hi, h2_hi) + (_dot_nt(rw_hi, h2_lo) + _dot_nt(rw_lo, h2_hi))
    logits = logits[0:N_EXPERTS] + rb_ref[...]
    tm = logits.shape[1]
    eid = lax.broadcasted_iota(jnp.int32, logits.shape, 0).astype(F32)
    score = logits
    picks = []
    onehot = jnp.zeros(logits.shape, F32)
    for _ in range(TOP_K):
        best = jnp.max(score, axis=0, keepdims=True)
        idx = jnp.min(jnp.where(score == best, eid, float(LANES)), axis=0, keepdims=True)
        hit = eid == idx
        picks.append((idx, best, hit))
        onehot = jnp.where(hit, 1.0, onehot)
        score = jnp.where(hit, LOWEST, score)
    ew = [jnp.exp(v - picks[0][1]) for _, v, _ in picks]
    den = ew[0]
    for v in ew[1:]:
        den = den + v
    inv_den = 1.0 / den

    before = _dot(onehot.astype(BF16), tri_ref[...]) + carry_sc[:, 0:1]
    carry_sc[...] = carry_sc[...] + jnp.sum(onehot, axis=1, keepdims=True)
    cnt_ref[...] = carry_sc[...]

    rows = [idx for idx, _, _ in picks]
    rows += [jnp.sum(jnp.where(hit, before, 0.0), axis=0, keepdims=True) for _, _, hit in picks]
    rows += [e * inv_den for e in ew]
    rows.append(jnp.zeros((LANES - len(rows), tm), F32))
    rt_ref[...] = jnp.concatenate(rows, axis=0).T


def _merge(o_nsa, o_lru, mg, x, mod3, norm2_g, wn, wl, wo, rw, rb, tri):
    B, S, D = x.shape
    tm = tri.shape[0]
    row = lambda w: pl.BlockSpec((None, tm, w), lambda b, s: (b, s, 0))
    consts = (norm2_g.reshape(1, D), wn, wl, wo, rw, rb, tri)
    return pl.pallas_call(
        _merge_kernel,
        grid=(B, S // tm),
        in_specs=[row(D), row(D), row(2 * D), row(D), pl.BlockSpec((None, 6, D), lambda b, s: (b, 0, 0))]
                 + [_full(a.shape) for a in consts],
        out_specs=[row(D), row(D // 2), row(LANES), _full((N_EXPERTS, LANES))],
        out_shape=[jax.ShapeDtypeStruct((B, S, D), F32), jax.ShapeDtypeStruct((B, S, D // 2), jnp.int32),
                   jax.ShapeDtypeStruct((B, S, LANES), F32), jax.ShapeDtypeStruct((N_EXPERTS, LANES), F32)],
        scratch_shapes=[pltpu.VMEM((N_EXPERTS, LANES), F32)],
        compiler_params=_params(("arbitrary", "arbitrary")),
        name="merge",
    )(o_nsa, o_lru, mg, x, mod3, *consts)


def _expert_kernel(be_ref, nb_ref, x_ref, w1_ref, b1_ref, w2_ref, b2_ref, pick_ref, y_ref, w1_sc, w2_sc):
    i = pl.program_id(0)
    ff = w2_ref.shape[0]
    chunk = pick_ref.shape[0]

    @pl.when((i == 0) | (be_ref[i] != be_ref[jnp.maximum(i - 1, 0)]))
    def _():
        for c in range(2 * ff // chunk):
            r = _dot(w1_ref[:, c * chunk:(c + 1) * chunk].astype(BF16), pick_ref[...])
            half = chunk // 2
            w1_sc[:, c * half:(c + 1) * half] = r[:, 0:half].astype(BF16)
            w1_sc[:, ff + c * half:ff + (c + 1) * half] = r[:, half:chunk].astype(BF16)
        w2_sc[...] = w2_ref[...].astype(BF16)

    @pl.when(i < nb_ref[0])
    def _():
        h = _dot(_unpack_bf16_pairs(x_ref[...]).astype(BF16), w1_sc[...]) + b1_ref[...]
        x_glu = jnp.minimum(h[:, 0:ff], SWIGLU_LIMIT)
        x_lin = jnp.clip(h[:, ff:2 * ff], -SWIGLU_LIMIT, SWIGLU_LIMIT)
        act = x_glu * _sigmoid(SWIGLU_ALPHA * x_glu) * (x_lin + 1.0)
        y_ref[...] = _pack_bf16_pairs(_dot(act.astype(BF16), w2_sc[...]) + b2_ref[...])

    @pl.when(i >= nb_ref[0])
    def _():
        y_ref[...] = jnp.zeros(y_ref.shape, jnp.int32)


def _experts(blk_e, n_used, buf, w1, b1, w2, b2):
    n_rows = buf.shape[0]
    F, D = w2.shape[1], w2.shape[2]
    tm = EXPERT_ROWS
    chunk = 2 * LANES
    pick = np.zeros((chunk, chunk), np.float32)
    pick[2 * np.arange(LANES), np.arange(LANES)] = 1.0
    pick[2 * np.arange(LANES) + 1, LANES + np.arange(LANES)] = 1.0
    wspec = lambda a, b: pl.BlockSpec((None, a, b), lambda i, be, nb: (be[i], 0, 0))
    return pl.pallas_call(
        _expert_kernel,
        grid_spec=pltpu.PrefetchScalarGridSpec(
            num_scalar_prefetch=2,
            grid=(n_rows // tm,),
            in_specs=[pl.BlockSpec((tm, D // 2), lambda i, be, nb: (i, 0)), wspec(D, 2 * F), wspec(1, 2 * F), wspec(F, D),
                      wspec(1, D), pl.BlockSpec((chunk, chunk), lambda i, be, nb: (0, 0))],
            out_specs=pl.BlockSpec((tm, D // 2), lambda i, be, nb: (i, 0)),
            scratch_shapes=[pltpu.VMEM((D, 2 * F), BF16), pltpu.VMEM((F, D), BF16)],
        ),
        out_shape=jax.ShapeDtypeStruct((n_rows, D // 2), jnp.int32),
        compiler_params=_params(("arbitrary",)),
        name="experts",
    )(blk_e, n_used, buf, w1, b1, w2, b2, jnp.asarray(pick, dtype=BF16))


def _sc_mesh():
    return plsc.VectorSubcoreMesh(core_axis_name="c", subcore_axis_name="s")


def _sc_worker():
    return lax.axis_index("s") * SC_CORES + lax.axis_index("c")


def _sc_scatter_rows(rows, dest, n_rows):
    T, W = rows.shape
    K = dest.shape[0]
    n_workers = SC_CORES * SC_SUBCORES
    per_w = T // n_workers
    ch = SC_CHUNK
    n_ch = per_w // ch
    assert per_w * n_workers == T and n_ch * ch == per_w and n_ch % 2 == 0
    dest4 = dest.reshape(K, n_workers, n_ch, ch).transpose(1, 2, 0, 3)

    @functools.partial(
        pl.kernel, mesh=_sc_mesh(), out_type=jax.ShapeDtypeStruct((n_rows, W), rows.dtype),
        scratch_types=[pltpu.VMEM((n_ch, K, ch), jnp.int32), pltpu.VMEM((2, ch, W), rows.dtype),
                       pltpu.SemaphoreType.DMA((2,)), pltpu.SemaphoreType.DMA((2,))])
    def scatter(rows_hbm, dest_hbm, out_hbm, idx_v, rows_v, read_sem, write_sem):
        wid = _sc_worker()
        base = wid * per_w
        pltpu.sync_copy(dest_hbm.at[wid], idx_v)

        def read(c, b):
            src = rows_hbm.at[pl.ds(pl.multiple_of(base + c * ch, ch), ch)]
            return pltpu.make_async_copy(src, rows_v.at[b], read_sem.at[b])

        def write(c, b, k):
            return pltpu.make_async_copy(rows_v.at[b], out_hbm.at[idx_v.at[c, k]], write_sem.at[b])

        read(0, 0).start()

        @pl.loop(0, n_ch, step=2)
        def _(i):
            for b in range(2):
                c = i + b
                read(c, b).wait()

                @pl.when(c >= 1)
                def _():
                    for k in range(K):
                        write(c - 1, 1 - b, k).wait()

                @pl.when(c + 1 < n_ch)
                def _():
                    read(c + 1, 1 - b).start()

                for k in range(K):
                    write(c, b, k).start()

        for k in range(K):
            write(n_ch - 1, 1, k).wait()

    return scatter(rows, dest4)


def _sc_gather_rows(table, idx):
    N = idx.shape[0]
    W = table.shape[1]
    n_workers = SC_CORES * SC_SUBCORES
    per_w = N // n_workers
    ch = SC_CHUNK
    n_ch = per_w // ch
    assert per_w * n_workers == N and n_ch * ch == per_w and n_ch % 2 == 0
    idx3 = idx.reshape(n_workers, n_ch, ch)

    @functools.partial(
        pl.kernel, mesh=_sc_mesh(), out_type=jax.ShapeDtypeStruct((N, W), table.dtype),
        scratch_types=[pltpu.VMEM((n_ch, ch), jnp.int32), pltpu.VMEM((2, ch, W), table.dtype),
                       pltpu.SemaphoreType.DMA((2,)), pltpu.SemaphoreType.DMA((2,))])
    def gather(table_hbm, idx_hbm, out_hbm, idx_v, rows_v, gather_sem, write_sem):
        wid = _sc_worker()
        base = wid * per_w
        pltpu.sync_copy(idx_hbm.at[wid], idx_v)

        def fetch(c, b):
            return pltpu.make_async_copy(table_hbm.at[idx_v.at[c]], rows_v.at[b], gather_sem.at[b])

        def write(c, b):
            dst = out_hbm.at[pl.ds(pl.multiple_of(base + c * ch, ch), ch)]
            return pltpu.make_async_copy(rows_v.at[b], dst, write_sem.at[b])

        fetch(0, 0).start()

        @pl.loop(0, n_ch, step=2)
        def _(i):
            for b in range(2):
                c = i + b
                fetch(c, b).wait()

                @pl.when(c >= 1)
                def _():
                    write(c - 1, 1 - b).wait()

                @pl.when(c + 1 < n_ch)
                def _():
                    fetch(c + 1, 1 - b).start()

                write(c, b).start()

        write(n_ch - 1, 1).wait()

    return gather(table, idx3)


def _combine_kernel(x1_ref, yg_ref, rt_ref, mod_ref, o_ref):
    gate2 = mod_ref[5:6, :]
    rt = rt_ref[...]
    acc = rt[:, 2 * TOP_K:2 * TOP_K + 1] * _unpack_bf16_pairs(yg_ref[0])
    for k in range(1, TOP_K):
        acc = acc + rt[:, 2 * TOP_K + k:2 * TOP_K + k + 1] * _unpack_bf16_pairs(yg_ref[k])
    o_ref[...] = x1_ref[...] + gate2 * acc


def _combine(x1, yg, rt, mod3):
    B, S, D = x1.shape
    tm = min(COMBINE_ROWS, S)
    row = lambda w: pl.BlockSpec((None, tm, w), lambda b, s: (b, s, 0))
    return pl.pallas_call(
        _combine_kernel,
        grid=(B, S // tm),
        in_specs=[row(D), pl.BlockSpec((TOP_K, None, tm, D // 2), lambda b, s: (0, b, s, 0)), row(LANES),
                  pl.BlockSpec((None, 6, D), lambda b, s: (b, 0, 0))],
        out_specs=row(D),
        out_shape=jax.ShapeDtypeStruct((B, S, D), F32),
        compiler_params=_params(("arbitrary", "arbitrary")),
        name="combine",
    )(x1, yg, rt, mod3)


def _overlap_matrix(n_cmp_pad, n_blk):
    cs = np.arange(n_cmp_pad)[:, None] * CMP_STRIDE
    js = np.arange(LANES)[None, :] * SEL_BLOCK
    m = (cs <= js + SEL_BLOCK - 1) & (cs + CMP_BLOCK - 1 >= js) & (np.arange(LANES)[None, :] < n_blk)
    return jnp.asarray(m.astype(np.float32).T)


def _block_onehot(seq):
    key = np.arange(seq)[:, None]
    j = np.arange(LANES)[None, :]
    return jnp.asarray((j == key // SEL_BLOCK).astype(np.float32), dtype=BF16)


def _layer(x, c, ang, ada_w, ada_b, norm1_g, w_in, q_norm_g, k_norm_g, cmp_pe_k, cmp_pe_v, cmp_wk1, cmp_wk2, cmp_wv1,
           cmp_wv2, conv_w, conv_b, lru_wa, lru_ba, lru_wx, lru_bx, lru_lambda, w_branch_out, w_out, norm2_g,
           router_w, router_b, moe_w1, moe_b1, moe_w2, moe_b2):
    B, S, D = x.shape
    T = B * S
    G = N_KV_GROUPS
    nsa_w = N_HEADS * HEAD_DIM
    kv_w = G * HEAD_DIM

    mod3 = _ada(c, ada_w, ada_b).reshape(B, 6, D)

    o = 0
    wq = w_in[:, o:o + nsa_w]; o += nsa_w
    wkc = w_in[:, o:o + 2 * kv_w]; o += 2 * kv_w
    wkr = w_in[:, o:o + 4 * kv_w]; o += 4 * kv_w
    n_gate = N_HEADS * N_NSA_BRANCHES
    wg_raw = w_in[:, o:o + n_gate].reshape(D, G, n_gate // G); o += n_gate
    wg = jnp.pad(wg_raw, ((0, 0), (0, 0), (0, LANES - n_gate // G))).reshape(D, G * LANES)
    wlx = w_in[:, o:o + D]; o += D
    wlg = w_in[:, o:o + D]; o += D
    wm = w_in[:, o:o + 2 * D]
    bf = lambda a: a.astype(BF16)

    qn, qr, kvc, kvr, gates, lx, lg, mg = _in_proj(x, mod3, norm1_g, ang, q_norm_g, k_norm_g, bf(wq), bf(wkc), bf(wkr),
                                                   bf(wg), bf(wlx), bf(wlg), bf(wm))

    nc = S // CMP_STRIDE
    pe = lambda p: jnp.broadcast_to(p.reshape(1, CMP_BLOCK * HEAD_DIM), (SUBLANES, CMP_BLOCK * HEAD_DIM)).astype(BF16)
    kc, vc = _compress(kvc, pe(cmp_pe_k), pe(cmp_pe_v), bf(cmp_wk1), bf(cmp_wk2), bf(cmp_wv1), bf(cmp_wv2), k_norm_g)

    o_nsa = _nsa(qn, qr, kc, vc, kvr, gates, _overlap_matrix(nc, S // SEL_BLOCK), _block_onehot(S))
    o_lru = _lru(lx, lg, conv_w, conv_b, bf(lru_wa), lru_ba, bf(lru_wx), lru_bx, lru_lambda)

    tm = min(IN_ROWS, S)
    tri = jnp.asarray(np.triu(np.ones((tm, tm), np.float32), 1), dtype=BF16)
    rw = jnp.pad(router_w.T, ((0, LANES - N_EXPERTS), (0, 0)))
    rb = router_b.reshape(N_EXPERTS, 1)
    x1, h2, rt, cnt = _merge(o_nsa, o_lru, mg, x, mod3, norm2_g, bf(w_branch_out[:nsa_w]), bf(w_branch_out[nsa_w:]),
                             bf(w_out), rw, rb, tri)

    rt2 = rt.reshape(T, LANES)
    top_e = rt2[:, 0:TOP_K].astype(jnp.int32)
    rank = rt2[:, TOP_K:2 * TOP_K].astype(jnp.int32)
    counts = cnt[:, 0].astype(jnp.int32)
    padded = (counts + EXPERT_ROWS - 1) // EXPERT_ROWS * EXPERT_ROWS
    ends = jnp.cumsum(padded)
    start = ends - padded
    dest = start[top_e] + rank
    n_rows = -(-(T * TOP_K + N_EXPERTS * (EXPERT_ROWS - 1)) // EXPERT_ROWS) * EXPERT_ROWS
    n_blocks = n_rows // EXPERT_ROWS
    blk_first = jnp.arange(n_blocks, dtype=jnp.int32) * EXPERT_ROWS
    blk_e = jnp.minimum(jnp.sum(ends[None, :] <= blk_first[:, None], axis=1), N_EXPERTS - 1).astype(jnp.int32)
    n_used = (ends[-1:] // EXPERT_ROWS).astype(jnp.int32)

    dest_kt = dest.T
    buf = _sc_scatter_rows(h2.reshape(T, D // 2), dest_kt, n_rows)

    b1 = jnp.concatenate([moe_b1[:, 0::2], moe_b1[:, 1::2]], axis=1)
    ybuf = _experts(blk_e, n_used, buf, moe_w1, b1.reshape(N_EXPERTS, 1, -1), moe_w2, moe_b2.reshape(N_EXPERTS, 1, D))
    yg = _sc_gather_rows(ybuf, dest_kt.reshape(-1)).reshape(TOP_K, B, S, D // 2)
    return _combine(x1, yg, rt, mod3)


def kernel(x, c, positions, ada_w, ada_b, norm1_g, w_in, q_norm_g, k_norm_g, cmp_pe_k, cmp_pe_v, cmp_wk1, cmp_wk2, cmp_wv1, cmp_wv2, conv_w, conv_b, lru_wa, lru_ba, lru_wx, lru_bx, lru_lambda, w_branch_out, w_out, norm2_g, router_w, router_b, moe_w1, moe_b1, moe_w2, moe_b2):
    inv = ROPE_THETA ** (-jnp.arange(0, HEAD_DIM, 2, dtype=F32) / HEAD_DIM)
    ang = positions.astype(F32)[..., None] * inv
    ang = jnp.concatenate([ang, ang], axis=-1)
    for l in range(ada_w.shape[0]):
        x = _layer(x, c, ang, ada_w[l], ada_b[l], norm1_g[l], w_in[l], q_norm_g[l], k_norm_g[l], cmp_pe_k[l],
                   cmp_pe_v[l], cmp_wk1[l], cmp_wk2[l], cmp_wv1[l], cmp_wv2[l], conv_w[l], conv_b[l], lru_wa[l],
                   lru_ba[l], lru_wx[l], lru_bx[l], lru_lambda[l], w_branch_out[l], w_out[l], norm2_g[l], router_w[l],
                   router_b[l], moe_w1[l], moe_b1[l], moe_w2[l], moe_b2[l])
    return x
```

```python
import functools
import math

import jax
import jax.numpy as jnp
import numpy as np
from jax import lax
from jax.experimental import pallas as pl
from jax.experimental.pallas import tpu as pltpu
from jax.experimental.pallas import tpu_sc as plsc

F32 = jnp.float32
BF16 = jnp.bfloat16

N_HEADS = 8
HEAD_DIM = 128
N_KV_GROUPS = 2
HEADS_PER_GROUP = N_HEADS // N_KV_GROUPS
N_NSA_BRANCHES = 3
CMP_BLOCK = 32
CMP_STRIDE = 16
CMP_HIDDEN = 256
SEL_BLOCK = 64
N_SELECT = 8
WINDOW = 512
ROPE_THETA = 10000.0
LRU_BLOCKS = 4
CONV_WIDTH = 4
LRU_C = 8.0
N_EXPERTS = 32
TOP_K = 4
SWIGLU_LIMIT = 7.0
SWIGLU_ALPHA = 1.702
RMS_EPS = 1e-6
NEG_INF = -1e30
LOWEST = -3.0e38
LOG2_E = 1.4426950408889634

LANES = 128
SUBLANES = 8
VMEM_LIMIT = 56 * 1024 * 1024
SC_CORES = 2
SC_SUBCORES = 16
SC_CHUNK = 64

Q_TILE = 256
SELECT_ROWS = 1024
KEY_TILE = 128
SLC_TILE = 512
ONES_ROWS = 16
HEAD_PAIR = 2
IN_ROWS = 512
LRU_ROWS = 256
EXPERT_ROWS = 512
COMBINE_ROWS = 256
COMBINE_PARTS = 4


def _sigmoid(v):
    return 0.5 * jnp.tanh(0.5 * v) + 0.5


def _gelu_tanh(v):
    c = math.sqrt(2.0 / math.pi)
    half = 0.5 * v
    return half + half * jnp.tanh(v * (c + (c * 0.044715) * (v * v)))


def _rms(v, g):
    return v * lax.rsqrt(jnp.mean(v * v, axis=-1, keepdims=True) + RMS_EPS) * g


def _dot(a, b, **kw):
    return jnp.dot(a, b, preferred_element_type=F32, **kw)


def _dot_nt(a, b, **kw):
    return lax.dot_general(a, b, (((1,), (1,)), ((), ())), preferred_element_type=F32, **kw)


def _pack_bf16_pairs(v):
    n = v.shape[1] // 2
    lo = lax.bitcast_convert_type(v[:, 0:n].astype(BF16).astype(F32), jnp.int32)
    hi = lax.bitcast_convert_type(v[:, n:2 * n].astype(BF16).astype(F32), jnp.int32)
    return lax.shift_right_logical(lo, 16) | hi


def _unpack_bf16_pairs(w):
    lo = lax.bitcast_convert_type(lax.shift_left(w, 16), F32)
    hi = lax.bitcast_convert_type(w & jnp.int32(-65536), F32)
    return jnp.concatenate([lo, hi], axis=1)


def _full(shape):
    nd = len(shape)
    return pl.BlockSpec(shape, lambda *_: (0,) * nd)


def _params(sem):
    return pltpu.CompilerParams(dimension_semantics=sem, vmem_limit_bytes=VMEM_LIMIT)


def _ada_kernel(c_ref, w_ref, b_ref, o_ref):
    c = c_ref[...]
    o_ref[...] = _dot(c * _sigmoid(c), w_ref[...], precision=lax.Precision.HIGHEST) + b_ref[...]


def _ada(c, ada_w, ada_b):
    B, D = c.shape
    N = ada_w.shape[1]
    return pl.pallas_call(
        _ada_kernel,
        grid=(N // D,),
        in_specs=[_full((B, D)), pl.BlockSpec((D, D), lambda j: (0, j)), pl.BlockSpec((1, D), lambda j: (0, j))],
        out_specs=pl.BlockSpec((B, D), lambda j: (0, j)),
        out_shape=jax.ShapeDtypeStruct((B, N), F32),
        compiler_params=_params(("arbitrary",)),
        name="ada",
    )(c, ada_w, ada_b.reshape(1, N))


def _in_kernel(x_ref, mod_ref, g1_ref, ang_ref, qg_ref, kg_ref, wq_ref, wkc_ref, wkr_ref, wg_ref, wlx_ref, wlg_ref,
               wm_ref, qn_ref, qr_ref, kvc_ref, kvr_ref, gt_ref, lx_ref, lg_ref, mg_ref, kvc_sc):
    x = x_ref[...]
    shift1 = mod_ref[0:1, :]
    scale1 = mod_ref[1:2, :]
    h = _rms(x, g1_ref[...]) * (1.0 + scale1) + shift1
    hb = h.astype(BF16)

    ang = ang_ref[...]
    cos = jnp.cos(ang)
    sin = jnp.sin(ang)
    lane = lax.broadcasted_iota(jnp.int32, ang.shape, 1)
    sin_signed = jnp.where(lane < HEAD_DIM // 2, -sin, sin)

    def rope(v):
        return v * cos + pltpu.roll(v, HEAD_DIM // 2, 1) * sin_signed

    q = _dot(hb, wq_ref[...])
    for hh in range(N_HEADS):
        sl = slice(hh * HEAD_DIM, (hh + 1) * HEAD_DIM)
        qh = _rms(q[:, sl], qg_ref[...])
        qn_ref[:, sl] = (qh * HEAD_DIM ** -0.5).astype(BF16)
        qr_ref[:, sl] = (rope(qh) * (HEAD_DIM ** -0.5 * LOG2_E)).astype(BF16)

    kvc = _dot(hb, wkc_ref[...])
    for part in range(kvc_ref.shape[0]):
        kvc_sc[part] = kvc[:, part * HEAD_DIM:(part + 1) * HEAD_DIM]
        for tok in range(CMP_STRIDE):
            piece = kvc_sc[part, pl.ds(tok, kvc_ref.shape[1], stride=CMP_STRIDE), :]
            kvc_ref[part, :, tok * HEAD_DIM:(tok + 1) * HEAD_DIM] = piece.astype(BF16)

    kvr = _dot(hb, wkr_ref[...])
    kvw = N_KV_GROUPS * HEAD_DIM
    for part in range(4):
        for gg in range(N_KV_GROUPS):
            sl = slice(part * kvw + gg * HEAD_DIM, part * kvw + (gg + 1) * HEAD_DIM)
            v = kvr[:, sl]
            if part % 2 == 0:
                row = 1 + part // 2
                v = rope(_rms(v, kg_ref[row:row + 1, :]))
            kvr_ref[:, sl] = v.astype(BF16)

    gt_ref[...] = _sigmoid(_dot(hb, wg_ref[...]))
    lx_ref[...] = _dot(hb, wlx_ref[...]).astype(BF16)
    lg_ref[...] = _dot(hb, wlg_ref[...]).astype(BF16)
    mg_ref[...] = _sigmoid(_dot(hb, wm_ref[...])).astype(BF16)


def _in_proj(x, mod3, norm1_g, ang, q_norm_g, k_norm_g, wq, wkc, wkr, wg, wlx, wlg, wm):
    B, S, D = x.shape
    tm = min(IN_ROWS, S)
    row = lambda w: pl.BlockSpec((None, tm, w), lambda b, s: (b, s, 0))
    widths = (wq.shape[1], wq.shape[1], wkc.shape[1], wkr.shape[1], wg.shape[1], wlx.shape[1], wlg.shape[1], wm.shape[1])
    dtypes = (BF16, BF16, BF16, BF16, F32, BF16, BF16, BF16)
    weights = (wq, wkc, wkr, wg, wlx, wlg, wm)
    out_specs = [row(w) for w in widths]
    out_shape = [jax.ShapeDtypeStruct((B, S, w), dt) for w, dt in zip(widths, dtypes)]
    n_part = wkc.shape[1] // HEAD_DIM
    out_specs[2] = pl.BlockSpec((None, n_part, tm // CMP_STRIDE, CMP_STRIDE * HEAD_DIM), lambda b, s: (b, 0, s, 0))
    out_shape[2] = jax.ShapeDtypeStruct((B, n_part, S // CMP_STRIDE, CMP_STRIDE * HEAD_DIM), BF16)
    return pl.pallas_call(
        _in_kernel,
        grid=(B, S // tm),
        in_specs=[row(D), pl.BlockSpec((None, 6, D), lambda b, s: (b, 0, 0)), _full((1, D)), row(HEAD_DIM),
                  _full((1, HEAD_DIM)), _full(k_norm_g.shape)] + [_full(w.shape) for w in weights],
        out_specs=out_specs,
        out_shape=out_shape,
        scratch_shapes=[pltpu.VMEM((n_part, tm, HEAD_DIM), F32)],
        compiler_params=_params(("arbitrary", "arbitrary")),
        name="in_proj",
    )(x, mod3, norm1_g.reshape(1, D), ang, q_norm_g.reshape(1, HEAD_DIM), k_norm_g, *weights)


def _cmp_kernel(ak_ref, av_ref, pek_ref, pev_ref, wk1_ref, wk2_ref, wv1_ref, wv2_ref, kg_ref, kc_ref, vc_ref):
    half = CMP_STRIDE * HEAD_DIM

    def compress(a_ref, pe_ref, w1_ref, w2_ref):
        a = a_ref[...]
        u = _dot(a, w1_ref[0:half, :])
        v = _dot(a, w1_ref[half:2 * half, :])
        pw = _dot(pe_ref[...], w1_ref[...])
        pre = u + pltpu.roll(v, v.shape[0] - 1, 0) + pw[0:1, :]
        return _dot(_gelu_tanh(pre).astype(BF16), w2_ref[...])

    kc_ref[...] = _rms(compress(ak_ref, pek_ref, wk1_ref, wk2_ref), kg_ref[0:1, :]).astype(BF16)
    vc_ref[...] = compress(av_ref, pev_ref, wv1_ref, wv2_ref).T.astype(BF16)


def _compress(a, pe_k, pe_v, wk1, wk2, wv1, wv2, k_norm_g):
    B, _, NC, W = a.shape
    G = N_KV_GROUPS
    out = pl.BlockSpec((None, None, NC, HEAD_DIM), lambda b, g: (b, g, 0, 0))
    consts = (pe_k, pe_v, wk1, wk2, wv1, wv2, k_norm_g)
    return pl.pallas_call(
        _cmp_kernel,
        grid=(B, G),
        in_specs=[pl.BlockSpec((None, None, NC, W), lambda b, g: (b, g, 0, 0)),
                  pl.BlockSpec((None, None, NC, W), lambda b, g: (b, G + g, 0, 0))] + [_full(c.shape) for c in consts],
        out_specs=[out, out],
        out_shape=[jax.ShapeDtypeStruct((B, G, NC, HEAD_DIM), BF16)] * 2,
        compiler_params=_params(("arbitrary", "arbitrary")),
        name="compress",
    )(a, a, *consts)


def _select_kernel(qn_ref, kc_ref, vc_ref, gt_ref, ov_ref, ocmp_ref, bias_ref, *, n_blk):
    qi = pl.program_id(2)
    tq = qn_ref.shape[0]
    hpg = HEADS_PER_GROUP
    qn = jnp.concatenate([qn_ref[:, h * HEAD_DIM:(h + 1) * HEAD_DIM] for h in range(hpg)], axis=0)
    row = lax.broadcasted_iota(jnp.int32, (kc_ref.shape[0], tq), 0)
    t_q = qi * tq + lax.broadcasted_iota(jnp.int32, (kc_ref.shape[0], tq), 1)

    cmask = jnp.concatenate([row * CMP_STRIDE + (CMP_BLOCK - 1) <= t_q] * hpg, axis=1)
    s = jnp.where(cmask, _dot_nt(kc_ref[...], qn), NEG_INF)
    e = jnp.exp(s - jnp.max(s, axis=0, keepdims=True))
    p = jnp.where(cmask, e * (1.0 / jnp.sum(e, axis=0, keepdims=True)), 0.0)
    o_cmp = _dot(vc_ref[...], p.astype(BF16))
    gt = gt_ref[...]
    for h in range(hpg):
        c0 = h * N_NSA_BRANCHES
        ocmp_ref[:, h * HEAD_DIM:(h + 1) * HEAD_DIM] = (gt[:, c0:c0 + 1] * o_cmp[:, h * tq:(h + 1) * tq].T).astype(BF16)

    psum = p[:, 0:tq]
    for h in range(1, hpg):
        psum = psum + p[:, h * tq:(h + 1) * tq]
    imp = _dot(ov_ref[...], psum, precision=lax.Precision.HIGHEST)[0:n_blk]
    blk = lax.broadcasted_iota(jnp.int32, (n_blk, tq), 0).astype(F32)
    cur = ((qi * tq + lax.broadcasted_iota(jnp.int32, (n_blk, tq), 1)) // SEL_BLOCK).astype(F32)
    forced = (blk == 0.0) | (blk == cur) | (blk == cur - 1.0)
    score = jnp.where(forced, 1e6, jnp.where(blk <= cur, imp, -1e6))
    bias = jnp.full((n_blk, tq), NEG_INF, F32)
    for _ in range(min(N_SELECT, n_blk)):
        best = jnp.max(score, axis=0, keepdims=True)
        idx = jnp.min(jnp.where(score == best, blk, float(LANES)), axis=0, keepdims=True)
        hit = blk == idx
        bias = jnp.where(hit, 0.0, bias)
        score = jnp.where(hit, LOWEST, score)
    bias = jnp.concatenate([bias, jnp.zeros((LANES - n_blk, tq), F32)], axis=0)
    bias_ref[...] = bias.T.astype(BF16)


def _select(qn, kc, vc, gates, overlap, n_blk):
    B, S, _ = qn.shape
    G = N_KV_GROUPS
    tq = min(SELECT_ROWS, S)
    gw = HEADS_PER_GROUP * HEAD_DIM
    qspec = pl.BlockSpec((None, tq, gw), lambda b, g, i: (b, i, g))
    lane_spec = pl.BlockSpec((None, tq, LANES), lambda b, g, i: (b, i, g))
    cspec = pl.BlockSpec((None, None) + kc.shape[2:], lambda b, g, i: (b, g, 0, 0))
    return pl.pallas_call(
        functools.partial(_select_kernel, n_blk=n_blk),
        grid=(B, G, S // tq),
        in_specs=[qspec, cspec, cspec, lane_spec, _full(overlap.shape)],
        out_specs=[qspec, lane_spec],
        out_shape=[jax.ShapeDtypeStruct(qn.shape, BF16), jax.ShapeDtypeStruct((B, S, G * LANES), BF16)],
        compiler_params=_params(("arbitrary", "arbitrary", "arbitrary")),
        name="select",
    )(qn, kc, vc, gates, overlap)


def _nsa_kernel(qr_ref, bias_ref, ocmp_ref, ks_ref, vs_ref, kw_ref, vw_ref, gt_ref, ex_ref, o_ref, vst_sc, vwt_sc):
    qi = pl.program_id(2)
    tq = qr_ref.shape[0]
    hpg = HEADS_PER_GROUP
    seq = ks_ref.shape[0]

    @pl.when(qi == 0)
    def _():
        for kt in range(seq // KEY_TILE):
            rows = slice(kt * KEY_TILE, (kt + 1) * KEY_TILE)
            vst_sc[0:HEAD_DIM, rows] = vs_ref[rows, :].astype(F32).T.astype(BF16)
            vwt_sc[0:HEAD_DIM, rows] = vw_ref[rows, :].astype(F32).T.astype(BF16)
        ones = jnp.ones((vst_sc.shape[0] - HEAD_DIM, seq), BF16)
        vst_sc[HEAD_DIM:, :] = ones
        vwt_sc[HEAD_DIM:, :] = ones

    def pair(v):
        return jnp.concatenate([v] * HEAD_PAIR, axis=1)

    qr = jnp.concatenate([qr_ref[:, h * HEAD_DIM:(h + 1) * HEAD_DIM] for h in range(hpg)], axis=0)
    q_aug = jnp.concatenate([qr, jnp.concatenate([bias_ref[...]] * hpg, axis=0)], axis=1)
    pair_cols = [slice(hp * HEAD_PAIR * tq, (hp + 1) * HEAD_PAIR * tq) for hp in range(hpg // HEAD_PAIR)]

    span = WINDOW + tq
    w0 = pl.multiple_of(jnp.maximum(qi * tq - WINDOW, 0), KEY_TILE)
    kpos = w0 + lax.broadcasted_iota(jnp.int32, (span, tq), 0)
    t_w = qi * tq + lax.broadcasted_iota(jnp.int32, (span, tq), 1)
    window_bias = pair(jnp.where((kpos <= t_w) & (kpos > t_w - WINDOW), 0.0, NEG_INF))
    k_win = kw_ref[pl.ds(w0, span), :]
    win_scores = [_dot_nt(k_win, qr[cols]) + window_bias for cols in pair_cols]

    def softmax_pv(sc, vt):
        pr = jnp.exp2((sc - jnp.max(sc, axis=0, keepdims=True)).astype(BF16))
        o = _dot(vt, pr)
        return o[0:HEAD_DIM] * (1.0 / o[HEAD_DIM:HEAD_DIM + 1])

    n_full = (qi * tq) // SLC_TILE
    vt_win = vwt_sc[:, pl.ds(w0, span)]
    gt = gt_ref[...].T
    for v in range(seq // SLC_TILE):
        @pl.when(n_full == v)
        def _():
            ext = (v + 1) * SLC_TILE
            k_aug = jnp.concatenate([ks_ref[0:ext, :], ex_ref[0:ext, :]], axis=1)
            kpos = v * SLC_TILE + lax.broadcasted_iota(jnp.int32, (SLC_TILE, tq), 0)
            t_s = qi * tq + lax.broadcasted_iota(jnp.int32, (SLC_TILE, tq), 1)
            causal = pair(jnp.where(kpos <= t_s, 0.0, NEG_INF))
            scores = []
            for cols in pair_cols:
                sc = _dot_nt(k_aug, q_aug[cols])
                last = sc[ext - SLC_TILE:ext] + causal
                scores.append(last if v == 0 else jnp.concatenate([sc[0:ext - SLC_TILE], last], axis=0))
            o_wins = [softmax_pv(sc, vt_win) for sc in win_scores]
            o_slcs = [softmax_pv(sc, vst_sc[:, 0:ext]) for sc in scores]
            for h in range(hpg):
                hp, j = divmod(h, HEAD_PAIR)
                sub = slice(j * tq, (j + 1) * tq)
                c0 = h * N_NSA_BRANCHES
                o = gt[c0 + 1:c0 + 2, :] * o_slcs[hp][:, sub] + gt[c0 + 2:c0 + 3, :] * o_wins[hp][:, sub]
                hd = slice(h * HEAD_DIM, (h + 1) * HEAD_DIM)
                o_ref[:, hd] = (o.T + ocmp_ref[:, hd].astype(F32)).astype(BF16)


def _nsa(qr, bias, ocmp, kvr, gates, onehot):
    B, S, _ = qr.shape
    G = N_KV_GROUPS
    tq = Q_TILE
    gw = HEADS_PER_GROUP * HEAD_DIM
    qspec = pl.BlockSpec((None, tq, gw), lambda b, g, i: (b, i, g))
    lane_spec = pl.BlockSpec((None, tq, LANES), lambda b, g, i: (b, i, g))
    kv = lambda part: pl.BlockSpec((None, S, HEAD_DIM), lambda b, g, i: (b, 0, part * G + g))
    return pl.pallas_call(
        _nsa_kernel,
        grid=(B, G, S // tq),
        in_specs=[qspec, lane_spec, qspec, kv(0), kv(1), kv(2), kv(3), lane_spec, _full(onehot.shape)],
        out_specs=qspec,
        out_shape=jax.ShapeDtypeStruct(qr.shape, BF16),
        scratch_shapes=[pltpu.VMEM((HEAD_DIM + ONES_ROWS, S), BF16), pltpu.VMEM((HEAD_DIM + ONES_ROWS, S), BF16)],
        compiler_params=_params(("arbitrary", "arbitrary", "arbitrary")),
        name="nsa",
    )(qr, bias, ocmp, kvr, kvr, kvr, kvr, gates, onehot)


def _lru_kernel(lx_ref, lg_ref, cw_ref, cb_ref, wa_ref, ba_ref, wx_ref, bx_ref, lam_ref, o_ref,
                xs_sc, a_sc, u_sc, h_sc, carry_sc):
    ts = lx_ref.shape[0]
    width = lx_ref.shape[1]
    bw = width // LRU_BLOCKS

    @pl.when(pl.program_id(1) == 0)
    def _():
        xs_sc[0:SUBLANES, :] = jnp.zeros((SUBLANES, width), F32)
        carry_sc[...] = jnp.zeros(carry_sc.shape, F32)

    xs_sc[SUBLANES:SUBLANES + ts, :] = lx_ref[...].astype(F32)
    xc = cb_ref[...] + cw_ref[CONV_WIDTH - 1:CONV_WIDTH, :] * xs_sc[SUBLANES:SUBLANES + ts, :]
    for d in range(1, CONV_WIDTH):
        w = cw_ref[CONV_WIDTH - 1 - d:CONV_WIDTH - d, :]
        xc = xc + w * xs_sc[SUBLANES - d:SUBLANES - d + ts, :]
    xs_sc[0:SUBLANES, :] = xs_sc[ts:ts + SUBLANES, :]

    lam = -lam_ref[...]
    neg_c_softplus = -LRU_C * (jnp.maximum(lam, 0.0) + jnp.log1p(jnp.exp(-jnp.abs(lam))))
    xcb = xc.astype(BF16)
    for blk in range(LRU_BLOCKS):
        sl = slice(blk * bw, (blk + 1) * bw)
        r = _sigmoid(_dot(xcb[:, sl], wa_ref[blk]) + ba_ref[:, sl])
        i = _sigmoid(_dot(xcb[:, sl], wx_ref[blk]) + bx_ref[:, sl])
        log_a = r * neg_c_softplus[:, sl]
        a = jnp.exp(log_a)
        a_sc[:, sl] = a
        u_sc[:, sl] = jnp.sqrt(-jnp.tanh(log_a) * (a * a + 1.0)) * (i * xc[:, sl])

    row = lax.broadcasted_iota(jnp.int32, (SUBLANES, width), 0)

    def chunk(c, h_prev):
        r0 = pl.multiple_of(c * SUBLANES, SUBLANES)
        a = a_sc[pl.ds(r0, SUBLANES), :]
        u = u_sc[pl.ds(r0, SUBLANES), :]
        for d in (1, 2, 4):
            keep = row >= d
            u = jnp.where(keep, a * pltpu.roll(u, d, 0) + u, u)
            a = jnp.where(keep, a * pltpu.roll(a, d, 0), a)
        h = a * h_prev + u
        h_sc[pl.ds(r0, SUBLANES), :] = h
        return jnp.broadcast_to(h[SUBLANES - 1:SUBLANES, :], h.shape)

    carry_sc[...] = lax.fori_loop(0, ts // SUBLANES, chunk, carry_sc[...])
    o_ref[...] = (_gelu_tanh(lg_ref[...].astype(F32)) * h_sc[...]).astype(BF16)


def _lru(lx, lg, conv_w, conv_b, wa, ba, wx, bx, lam):
    B, S, W = lx.shape
    ts = min(LRU_ROWS, S)
    row = pl.BlockSpec((None, ts, W), lambda b, s: (b, s, 0))
    consts = (conv_w, conv_b.reshape(1, W), wa, ba.reshape(1, W), wx, bx.reshape(1, W), lam.reshape(1, W))
    return pl.pallas_call(
        _lru_kernel,
        grid=(B, S // ts),
        in_specs=[row, row] + [_full(a.shape) for a in consts],
        out_specs=row,
        out_shape=jax.ShapeDtypeStruct((B, S, W), BF16),
        scratch_shapes=[pltpu.VMEM((ts + SUBLANES, W), F32), pltpu.VMEM((ts, W), F32), pltpu.VMEM((ts, W), F32),
                        pltpu.VMEM((ts, W), F32), pltpu.VMEM((SUBLANES, W), F32)],
        compiler_params=_params(("arbitrary", "arbitrary")),
        name="lru",
    )(lx, lg, *consts)


def _merge_kernel(on_ref, ol_ref, mg_ref, x_ref, mod_ref, g2_ref, wn_ref, wl_ref, wo_ref, rw_ref, rb_ref, tri_ref,
                  x1_ref, h2_ref, rt_ref, cnt_ref, carry_sc):
    first = (pl.program_id(0) == 0) & (pl.program_id(1) == 0)

    @pl.when(first)
    def _():
        carry_sc[...] = jnp.zeros(carry_sc.shape, F32)

    d = x_ref.shape[1]
    y_nsa = _dot(on_ref[...], wn_ref[...])
    y_lru = _dot(ol_ref[...], wl_ref[...])
    merged = mg_ref[:, 0:d].astype(F32) * y_nsa + mg_ref[:, d:2 * d].astype(F32) * y_lru
    gate1 = mod_ref[2:3, :]
    shift2 = mod_ref[3:4, :]
    scale2 = mod_ref[4:5, :]
    x1 = x_ref[...] + gate1 * _dot(merged.astype(BF16), wo_ref[...])
    x1_ref[...] = x1
    h2 = _rms(x1, g2_ref[...]) * (1.0 + scale2) + shift2
    h2_ref[...] = _pack_bf16_pairs(h2)

    rw = rw_ref[...]
    rw_hi = rw.astype(BF16)
    rw_lo = (rw - rw_hi.astype(F32)).astype(BF16)
    h2_hi = h2.astype(BF16)
    h2_lo = (h2 - h2_hi.astype(F32)).astype(BF16)
    logits = _dot_nt(rw_hi, h2_hi) + (_dot_nt(rw_hi, h2_lo) + _dot_nt(rw_lo, h2_hi))
    logits = logits[0:N_EXPERTS] + rb_ref[...]
    tm = logits.shape[1]
    eid = lax.broadcasted_iota(jnp.int32, logits.shape, 0).astype(F32)
    score = logits
    picks = []
    onehot = jnp.zeros(logits.shape, F32)
    for _ in range(TOP_K):
        best = jnp.max(score, axis=0, keepdims=True)
        idx = jnp.min(jnp.where(score == best, eid, float(LANES)), axis=0, keepdims=True)
        hit = eid == idx
        picks.append((idx, best, hit))
        onehot = jnp.where(hit, 1.0, onehot)
        score = jnp.where(hit, LOWEST, score)
    ew = [jnp.exp(v - picks[0][1]) for _, v, _ in picks]
    den = ew[0]
    for v in ew[1:]:
        den = den + v
    inv_den = 1.0 / den

    before = _dot(onehot.astype(BF16), tri_ref[...]) + carry_sc[:, 0:1]
    carry_sc[...] = carry_sc[...] + jnp.sum(onehot, axis=1, keepdims=True)
    cnt_ref[...] = carry_sc[...]

    rows = [idx for idx, _, _ in picks]
    rows += [jnp.sum(jnp.where(hit, before, 0.0), axis=0, keepdims=True) for _, _, hit in picks]
    rows += [e * inv_den for e in ew]
    rows.append(jnp.zeros((LANES - len(rows), tm), F32))
    rt_ref[...] = jnp.concatenate(rows, axis=0).T


def _merge(o_nsa, o_lru, mg, x, mod3, norm2_g, wn, wl, wo, rw, rb, tri):
    B, S, D = x.shape
    tm = tri.shape[0]
    row = lambda w: pl.BlockSpec((None, tm, w), lambda b, s: (b, s, 0))
    consts = (norm2_g.reshape(1, D), wn, wl, wo, rw, rb, tri)
    return pl.pallas_call(
        _merge_kernel,
        grid=(B, S // tm),
        in_specs=[row(D), row(D), row(2 * D), row(D), pl.BlockSpec((None, 6, D), lambda b, s: (b, 0, 0))]
                 + [_full(a.shape) for a in consts],
        out_specs=[row(D), row(D // 2), row(LANES), _full((N_EXPERTS, LANES))],
        out_shape=[jax.ShapeDtypeStruct((B, S, D), F32), jax.ShapeDtypeStruct((B, S, D // 2), jnp.int32),
                   jax.ShapeDtypeStruct((B, S, LANES), F32), jax.ShapeDtypeStruct((N_EXPERTS, LANES), F32)],
        scratch_shapes=[pltpu.VMEM((N_EXPERTS, LANES), F32)],
        compiler_params=_params(("arbitrary", "arbitrary")),
        name="merge",
    )(o_nsa, o_lru, mg, x, mod3, *consts)


def _expert_kernel(be_ref, nb_ref, x_ref, w1_ref, b1_ref, w2_ref, b2_ref, pick_ref, y_ref, w1_sc, w2_sc):
    i = pl.program_id(0)
    ff = w2_ref.shape[0]
    chunk = pick_ref.shape[0]

    @pl.when((i == 0) | (be_ref[i] != be_ref[jnp.maximum(i - 1, 0)]))
    def _():
        for c in range(2 * ff // chunk):
            r = _dot(w1_ref[:, c * chunk:(c + 1) * chunk].astype(BF16), pick_ref[...])
            half = chunk // 2
            w1_sc[:, c * half:(c + 1) * half] = r[:, 0:half].astype(BF16)
            w1_sc[:, ff + c * half:ff + (c + 1) * half] = r[:, half:chunk].astype(BF16)
        w2_sc[...] = w2_ref[...].astype(BF16)

    @pl.when(i < nb_ref[0])
    def _():
        h = _dot(_unpack_bf16_pairs(x_ref[...]).astype(BF16), w1_sc[...]) + b1_ref[...]
        x_glu = jnp.minimum(h[:, 0:ff], SWIGLU_LIMIT)
        x_lin = jnp.clip(h[:, ff:2 * ff], -SWIGLU_LIMIT, SWIGLU_LIMIT)
        act = x_glu * _sigmoid(SWIGLU_ALPHA * x_glu) * (x_lin + 1.0)
        y_ref[...] = _pack_bf16_pairs(_dot(act.astype(BF16), w2_sc[...]) + b2_ref[...])

    @pl.when(i >= nb_ref[0])
    def _():
        y_ref[...] = jnp.zeros(y_ref.shape, jnp.int32)


def _experts(blk_e, n_used, buf, w1, b1, w2, b2):
    n_rows = buf.shape[0]
    F, D = w2.shape[1], w2.shape[2]
    tm = EXPERT_ROWS
    chunk = 2 * LANES
    pick = np.zeros((chunk, chunk), np.float32)
    pick[2 * np.arange(LANES), np.arange(LANES)] = 1.0
    pick[2 * np.arange(LANES) + 1, LANES + np.arange(LANES)] = 1.0
    wspec = lambda a, b: pl.BlockSpec((None, a, b), lambda i, be, nb: (be[i], 0, 0))
    return pl.pallas_call(
        _expert_kernel,
        grid_spec=pltpu.PrefetchScalarGridSpec(
            num_scalar_prefetch=2,
            grid=(n_rows // tm,),
            in_specs=[pl.BlockSpec((tm, D // 2), lambda i, be, nb: (i, 0)), wspec(D, 2 * F), wspec(1, 2 * F), wspec(F, D),
                      wspec(1, D), pl.BlockSpec((chunk, chunk), lambda i, be, nb: (0, 0))],
            out_specs=pl.BlockSpec((tm, D // 2), lambda i, be, nb: (i, 0)),
            scratch_shapes=[pltpu.VMEM((D, 2 * F), BF16), pltpu.VMEM((F, D), BF16)],
        ),
        out_shape=jax.ShapeDtypeStruct((n_rows, D // 2), jnp.int32),
        compiler_params=_params(("arbitrary",)),
        name="experts",
    )(blk_e, n_used, buf, w1, b1, w2, b2, jnp.asarray(pick, dtype=BF16))


def _sc_mesh():
    return plsc.VectorSubcoreMesh(core_axis_name="c", subcore_axis_name="s")


def _sc_worker():
    return lax.axis_index("s") * SC_CORES + lax.axis_index("c")


def _sc_scatter_rows(rows, dest, n_rows):
    T, W = rows.shape
    K = dest.shape[0]
    n_workers = SC_CORES * SC_SUBCORES
    per_w = T // n_workers
    ch = SC_CHUNK
    n_ch = per_w // ch
    assert per_w * n_workers == T and n_ch * ch == per_w and n_ch % 2 == 0
    dest4 = dest.reshape(K, n_workers, n_ch, ch).transpose(1, 2, 0, 3)

    @functools.partial(
        pl.kernel, mesh=_sc_mesh(), out_type=jax.ShapeDtypeStruct((n_rows, W), rows.dtype),
        scratch_types=[pltpu.VMEM((n_ch, K, ch), jnp.int32), pltpu.VMEM((2, ch, W), rows.dtype),
                       pltpu.SemaphoreType.DMA((2,)), pltpu.SemaphoreType.DMA((2,))])
    def scatter(rows_hbm, dest_hbm, out_hbm, idx_v, rows_v, read_sem, write_sem):
        wid = _sc_worker()
        base = wid * per_w
        pltpu.sync_copy(dest_hbm.at[wid], idx_v)

        def read(c, b):
            src = rows_hbm.at[pl.ds(pl.multiple_of(base + c * ch, ch), ch)]
            return pltpu.make_async_copy(src, rows_v.at[b], read_sem.at[b])

        def write(c, b, k):
            return pltpu.make_async_copy(rows_v.at[b], out_hbm.at[idx_v.at[c, k]], write_sem.at[b])

        read(0, 0).start()

        @pl.loop(0, n_ch, step=2)
        def _(i):
            for b in range(2):
                c = i + b
                read(c, b).wait()

                @pl.when(c >= 1)
                def _():
                    for k in range(K):
                        write(c - 1, 1 - b, k).wait()

                @pl.when(c + 1 < n_ch)
                def _():
                    read(c + 1, 1 - b).start()

                for k in range(K):
                    write(c, b, k).start()

        for k in range(K):
            write(n_ch - 1, 1, k).wait()

    return scatter(rows, dest4)


def _sc_gather_rows(table, idx):
    N = idx.shape[0]
    W = table.shape[1]
    n_workers = SC_CORES * SC_SUBCORES
    per_w = N // n_workers
    ch = SC_CHUNK
    n_ch = per_w // ch
    assert per_w * n_workers == N and n_ch * ch == per_w and n_ch % 2 == 0
    idx3 = idx.reshape(n_workers, n_ch, ch)

    @functools.partial(
        pl.kernel, mesh=_sc_mesh(), out_type=jax.ShapeDtypeStruct((N, W), table.dtype),
        scratch_types=[pltpu.VMEM((n_ch, ch), jnp.int32), pltpu.VMEM((2, ch, W), table.dtype),
                       pltpu.SemaphoreType.DMA((2,)), pltpu.SemaphoreType.DMA((2,))])
    def gather(table_hbm, idx_hbm, out_hbm, idx_v, rows_v, gather_sem, write_sem):
        wid = _sc_worker()
        base = wid * per_w
        pltpu.sync_copy(idx_hbm.at[wid], idx_v)

        def fetch(c, b):
            return pltpu.make_async_copy(table_hbm.at[idx_v.at[c]], rows_v.at[b], gather_sem.at[b])

        def write(c, b):
            dst = out_hbm.at[pl.ds(pl.multiple_of(base + c * ch, ch), ch)]
            return pltpu.make_async_copy(rows_v.at[b], dst, write_sem.at[b])

        fetch(0, 0).start()

        @pl.loop(0, n_ch, step=2)
        def _(i):
            for b in range(2):
                c = i + b
                fetch(c, b).wait()

                @pl.when(c >= 1)
                def _():
                    write(c - 1, 1 - b).wait()

                @pl.when(c + 1 < n_ch)
                def _():
                    fetch(c + 1, 1 - b).start()

                write(c, b).start()

        write(n_ch - 1, 1).wait()

    return gather(table, idx3)


def _combine_kernel(x1_ref, yg_ref, rt_ref, mod_ref, *rest):
    o_ref = rest[-1]
    gate2 = mod_ref[5:6, :]
    rt = rt_ref[...]
    acc = rt[:, 2 * TOP_K:2 * TOP_K + 1] * _unpack_bf16_pairs(yg_ref[0])
    for k in range(1, TOP_K):
        acc = acc + rt[:, 2 * TOP_K + k:2 * TOP_K + k + 1] * _unpack_bf16_pairs(yg_ref[k])
    o_ref[...] = x1_ref[...] + gate2 * acc


def _combine(x1, yg, rt, mod3, prev, b0):
    B, S, D = x1.shape
    nb = yg.shape[1]
    tm = min(COMBINE_ROWS, S)
    row = lambda w: pl.BlockSpec((None, tm, w), lambda b, s: (b0 + b, s, 0))
    in_specs = [row(D), pl.BlockSpec((TOP_K, None, tm, D // 2), lambda b, s: (0, b, s, 0)), row(LANES),
                pl.BlockSpec((None, 6, D), lambda b, s: (b0 + b, 0, 0))]
    operands = [x1, yg, rt, mod3]
    if prev is not None:
        in_specs.append(pl.BlockSpec(memory_space=pl.ANY))
        operands.append(prev)
    return pl.pallas_call(
        _combine_kernel,
        grid=(nb, S // tm),
        in_specs=in_specs,
        out_specs=row(D),
        out_shape=jax.ShapeDtypeStruct((B, S, D), F32),
        input_output_aliases={} if prev is None else {4: 0},
        compiler_params=_params(("arbitrary", "arbitrary")),
        name="combine",
    )(*operands)


def _overlap_matrix(n_cmp_pad, n_blk):
    cs = np.arange(n_cmp_pad)[:, None] * CMP_STRIDE
    js = np.arange(LANES)[None, :] * SEL_BLOCK
    m = (cs <= js + SEL_BLOCK - 1) & (cs + CMP_BLOCK - 1 >= js) & (np.arange(LANES)[None, :] < n_blk)
    return jnp.asarray(m.astype(np.float32).T)


def _block_onehot(seq):
    key = np.arange(seq)[:, None]
    j = np.arange(LANES)[None, :]
    return jnp.asarray((j == key // SEL_BLOCK).astype(np.float32), dtype=BF16)


def _layer(x, c, ang, ada_w, ada_b, norm1_g, w_in, q_norm_g, k_norm_g, cmp_pe_k, cmp_pe_v, cmp_wk1, cmp_wk2, cmp_wv1,
           cmp_wv2, conv_w, conv_b, lru_wa, lru_ba, lru_wx, lru_bx, lru_lambda, w_branch_out, w_out, norm2_g,
           router_w, router_b, moe_w1, moe_b1, moe_w2, moe_b2):
    B, S, D = x.shape
    T = B * S
    G = N_KV_GROUPS
    nsa_w = N_HEADS * HEAD_DIM
    kv_w = G * HEAD_DIM

    mod3 = _ada(c, ada_w, ada_b).reshape(B, 6, D)

    o = 0
    wq = w_in[:, o:o + nsa_w]; o += nsa_w
    wkc = w_in[:, o:o + 2 * kv_w]; o += 2 * kv_w
    wkr = w_in[:, o:o + 4 * kv_w]; o += 4 * kv_w
    n_gate = N_HEADS * N_NSA_BRANCHES
    wg_raw = w_in[:, o:o + n_gate].reshape(D, G, n_gate // G); o += n_gate
    wg = jnp.pad(wg_raw, ((0, 0), (0, 0), (0, LANES - n_gate // G))).reshape(D, G * LANES)
    wlx = w_in[:, o:o + D]; o += D
    wlg = w_in[:, o:o + D]; o += D
    wm = w_in[:, o:o + 2 * D]
    bf = lambda a: a.astype(BF16)

    qn, qr, kvc, kvr, gates, lx, lg, mg = _in_proj(x, mod3, norm1_g, ang, q_norm_g, k_norm_g, bf(wq), bf(wkc), bf(wkr),
                                                   bf(wg), bf(wlx), bf(wlg), bf(wm))

    nc = S // CMP_STRIDE
    pe = lambda p: jnp.broadcast_to(p.reshape(1, CMP_BLOCK * HEAD_DIM), (SUBLANES, CMP_BLOCK * HEAD_DIM)).astype(BF16)
    kc, vc = _compress(kvc, pe(cmp_pe_k), pe(cmp_pe_v), bf(cmp_wk1), bf(cmp_wk2), bf(cmp_wv1), bf(cmp_wv2), k_norm_g)

    ocmp, bias = _select(qn, kc, vc, gates, _overlap_matrix(nc, S // SEL_BLOCK), S // SEL_BLOCK)
    o_nsa = _nsa(qr, bias, ocmp, kvr, gates, _block_onehot(S))
    o_lru = _lru(lx, lg, conv_w, conv_b, bf(lru_wa), lru_ba, bf(lru_wx), lru_bx, lru_lambda)

    tm = min(IN_ROWS, S)
    tri = jnp.asarray(np.triu(np.ones((tm, tm), np.float32), 1), dtype=BF16)
    rw = jnp.pad(router_w.T, ((0, LANES - N_EXPERTS), (0, 0)))
    rb = router_b.reshape(N_EXPERTS, 1)
    x1, h2, rt, cnt = _merge(o_nsa, o_lru, mg, x, mod3, norm2_g, bf(w_branch_out[:nsa_w]), bf(w_branch_out[nsa_w:]),
                             bf(w_out), rw, rb, tri)

    rt2 = rt.reshape(T, LANES)
    top_e = rt2[:, 0:TOP_K].astype(jnp.int32)
    rank = rt2[:, TOP_K:2 * TOP_K].astype(jnp.int32)
    counts = cnt[:, 0].astype(jnp.int32)
    padded = (counts + EXPERT_ROWS - 1) // EXPERT_ROWS * EXPERT_ROWS
    ends = jnp.cumsum(padded)
    start = ends - padded
    dest = start[top_e] + rank
    n_rows = -(-(T * TOP_K + N_EXPERTS * (EXPERT_ROWS - 1)) // EXPERT_ROWS) * EXPERT_ROWS
    n_blocks = n_rows // EXPERT_ROWS
    blk_first = jnp.arange(n_blocks, dtype=jnp.int32) * EXPERT_ROWS
    blk_e = jnp.minimum(jnp.sum(ends[None, :] <= blk_first[:, None], axis=1), N_EXPERTS - 1).astype(jnp.int32)
    n_used = (ends[-1:] // EXPERT_ROWS).astype(jnp.int32)

    dest_kt = dest.T
    buf = _sc_scatter_rows(h2.reshape(T, D // 2), dest_kt, n_rows)

    b1 = jnp.concatenate([moe_b1[:, 0::2], moe_b1[:, 1::2]], axis=1)
    ybuf = _experts(blk_e, n_used, buf, moe_w1, b1.reshape(N_EXPERTS, 1, -1), moe_w2, moe_b2.reshape(N_EXPERTS, 1, D))
    nb = B // COMBINE_PARTS if B % COMBINE_PARTS == 0 else B
    dest_parts = dest_kt.reshape(TOP_K, B // nb, nb * S)
    out = None
    for p in range(B // nb):
        yg = _sc_gather_rows(ybuf, dest_parts[:, p].reshape(-1)).reshape(TOP_K, nb, S, D // 2)
        out = _combine(x1, yg, rt, mod3, out, p * nb)
    return out


def kernel(x, c, positions, ada_w, ada_b, norm1_g, w_in, q_norm_g, k_norm_g, cmp_pe_k, cmp_pe_v, cmp_wk1, cmp_wk2, cmp_wv1, cmp_wv2, conv_w, conv_b, lru_wa, lru_ba, lru_wx, lru_bx, lru_lambda, w_branch_out, w_out, norm2_g, router_w, router_b, moe_w1, moe_b1, moe_w2, moe_b2):
    inv = ROPE_THETA ** (-jnp.arange(0, HEAD_DIM, 2, dtype=F32) / HEAD_DIM)
    ang = positions.astype(F32)[..., None] * inv
    ang = jnp.concatenate([ang, ang], axis=-1)
    for l in range(ada_w.shape[0]):
        x = _layer(x, c, ang, ada_w[l], ada_b[l], norm1_g[l], w_in[l], q_norm_g[l], k_norm_g[l], cmp_pe_k[l],
                   cmp_pe_v[l], cmp_wk1[l], cmp_wk2[l], cmp_wv1[l], cmp_wv2[l], conv_w[l], conv_b[l], lru_wa[l],
                   lru_ba[l], lru_wx[l], lru_bx[l], lru_lambda[l], w_branch_out[l], w_out[l], norm2_g[l], router_w[l],
                   router_b[l], moe_w1[l], moe_b1[l], moe_w2[l], moe_b2[l])
    return x
```

```python
import functools
import math

import jax
import jax.numpy as jnp
import numpy as np
from jax import lax
from jax.experimental import pallas as pl
from jax.experimental.pallas import tpu as pltpu
from jax.experimental.pallas import tpu_sc as plsc

F32 = jnp.float32
BF16 = jnp.bfloat16

N_HEADS = 8
HEAD_DIM = 128
N_KV_GROUPS = 2
HEADS_PER_GROUP = N_HEADS // N_KV_GROUPS
N_NSA_BRANCHES = 3
CMP_BLOCK = 32
CMP_STRIDE = 16
CMP_HIDDEN = 256
SEL_BLOCK = 64
N_SELECT = 8
WINDOW = 512
ROPE_THETA = 10000.0
LRU_BLOCKS = 4
CONV_WIDTH = 4
LRU_C = 8.0
N_EXPERTS = 32
TOP_K = 4
SWIGLU_LIMIT = 7.0
SWIGLU_ALPHA = 1.702
RMS_EPS = 1e-6
NEG_INF = -1e30
LOWEST = -3.0e38
LOG2_E = 1.4426950408889634

LANES = 128
SUBLANES = 8
VMEM_LIMIT = 56 * 1024 * 1024
SC_CORES = 2
SC_SUBCORES = 16
SC_CHUNK = 64

Q_TILE = 256
SELECT_ROWS = 1024
KEY_TILE = 128
SLC_TILE = 512
ONES_ROWS = 16
HEAD_PAIR = 2
IN_ROWS = 512
LRU_ROWS = 512
EXPERT_ROWS = 512
COMBINE_ROWS = 256
COMBINE_PARTS = 4


def _sigmoid(v):
    return 0.5 * jnp.tanh(0.5 * v) + 0.5


def _gelu_tanh(v):
    c = math.sqrt(2.0 / math.pi)
    half = 0.5 * v
    return half + half * jnp.tanh(v * (c + (c * 0.044715) * (v * v)))


def _rms(v, g):
    return v * lax.rsqrt(jnp.mean(v * v, axis=-1, keepdims=True) + RMS_EPS) * g


def _dot(a, b, **kw):
    return jnp.dot(a, b, preferred_element_type=F32, **kw)


def _dot_nt(a, b, **kw):
    return lax.dot_general(a, b, (((1,), (1,)), ((), ())), preferred_element_type=F32, **kw)


def _pack_bf16_pairs(v):
    n = v.shape[1] // 2
    lo = lax.bitcast_convert_type(v[:, 0:n].astype(BF16).astype(F32), jnp.int32)
    hi = lax.bitcast_convert_type(v[:, n:2 * n].astype(BF16).astype(F32), jnp.int32)
    return lax.shift_right_logical(lo, 16) | hi


def _unpack_bf16_pairs(w):
    lo = lax.bitcast_convert_type(lax.shift_left(w, 16), F32)
    hi = lax.bitcast_convert_type(w & jnp.int32(-65536), F32)
    return jnp.concatenate([lo, hi], axis=1)


def _full(shape):
    nd = len(shape)
    return pl.BlockSpec(shape, lambda *_: (0,) * nd)


def _params(sem):
    return pltpu.CompilerParams(dimension_semantics=sem, vmem_limit_bytes=VMEM_LIMIT)


def _ada_kernel(c_ref, w_ref, b_ref, o_ref):
    c = c_ref[...]
    o_ref[...] = _dot(c * _sigmoid(c), w_ref[...], precision=lax.Precision.HIGHEST) + b_ref[...]


def _ada(c, ada_w, ada_b):
    B, D = c.shape
    N = ada_w.shape[1]
    return pl.pallas_call(
        _ada_kernel,
        grid=(N // D,),
        in_specs=[_full((B, D)), pl.BlockSpec((D, D), lambda j: (0, j)), pl.BlockSpec((1, D), lambda j: (0, j))],
        out_specs=pl.BlockSpec((B, D), lambda j: (0, j)),
        out_shape=jax.ShapeDtypeStruct((B, N), F32),
        compiler_params=_params(("arbitrary",)),
        name="ada",
    )(c, ada_w, ada_b.reshape(1, N))


def _in_kernel(x_ref, mod_ref, g1_ref, ang_ref, qg_ref, kg_ref, wq_ref, wkc_ref, wkr_ref, wg_ref, wlx_ref, wlg_ref,
               wm_ref, qn_ref, qr_ref, kvc_ref, kvr_ref, gt_ref, lx_ref, lg_ref, mg_ref, kvc_sc):
    x = x_ref[...]
    shift1 = mod_ref[0:1, :]
    scale1 = mod_ref[1:2, :]
    h = _rms(x, g1_ref[...]) * (1.0 + scale1) + shift1
    hb = h.astype(BF16)

    ang = ang_ref[...]
    cos = jnp.cos(ang)
    sin = jnp.sin(ang)
    lane = lax.broadcasted_iota(jnp.int32, ang.shape, 1)
    sin_signed = jnp.where(lane < HEAD_DIM // 2, -sin, sin)

    def rope(v):
        return v * cos + pltpu.roll(v, HEAD_DIM // 2, 1) * sin_signed

    q = _dot(hb, wq_ref[...])
    for hh in range(N_HEADS):
        sl = slice(hh * HEAD_DIM, (hh + 1) * HEAD_DIM)
        qh = _rms(q[:, sl], qg_ref[...])
        qn_ref[:, sl] = (qh * HEAD_DIM ** -0.5).astype(BF16)
        qr_ref[:, sl] = (rope(qh) * (HEAD_DIM ** -0.5 * LOG2_E)).astype(BF16)

    kvc = _dot(hb, wkc_ref[...])
    for part in range(kvc_ref.shape[0]):
        kvc_sc[part] = kvc[:, part * HEAD_DIM:(part + 1) * HEAD_DIM]
        for tok in range(CMP_STRIDE):
            piece = kvc_sc[part, pl.ds(tok, kvc_ref.shape[1], stride=CMP_STRIDE), :]
            kvc_ref[part, :, tok * HEAD_DIM:(tok + 1) * HEAD_DIM] = piece.astype(BF16)

    kvr = _dot(hb, wkr_ref[...])
    kvw = N_KV_GROUPS * HEAD_DIM
    for part in range(4):
        for gg in range(N_KV_GROUPS):
            sl = slice(part * kvw + gg * HEAD_DIM, part * kvw + (gg + 1) * HEAD_DIM)
            v = kvr[:, sl]
            if part % 2 == 0:
                row = 1 + part // 2
                v = rope(_rms(v, kg_ref[row:row + 1, :]))
            kvr_ref[:, sl] = v.astype(BF16)

    gt_ref[...] = _sigmoid(_dot(hb, wg_ref[...]))
    lx_ref[...] = _dot(hb, wlx_ref[...]).astype(BF16)
    lg_ref[...] = _dot(hb, wlg_ref[...]).astype(BF16)
    mg_ref[...] = _sigmoid(_dot(hb, wm_ref[...])).astype(BF16)


def _in_proj(x, mod3, norm1_g, ang, q_norm_g, k_norm_g, wq, wkc, wkr, wg, wlx, wlg, wm):
    B, S, D = x.shape
    tm = min(IN_ROWS, S)
    row = lambda w: pl.BlockSpec((None, tm, w), lambda b, s: (b, s, 0))
    widths = (wq.shape[1], wq.shape[1], wkc.shape[1], wkr.shape[1], wg.shape[1], wlx.shape[1], wlg.shape[1], wm.shape[1])
    dtypes = (BF16, BF16, BF16, BF16, F32, BF16, BF16, BF16)
    weights = (wq, wkc, wkr, wg, wlx, wlg, wm)
    out_specs = [row(w) for w in widths]
    out_shape = [jax.ShapeDtypeStruct((B, S, w), dt) for w, dt in zip(widths, dtypes)]
    n_part = wkc.shape[1] // HEAD_DIM
    out_specs[2] = pl.BlockSpec((None, n_part, tm // CMP_STRIDE, CMP_STRIDE * HEAD_DIM), lambda b, s: (b, 0, s, 0))
    out_shape[2] = jax.ShapeDtypeStruct((B, n_part, S // CMP_STRIDE, CMP_STRIDE * HEAD_DIM), BF16)
    return pl.pallas_call(
        _in_kernel,
        grid=(B, S // tm),
        in_specs=[row(D), pl.BlockSpec((None, 6, D), lambda b, s: (b, 0, 0)), _full((1, D)), row(HEAD_DIM),
                  _full((1, HEAD_DIM)), _full(k_norm_g.shape)] + [_full(w.shape) for w in weights],
        out_specs=out_specs,
        out_shape=out_shape,
        scratch_shapes=[pltpu.VMEM((n_part, tm, HEAD_DIM), F32)],
        compiler_params=_params(("arbitrary", "arbitrary")),
        name="in_proj",
    )(x, mod3, norm1_g.reshape(1, D), ang, q_norm_g.reshape(1, HEAD_DIM), k_norm_g, *weights)


def _cmp_kernel(ak_ref, av_ref, pek_ref, pev_ref, wk1_ref, wk2_ref, wv1_ref, wv2_ref, kg_ref, kc_ref, vc_ref):
    half = CMP_STRIDE * HEAD_DIM

    def compress(a_ref, pe_ref, w1_ref, w2_ref):
        a = a_ref[...]
        u = _dot(a, w1_ref[0:half, :])
        v = _dot(a, w1_ref[half:2 * half, :])
        pw = _dot(pe_ref[...], w1_ref[...])
        pre = u + pltpu.roll(v, v.shape[0] - 1, 0) + pw[0:1, :]
        return _dot(_gelu_tanh(pre).astype(BF16), w2_ref[...])

    kc_ref[...] = _rms(compress(ak_ref, pek_ref, wk1_ref, wk2_ref), kg_ref[0:1, :]).astype(BF16)
    vc_ref[...] = compress(av_ref, pev_ref, wv1_ref, wv2_ref).T.astype(BF16)


def _compress(a, pe_k, pe_v, wk1, wk2, wv1, wv2, k_norm_g):
    B, _, NC, W = a.shape
    G = N_KV_GROUPS
    out = pl.BlockSpec((None, None, NC, HEAD_DIM), lambda b, g: (b, g, 0, 0))
    consts = (pe_k, pe_v, wk1, wk2, wv1, wv2, k_norm_g)
    return pl.pallas_call(
        _cmp_kernel,
        grid=(B, G),
        in_specs=[pl.BlockSpec((None, None, NC, W), lambda b, g: (b, g, 0, 0)),
                  pl.BlockSpec((None, None, NC, W), lambda b, g: (b, G + g, 0, 0))] + [_full(c.shape) for c in consts],
        out_specs=[out, out],
        out_shape=[jax.ShapeDtypeStruct((B, G, NC, HEAD_DIM), BF16)] * 2,
        compiler_params=_params(("arbitrary", "arbitrary")),
        name="compress",
    )(a, a, *consts)


def _select_kernel(qn_ref, kc_ref, vc_ref, gt_ref, ov_ref, ocmp_ref, bias_ref, *, n_blk):
    qi = pl.program_id(2)
    tq = qn_ref.shape[0]
    hpg = HEADS_PER_GROUP
    qn = jnp.concatenate([qn_ref[:, h * HEAD_DIM:(h + 1) * HEAD_DIM] for h in range(hpg)], axis=0)
    row = lax.broadcasted_iota(jnp.int32, (kc_ref.shape[0], tq), 0)
    t_q = qi * tq + lax.broadcasted_iota(jnp.int32, (kc_ref.shape[0], tq), 1)

    cmask = jnp.concatenate([row * CMP_STRIDE + (CMP_BLOCK - 1) <= t_q] * hpg, axis=1)
    s = jnp.where(cmask, _dot_nt(kc_ref[...], qn), NEG_INF)
    e = jnp.exp(s - jnp.max(s, axis=0, keepdims=True))
    p = jnp.where(cmask, e * (1.0 / jnp.sum(e, axis=0, keepdims=True)), 0.0)
    o_cmp = _dot(vc_ref[...], p.astype(BF16))
    gt = gt_ref[...]
    for h in range(hpg):
        c0 = h * N_NSA_BRANCHES
        ocmp_ref[:, h * HEAD_DIM:(h + 1) * HEAD_DIM] = (gt[:, c0:c0 + 1] * o_cmp[:, h * tq:(h + 1) * tq].T).astype(BF16)

    psum = p[:, 0:tq]
    for h in range(1, hpg):
        psum = psum + p[:, h * tq:(h + 1) * tq]
    imp = _dot(ov_ref[...], psum, precision=lax.Precision.HIGHEST)[0:n_blk]
    blk = lax.broadcasted_iota(jnp.int32, (n_blk, tq), 0).astype(F32)
    cur = ((qi * tq + lax.broadcasted_iota(jnp.int32, (n_blk, tq), 1)) // SEL_BLOCK).astype(F32)
    forced = (blk == 0.0) | (blk == cur) | (blk == cur - 1.0)
    score = jnp.where(forced, 1e6, jnp.where(blk <= cur, imp, -1e6))
    bias = jnp.full((n_blk, tq), NEG_INF, F32)
    for _ in range(min(N_SELECT, n_blk)):
        best = jnp.max(score, axis=0, keepdims=True)
        idx = jnp.min(jnp.where(score == best, blk, float(LANES)), axis=0, keepdims=True)
        hit = blk == idx
        bias = jnp.where(hit, 0.0, bias)
        score = jnp.where(hit, LOWEST, score)
    bias = jnp.concatenate([bias, jnp.zeros((LANES - n_blk, tq), F32)], axis=0)
    bias_ref[...] = bias.T.astype(BF16)


def _select(qn, kc, vc, gates, overlap, n_blk):
    B, S, _ = qn.shape
    G = N_KV_GROUPS
    tq = min(SELECT_ROWS, S)
    gw = HEADS_PER_GROUP * HEAD_DIM
    qspec = pl.BlockSpec((None, tq, gw), lambda b, g, i: (b, i, g))
    lane_spec = pl.BlockSpec((None, tq, LANES), lambda b, g, i: (b, i, g))
    cspec = pl.BlockSpec((None, None) + kc.shape[2:], lambda b, g, i: (b, g, 0, 0))
    return pl.pallas_call(
        functools.partial(_select_kernel, n_blk=n_blk),
        grid=(B, G, S // tq),
        in_specs=[qspec, cspec, cspec, lane_spec, _full(overlap.shape)],
        out_specs=[qspec, lane_spec],
        out_shape=[jax.ShapeDtypeStruct(qn.shape, BF16), jax.ShapeDtypeStruct((B, S, G * LANES), BF16)],
        compiler_params=_params(("arbitrary", "arbitrary", "arbitrary")),
        name="select",
    )(qn, kc, vc, gates, overlap)


def _nsa_kernel(qr_ref, bias_ref, ocmp_ref, ks_ref, vs_ref, kw_ref, vw_ref, gt_ref, ex_ref, o_ref, vst_sc, vwt_sc):
    qi = pl.program_id(2)
    tq = qr_ref.shape[0]
    hpg = HEADS_PER_GROUP
    seq = ks_ref.shape[0]

    @pl.when(qi == 0)
    def _():
        for kt in range(seq // KEY_TILE):
            rows = slice(kt * KEY_TILE, (kt + 1) * KEY_TILE)
            vst_sc[0:HEAD_DIM, rows] = vs_ref[rows, :].astype(F32).T.astype(BF16)
            vwt_sc[0:HEAD_DIM, rows] = vw_ref[rows, :].astype(F32).T.astype(BF16)
        ones = jnp.ones((vst_sc.shape[0] - HEAD_DIM, seq), BF16)
        vst_sc[HEAD_DIM:, :] = ones
        vwt_sc[HEAD_DIM:, :] = ones

    def pair(v):
        return jnp.concatenate([v] * HEAD_PAIR, axis=1)

    qr = jnp.concatenate([qr_ref[:, h * HEAD_DIM:(h + 1) * HEAD_DIM] for h in range(hpg)], axis=0)
    q_aug = jnp.concatenate([qr, jnp.concatenate([bias_ref[...]] * hpg, axis=0)], axis=1)
    pair_cols = [slice(hp * HEAD_PAIR * tq, (hp + 1) * HEAD_PAIR * tq) for hp in range(hpg // HEAD_PAIR)]

    span = WINDOW + tq
    w0 = pl.multiple_of(jnp.maximum(qi * tq - WINDOW, 0), KEY_TILE)
    kpos = w0 + lax.broadcasted_iota(jnp.int32, (span, tq), 0)
    t_w = qi * tq + lax.broadcasted_iota(jnp.int32, (span, tq), 1)
    window_bias = pair(jnp.where((kpos <= t_w) & (kpos > t_w - WINDOW), 0.0, NEG_INF))
    k_win = kw_ref[pl.ds(w0, span), :]
    win_scores = [_dot_nt(k_win, qr[cols]) + window_bias for cols in pair_cols]

    def softmax_pv(sc, vt):
        pr = jnp.exp2((sc - jnp.max(sc, axis=0, keepdims=True)).astype(BF16))
        o = _dot(vt, pr)
        return o[0:HEAD_DIM] * (1.0 / o[HEAD_DIM:HEAD_DIM + 1])

    n_full = (qi * tq) // SLC_TILE
    vt_win = vwt_sc[:, pl.ds(w0, span)]
    gt = gt_ref[...].T
    for v in range(seq // SLC_TILE):
        @pl.when(n_full == v)
        def _():
            ext = (v + 1) * SLC_TILE
            k_aug = jnp.concatenate([ks_ref[0:ext, :], ex_ref[0:ext, :]], axis=1)
            kpos = v * SLC_TILE + lax.broadcasted_iota(jnp.int32, (SLC_TILE, tq), 0)
            t_s = qi * tq + lax.broadcasted_iota(jnp.int32, (SLC_TILE, tq), 1)
            causal = pair(jnp.where(kpos <= t_s, 0.0, NEG_INF))
            scores = []
            for cols in pair_cols:
                sc = _dot_nt(k_aug, q_aug[cols])
                last = sc[ext - SLC_TILE:ext] + causal
                scores.append(last if v == 0 else jnp.concatenate([sc[0:ext - SLC_TILE], last], axis=0))
            o_wins = [softmax_pv(sc, vt_win) for sc in win_scores]
            o_slcs = [softmax_pv(sc, vst_sc[:, 0:ext]) for sc in scores]
            for h in range(hpg):
                hp, j = divmod(h, HEAD_PAIR)
                sub = slice(j * tq, (j + 1) * tq)
                c0 = h * N_NSA_BRANCHES
                o = gt[c0 + 1:c0 + 2, :] * o_slcs[hp][:, sub] + gt[c0 + 2:c0 + 3, :] * o_wins[hp][:, sub]
                hd = slice(h * HEAD_DIM, (h + 1) * HEAD_DIM)
                o_ref[:, hd] = (o.T + ocmp_ref[:, hd].astype(F32)).astype(BF16)


def _nsa(qr, bias, ocmp, kvr, gates, onehot):
    B, S, _ = qr.shape
    G = N_KV_GROUPS
    tq = Q_TILE
    gw = HEADS_PER_GROUP * HEAD_DIM
    qspec = pl.BlockSpec((None, tq, gw), lambda b, g, i: (b, i, g))
    lane_spec = pl.BlockSpec((None, tq, LANES), lambda b, g, i: (b, i, g))
    kv = lambda part: pl.BlockSpec((None, S, HEAD_DIM), lambda b, g, i: (b, 0, part * G + g))
    return pl.pallas_call(
        _nsa_kernel,
        grid=(B, G, S // tq),
        in_specs=[qspec, lane_spec, qspec, kv(0), kv(1), kv(2), kv(3), lane_spec, _full(onehot.shape)],
        out_specs=qspec,
        out_shape=jax.ShapeDtypeStruct(qr.shape, BF16),
        scratch_shapes=[pltpu.VMEM((HEAD_DIM + ONES_ROWS, S), BF16), pltpu.VMEM((HEAD_DIM + ONES_ROWS, S), BF16)],
        compiler_params=_params(("arbitrary", "arbitrary", "arbitrary")),
        name="nsa",
    )(qr, bias, ocmp, kvr, kvr, kvr, kvr, gates, onehot)


def _lru_kernel(lx_ref, lg_ref, cw_ref, cb_ref, wa_ref, ba_ref, wx_ref, bx_ref, lam_ref, o_ref,
                xs_sc, a_sc, u_sc, h_sc, carry_sc):
    ts = lx_ref.shape[0]
    width = lx_ref.shape[1]
    bw = width // LRU_BLOCKS

    @pl.when(pl.program_id(1) == 0)
    def _():
        xs_sc[0:SUBLANES, :] = jnp.zeros((SUBLANES, width), F32)
        carry_sc[...] = jnp.zeros(carry_sc.shape, F32)

    xs_sc[SUBLANES:SUBLANES + ts, :] = lx_ref[...].astype(F32)
    xc = cb_ref[...] + cw_ref[CONV_WIDTH - 1:CONV_WIDTH, :] * xs_sc[SUBLANES:SUBLANES + ts, :]
    for d in range(1, CONV_WIDTH):
        w = cw_ref[CONV_WIDTH - 1 - d:CONV_WIDTH - d, :]
        xc = xc + w * xs_sc[SUBLANES - d:SUBLANES - d + ts, :]
    xs_sc[0:SUBLANES, :] = xs_sc[ts:ts + SUBLANES, :]

    lam = -lam_ref[...]
    neg_c_softplus = -LRU_C * (jnp.maximum(lam, 0.0) + jnp.log1p(jnp.exp(-jnp.abs(lam))))
    xcb = xc.astype(BF16)
    for blk in range(LRU_BLOCKS):
        sl = slice(blk * bw, (blk + 1) * bw)
        r = _sigmoid(_dot(xcb[:, sl], wa_ref[blk]) + ba_ref[:, sl])
        i = _sigmoid(_dot(xcb[:, sl], wx_ref[blk]) + bx_ref[:, sl])
        log_a = r * neg_c_softplus[:, sl]
        a = jnp.exp(log_a)
        a_sc[:, sl] = a
        u_sc[:, sl] = jnp.sqrt(-jnp.tanh(log_a) * (a * a + 1.0)) * (i * xc[:, sl])

    row = lax.broadcasted_iota(jnp.int32, (SUBLANES, width), 0)

    def chunk(c, h_prev):
        r0 = pl.multiple_of(c * SUBLANES, SUBLANES)
        a = a_sc[pl.ds(r0, SUBLANES), :]
        u = u_sc[pl.ds(r0, SUBLANES), :]
        for d in (1, 2, 4):
            keep = row >= d
            u = jnp.where(keep, a * pltpu.roll(u, d, 0) + u, u)
            a = jnp.where(keep, a * pltpu.roll(a, d, 0), a)
        h = a * h_prev + u
        h_sc[pl.ds(r0, SUBLANES), :] = h
        return jnp.broadcast_to(h[SUBLANES - 1:SUBLANES, :], h.shape)

    carry_sc[...] = lax.fori_loop(0, ts // SUBLANES, chunk, carry_sc[...])
    o_ref[...] = (_gelu_tanh(lg_ref[...].astype(F32)) * h_sc[...]).astype(BF16)


def _lru(lx, lg, conv_w, conv_b, wa, ba, wx, bx, lam):
    B, S, W = lx.shape
    ts = min(LRU_ROWS, S)
    row = pl.BlockSpec((None, ts, W), lambda b, s: (b, s, 0))
    consts = (conv_w, conv_b.reshape(1, W), wa, ba.reshape(1, W), wx, bx.reshape(1, W), lam.reshape(1, W))
    return pl.pallas_call(
        _lru_kernel,
        grid=(B, S // ts),
        in_specs=[row, row] + [_full(a.shape) for a in consts],
        out_specs=row,
        out_shape=jax.ShapeDtypeStruct((B, S, W), BF16),
        scratch_shapes=[pltpu.VMEM((ts + SUBLANES, W), F32), pltpu.VMEM((ts, W), F32), pltpu.VMEM((ts, W), F32),
                        pltpu.VMEM((ts, W), F32), pltpu.VMEM((SUBLANES, W), F32)],
        compiler_params=_params(("arbitrary", "arbitrary")),
        name="lru",
    )(lx, lg, *consts)


def _merge_kernel(on_ref, ol_ref, mg_ref, x_ref, mod_ref, g2_ref, wn_ref, wl_ref, wo_ref, rw_ref, rb_ref, tri_ref,
                  x1_ref, h2_ref, rt_ref, cnt_ref, carry_sc):
    first = (pl.program_id(0) == 0) & (pl.program_id(1) == 0)

    @pl.when(first)
    def _():
        carry_sc[...] = jnp.zeros(carry_sc.shape, F32)

    d = x_ref.shape[1]
    y_nsa = _dot(on_ref[...], wn_ref[...])
    y_lru = _dot(ol_ref[...], wl_ref[...])
    merged = mg_ref[:, 0:d].astype(F32) * y_nsa + mg_ref[:, d:2 * d].astype(F32) * y_lru
    gate1 = mod_ref[2:3, :]
    shift2 = mod_ref[3:4, :]
    scale2 = mod_ref[4:5, :]
    x1 = x_ref[...] + gate1 * _dot(merged.astype(BF16), wo_ref[...])
    x1_ref[...] = x1
    h2 = _rms(x1, g2_ref[...]) * (1.0 + scale2) + shift2
    h2_ref[...] = _pack_bf16_pairs(h2)

    rw = rw_ref[...]
    rw_hi = rw.astype(BF16)
    rw_lo = (rw - rw_hi.astype(F32)).astype(BF16)
    h2_hi = h2.astype(BF16)
    h2_lo = (h2 - h2_hi.astype(F32)).astype(BF16)
    logits = _dot_nt(rw_hi, h2_hi) + (_dot_nt(rw_hi, h2_lo) + _dot_nt(rw_lo, h2_hi))
    logits = logits[0:N_EXPERTS] + rb_ref[...]
    tm = logits.shape[1]
    eid = lax.broadcasted_iota(jnp.int32, logits.shape, 0).astype(F32)
    score = logits
    picks = []
    onehot = jnp.zeros(logits.shape, F32)
    for _ in range(TOP_K):
        best = jnp.max(score, axis=0, keepdims=True)
        idx = jnp.min(jnp.where(score == best, eid, float(LANES)), axis=0, keepdims=True)
        hit = eid == idx
        picks.append((idx, best, hit))
        onehot = jnp.where(hit, 1.0, onehot)
        score = jnp.where(hit, LOWEST, score)
    ew = [jnp.exp(v - picks[0][1]) for _, v, _ in picks]
    den = ew[0]
    for v in ew[1:]:
        den = den + v
    inv_den = 1.0 / den

    before = _dot(onehot.astype(BF16), tri_ref[...]) + carry_sc[:, 0:1]
    carry_sc[...] = carry_sc[...] + jnp.sum(onehot, axis=1, keepdims=True)
    cnt_ref[...] = carry_sc[...]

    rows = [idx for idx, _, _ in picks]
    rows += [jnp.sum(jnp.where(hit, before, 0.0), axis=0, keepdims=True) for _, _, hit in picks]
    rows += [e * inv_den for e in ew]
    rows.append(jnp.zeros((LANES - len(rows), tm), F32))
    rt_ref[...] = jnp.concatenate(rows, axis=0).T


def _merge(o_nsa, o_lru, mg, x, mod3, norm2_g, wn, wl, wo, rw, rb, tri):
    B, S, D = x.shape
    tm = tri.shape[0]
    row = lambda w: pl.BlockSpec((None, tm, w), lambda b, s: (b, s, 0))
    consts = (norm2_g.reshape(1, D), wn, wl, wo, rw, rb, tri)
    return pl.pallas_call(
        _merge_kernel,
        grid=(B, S // tm),
        in_specs=[row(D), row(D), row(2 * D), row(D), pl.BlockSpec((None, 6, D), lambda b, s: (b, 0, 0))]
                 + [_full(a.shape) for a in consts],
        out_specs=[row(D), row(D // 2), row(LANES), _full((N_EXPERTS, LANES))],
        out_shape=[jax.ShapeDtypeStruct((B, S, D), F32), jax.ShapeDtypeStruct((B, S, D // 2), jnp.int32),
                   jax.ShapeDtypeStruct((B, S, LANES), F32), jax.ShapeDtypeStruct((N_EXPERTS, LANES), F32)],
        scratch_shapes=[pltpu.VMEM((N_EXPERTS, LANES), F32)],
        compiler_params=_params(("arbitrary", "arbitrary")),
        name="merge",
    )(o_nsa, o_lru, mg, x, mod3, *consts)


def _expert_kernel(be_ref, nb_ref, x_ref, w1_ref, b1_ref, w2_ref, b2_ref, pick_ref, y_ref, w1_sc, w2_sc):
    i = pl.program_id(0)
    ff = w2_ref.shape[0]
    chunk = pick_ref.shape[0]

    @pl.when((i == 0) | (be_ref[i] != be_ref[jnp.maximum(i - 1, 0)]))
    def _():
        for c in range(2 * ff // chunk):
            r = _dot(w1_ref[:, c * chunk:(c + 1) * chunk].astype(BF16), pick_ref[...])
            half = chunk // 2
            w1_sc[:, c * half:(c + 1) * half] = r[:, 0:half].astype(BF16)
            w1_sc[:, ff + c * half:ff + (c + 1) * half] = r[:, half:chunk].astype(BF16)
        w2_sc[...] = w2_ref[...].astype(BF16)

    @pl.when(i < nb_ref[0])
    def _():
        h = _dot(_unpack_bf16_pairs(x_ref[...]).astype(BF16), w1_sc[...]) + b1_ref[...]
        x_glu = jnp.minimum(h[:, 0:ff], SWIGLU_LIMIT)
        x_lin = jnp.clip(h[:, ff:2 * ff], -SWIGLU_LIMIT, SWIGLU_LIMIT)
        act = x_glu * _sigmoid(SWIGLU_ALPHA * x_glu) * (x_lin + 1.0)
        y_ref[...] = _pack_bf16_pairs(_dot(act.astype(BF16), w2_sc[...]) + b2_ref[...])

    @pl.when(i >= nb_ref[0])
    def _():
        y_ref[...] = jnp.zeros(y_ref.shape, jnp.int32)


def _experts(blk_e, n_used, buf, w1, b1, w2, b2):
    n_rows = buf.shape[0]
    F, D = w2.shape[1], w2.shape[2]
    tm = EXPERT_ROWS
    chunk = 2 * LANES
    pick = np.zeros((chunk, chunk), np.float32)
    pick[2 * np.arange(LANES), np.arange(LANES)] = 1.0
    pick[2 * np.arange(LANES) + 1, LANES + np.arange(LANES)] = 1.0
    wspec = lambda a, b: pl.BlockSpec((None, a, b), lambda i, be, nb: (be[i], 0, 0))
    return pl.pallas_call(
        _expert_kernel,
        grid_spec=pltpu.PrefetchScalarGridSpec(
            num_scalar_prefetch=2,
            grid=(n_rows // tm,),
            in_specs=[pl.BlockSpec((tm, D // 2), lambda i, be, nb: (i, 0)), wspec(D, 2 * F), wspec(1, 2 * F), wspec(F, D),
                      wspec(1, D), pl.BlockSpec((chunk, chunk), lambda i, be, nb: (0, 0))],
            out_specs=pl.BlockSpec((tm, D // 2), lambda i, be, nb: (i, 0)),
            scratch_shapes=[pltpu.VMEM((D, 2 * F), BF16), pltpu.VMEM((F, D), BF16)],
        ),
        out_shape=jax.ShapeDtypeStruct((n_rows, D // 2), jnp.int32),
        compiler_params=_params(("arbitrary",)),
        name="experts",
    )(blk_e, n_used, buf, w1, b1, w2, b2, jnp.asarray(pick, dtype=BF16))


def _sc_mesh():
    return plsc.VectorSubcoreMesh(core_axis_name="c", subcore_axis_name="s")


def _sc_worker():
    return lax.axis_index("s") * SC_CORES + lax.axis_index("c")


def _sc_scatter_rows(rows, dest, n_rows):
    T, W = rows.shape
    K = dest.shape[0]
    n_workers = SC_CORES * SC_SUBCORES
    per_w = T // n_workers
    ch = SC_CHUNK
    n_ch = per_w // ch
    assert per_w * n_workers == T and n_ch * ch == per_w and n_ch % 2 == 0
    dest4 = dest.reshape(K, n_workers, n_ch, ch).transpose(1, 2, 0, 3)

    @functools.partial(
        pl.kernel, mesh=_sc_mesh(), out_type=jax.ShapeDtypeStruct((n_rows, W), rows.dtype),
        scratch_types=[pltpu.VMEM((n_ch, K, ch), jnp.int32), pltpu.VMEM((2, ch, W), rows.dtype),
                       pltpu.SemaphoreType.DMA((2,)), pltpu.SemaphoreType.DMA((2,))])
    def scatter(rows_hbm, dest_hbm, out_hbm, idx_v, rows_v, read_sem, write_sem):
        wid = _sc_worker()
        base = wid * per_w
        pltpu.sync_copy(dest_hbm.at[wid], idx_v)

        def read(c, b):
            src = rows_hbm.at[pl.ds(pl.multiple_of(base + c * ch, ch), ch)]
            return pltpu.make_async_copy(src, rows_v.at[b], read_sem.at[b])

        def write(c, b, k):
            return pltpu.make_async_copy(rows_v.at[b], out_hbm.at[idx_v.at[c, k]], write_sem.at[b])

        read(0, 0).start()

        @pl.loop(0, n_ch, step=2)
        def _(i):
            for b in range(2):
                c = i + b
                read(c, b).wait()

                @pl.when(c >= 1)
                def _():
                    for k in range(K):
                        write(c - 1, 1 - b, k).wait()

                @pl.when(c + 1 < n_ch)
                def _():
                    read(c + 1, 1 - b).start()

                for k in range(K):
                    write(c, b, k).start()

        for k in range(K):
            write(n_ch - 1, 1, k).wait()

    return scatter(rows, dest4)


def _sc_gather_rows(table, idx):
    N = idx.shape[0]
    W = table.shape[1]
    n_workers = SC_CORES * SC_SUBCORES
    per_w = N // n_workers
    ch = SC_CHUNK
    n_ch = per_w // ch
    assert per_w * n_workers == N and n_ch * ch == per_w and n_ch % 2 == 0
    idx3 = idx.reshape(n_workers, n_ch, ch)

    @functools.partial(
        pl.kernel, mesh=_sc_mesh(), out_type=jax.ShapeDtypeStruct((N, W), table.dtype),
        scratch_types=[pltpu.VMEM((n_ch, ch), jnp.int32), pltpu.VMEM((2, ch, W), table.dtype),
                       pltpu.SemaphoreType.DMA((2,)), pltpu.SemaphoreType.DMA((2,))])
    def gather(table_hbm, idx_hbm, out_hbm, idx_v, rows_v, gather_sem, write_sem):
        wid = _sc_worker()
        base = wid * per_w
        pltpu.sync_copy(idx_hbm.at[wid], idx_v)

        def fetch(c, b):
            return pltpu.make_async_copy(table_hbm.at[idx_v.at[c]], rows_v.at[b], gather_sem.at[b])

        def write(c, b):
            dst = out_hbm.at[pl.ds(pl.multiple_of(base + c * ch, ch), ch)]
            return pltpu.make_async_copy(rows_v.at[b], dst, write_sem.at[b])

        fetch(0, 0).start()

        @pl.loop(0, n_ch, step=2)
        def _(i):
            for b in range(2):
                c = i + b
                fetch(c, b).wait()

                @pl.when(c >= 1)
                def _():
                    write(c - 1, 1 - b).wait()

                @pl.when(c + 1 < n_ch)
                def _():
                    fetch(c + 1, 1 - b).start()

                write(c, b).start()

        write(n_ch - 1, 1).wait()

    return gather(table, idx3)


def _combine_kernel(x1_ref, yg_ref, rt_ref, mod_ref, *rest):
    o_ref = rest[-1]
    gate2 = mod_ref[5:6, :]
    rt = rt_ref[...]
    acc = rt[:, 2 * TOP_K:2 * TOP_K + 1] * _unpack_bf16_pairs(yg_ref[0])
    for k in range(1, TOP_K):
        acc = acc + rt[:, 2 * TOP_K + k:2 * TOP_K + k + 1] * _unpack_bf16_pairs(yg_ref[k])
    o_ref[...] = x1_ref[...] + gate2 * acc


def _combine(x1, yg, rt, mod3, prev, b0):
    B, S, D = x1.shape
    nb = yg.shape[1]
    tm = min(COMBINE_ROWS, S)
    row = lambda w: pl.BlockSpec((None, tm, w), lambda b, s: (b0 + b, s, 0))
    in_specs = [row(D), pl.BlockSpec((TOP_K, None, tm, D // 2), lambda b, s: (0, b, s, 0)), row(LANES),
                pl.BlockSpec((None, 6, D), lambda b, s: (b0 + b, 0, 0))]
    operands = [x1, yg, rt, mod3]
    if prev is not None:
        in_specs.append(pl.BlockSpec(memory_space=pl.ANY))
        operands.append(prev)
    return pl.pallas_call(
        _combine_kernel,
        grid=(nb, S // tm),
        in_specs=in_specs,
        out_specs=row(D),
        out_shape=jax.ShapeDtypeStruct((B, S, D), F32),
        input_output_aliases={} if prev is None else {4: 0},
        compiler_params=_params(("arbitrary", "arbitrary")),
        name="combine",
    )(*operands)


def _overlap_matrix(n_cmp_pad, n_blk):
    cs = np.arange(n_cmp_pad)[:, None] * CMP_STRIDE
    js = np.arange(LANES)[None, :] * SEL_BLOCK
    m = (cs <= js + SEL_BLOCK - 1) & (cs + CMP_BLOCK - 1 >= js) & (np.arange(LANES)[None, :] < n_blk)
    return jnp.asarray(m.astype(np.float32).T)


def _block_onehot(seq):
    key = np.arange(seq)[:, None]
    j = np.arange(LANES)[None, :]
    return jnp.asarray((j == key // SEL_BLOCK).astype(np.float32), dtype=BF16)


def _layer(x, c, ang, ada_w, ada_b, norm1_g, w_in, q_norm_g, k_norm_g, cmp_pe_k, cmp_pe_v, cmp_wk1, cmp_wk2, cmp_wv1,
           cmp_wv2, conv_w, conv_b, lru_wa, lru_ba, lru_wx, lru_bx, lru_lambda, w_branch_out, w_out, norm2_g,
           router_w, router_b, moe_w1, moe_b1, moe_w2, moe_b2):
    B, S, D = x.shape
    T = B * S
    G = N_KV_GROUPS
    nsa_w = N_HEADS * HEAD_DIM
    kv_w = G * HEAD_DIM

    mod3 = _ada(c, ada_w, ada_b).reshape(B, 6, D)

    o = 0
    wq = w_in[:, o:o + nsa_w]; o += nsa_w
    wkc = w_in[:, o:o + 2 * kv_w]; o += 2 * kv_w
    wkr = w_in[:, o:o + 4 * kv_w]; o += 4 * kv_w
    n_gate = N_HEADS * N_NSA_BRANCHES
    wg_raw = w_in[:, o:o + n_gate].reshape(D, G, n_gate // G); o += n_gate
    wg = jnp.pad(wg_raw, ((0, 0), (0, 0), (0, LANES - n_gate // G))).reshape(D, G * LANES)
    wlx = w_in[:, o:o + D]; o += D
    wlg = w_in[:, o:o + D]; o += D
    wm = w_in[:, o:o + 2 * D]
    bf = lambda a: a.astype(BF16)

    qn, qr, kvc, kvr, gates, lx, lg, mg = _in_proj(x, mod3, norm1_g, ang, q_norm_g, k_norm_g, bf(wq), bf(wkc), bf(wkr),
                                                   bf(wg), bf(wlx), bf(wlg), bf(wm))

    nc = S // CMP_STRIDE
    pe = lambda p: jnp.broadcast_to(p.reshape(1, CMP_BLOCK * HEAD_DIM), (SUBLANES, CMP_BLOCK * HEAD_DIM)).astype(BF16)
    kc, vc = _compress(kvc, pe(cmp_pe_k), pe(cmp_pe_v), bf(cmp_wk1), bf(cmp_wk2), bf(cmp_wv1), bf(cmp_wv2), k_norm_g)

    ocmp, bias = _select(qn, kc, vc, gates, _overlap_matrix(nc, S // SEL_BLOCK), S // SEL_BLOCK)
    o_nsa = _nsa(qr, bias, ocmp, kvr, gates, _block_onehot(S))
    o_lru = _lru(lx, lg, conv_w, conv_b, bf(lru_wa), lru_ba, bf(lru_wx), lru_bx, lru_lambda)

    tm = min(IN_ROWS, S)
    tri = jnp.asarray(np.triu(np.ones((tm, tm), np.float32), 1), dtype=BF16)
    rw = jnp.pad(router_w.T, ((0, LANES - N_EXPERTS), (0, 0)))
    rb = router_b.reshape(N_EXPERTS, 1)
    x1, h2, rt, cnt = _merge(o_nsa, o_lru, mg, x, mod3, norm2_g, bf(w_branch_out[:nsa_w]), bf(w_branch_out[nsa_w:]),
                             bf(w_out), rw, rb, tri)

    rt2 = rt.reshape(T, LANES)
    top_e = rt2[:, 0:TOP_K].astype(jnp.int32)
    rank = rt2[:, TOP_K:2 * TOP_K].astype(jnp.int32)
    counts = cnt[:, 0].astype(jnp.int32)
    padded = (counts + EXPERT_ROWS - 1) // EXPERT_ROWS * EXPERT_ROWS
    ends = jnp.cumsum(padded)
    start = ends - padded
    dest = start[top_e] + rank
    n_rows = -(-(T * TOP_K + N_EXPERTS * (EXPERT_ROWS - 1)) // EXPERT_ROWS) * EXPERT_ROWS
    n_blocks = n_rows // EXPERT_ROWS
    blk_first = jnp.arange(n_blocks, dtype=jnp.int32) * EXPERT_ROWS
    blk_e = jnp.minimum(jnp.sum(ends[None, :] <= blk_first[:, None], axis=1), N_EXPERTS - 1).astype(jnp.int32)
    n_used = (ends[-1:] // EXPERT_ROWS).astype(jnp.int32)

    dest_kt = dest.T
    buf = _sc_scatter_rows(h2.reshape(T, D // 2), dest_kt, n_rows)

    b1 = jnp.concatenate([moe_b1[:, 0::2], moe_b1[:, 1::2]], axis=1)
    ybuf = _experts(blk_e, n_used, buf, moe_w1, b1.reshape(N_EXPERTS, 1, -1), moe_w2, moe_b2.reshape(N_EXPERTS, 1, D))
    nb = B // COMBINE_PARTS if B % COMBINE_PARTS == 0 else B
    dest_parts = dest_kt.reshape(TOP_K, B // nb, nb * S)
    out = None
    for p in range(B // nb):
        yg = _sc_gather_rows(ybuf, dest_parts[:, p].reshape(-1)).reshape(TOP_K, nb, S, D // 2)
        out = _combine(x1, yg, rt, mod3, out, p * nb)
    return out


def kernel(x, c, positions, ada_w, ada_b, norm1_g, w_in, q_norm_g, k_norm_g, cmp_pe_k, cmp_pe_v, cmp_wk1, cmp_wk2, cmp_wv1, cmp_wv2, conv_w, conv_b, lru_wa, lru_ba, lru_wx, lru_bx, lru_lambda, w_branch_out, w_out, norm2_g, router_w, router_b, moe_w1, moe_b1, moe_w2, moe_b2):
    inv = ROPE_THETA ** (-jnp.arange(0, HEAD_DIM, 2, dtype=F32) / HEAD_DIM)
    ang = positions.astype(F32)[..., None] * inv
    ang = jnp.concatenate([ang, ang], axis=-1)
    for l in range(ada_w.shape[0]):
        x = _layer(x, c, ang, ada_w[l], ada_b[l], norm1_g[l], w_in[l], q_norm_g[l], k_norm_g[l], cmp_pe_k[l],
                   cmp_pe_v[l], cmp_wk1[l], cmp_wk2[l], cmp_wv1[l], cmp_wv2[l], conv_w[l], conv_b[l], lru_wa[l],
                   lru_ba[l], lru_wx[l], lru_bx[l], lru_lambda[l], w_branch_out[l], w_out[l], norm2_g[l], router_w[l],
                   router_b[l], moe_w1[l], moe_b1[l], moe_w2[l], moe_b2[l])
    return x
```

```python
import functools
import math

import jax
import jax.numpy as jnp
import numpy as np
from jax import lax
from jax.experimental import pallas as pl
from jax.experimental.pallas import tpu as pltpu
from jax.experimental.pallas import tpu_sc as plsc

F32 = jnp.float32
BF16 = jnp.bfloat16

N_HEADS = 8
HEAD_DIM = 128
N_KV_GROUPS = 2
HEADS_PER_GROUP = N_HEADS // N_KV_GROUPS
N_NSA_BRANCHES = 3
CMP_BLOCK = 32
CMP_STRIDE = 16
CMP_HIDDEN = 256
SEL_BLOCK = 64
N_SELECT = 8
WINDOW = 512
ROPE_THETA = 10000.0
LRU_BLOCKS = 4
CONV_WIDTH = 4
LRU_C = 8.0
N_EXPERTS = 32
TOP_K = 4
SWIGLU_LIMIT = 7.0
SWIGLU_ALPHA = 1.702
RMS_EPS = 1e-6
NEG_INF = -1e30
LOWEST = -3.0e38
LOG2_E = 1.4426950408889634

LANES = 128
SUBLANES = 8
VMEM_LIMIT = 56 * 1024 * 1024
SC_CORES = 2
SC_SUBCORES = 16
SC_CHUNK = 64

Q_TILE = 256
SELECT_ROWS = 1024
KEY_TILE = 128
SLC_TILE = 256
ONES_ROWS = 16
HEAD_PAIR = 2
IN_ROWS = 512
LRU_ROWS = 512
EXPERT_ROWS = 512
COMBINE_ROWS = 256
COMBINE_PARTS = 4


def _sigmoid(v):
    return 0.5 * jnp.tanh(0.5 * v) + 0.5


def _gelu_tanh(v):
    c = math.sqrt(2.0 / math.pi)
    half = 0.5 * v
    return half + half * jnp.tanh(v * (c + (c * 0.044715) * (v * v)))


def _rms(v, g):
    return v * lax.rsqrt(jnp.mean(v * v, axis=-1, keepdims=True) + RMS_EPS) * g


def _dot(a, b, **kw):
    return jnp.dot(a, b, preferred_element_type=F32, **kw)


def _dot_nt(a, b, **kw):
    return lax.dot_general(a, b, (((1,), (1,)), ((), ())), preferred_element_type=F32, **kw)


def _pack_bf16_pairs(v):
    n = v.shape[1] // 2
    lo = lax.bitcast_convert_type(v[:, 0:n].astype(BF16).astype(F32), jnp.int32)
    hi = lax.bitcast_convert_type(v[:, n:2 * n].astype(BF16).astype(F32), jnp.int32)
    return lax.shift_right_logical(lo, 16) | hi


def _unpack_bf16_pairs(w):
    lo = lax.bitcast_convert_type(lax.shift_left(w, 16), F32)
    hi = lax.bitcast_convert_type(w & jnp.int32(-65536), F32)
    return jnp.concatenate([lo, hi], axis=1)


def _full(shape):
    nd = len(shape)
    return pl.BlockSpec(shape, lambda *_: (0,) * nd)


def _params(sem):
    return pltpu.CompilerParams(dimension_semantics=sem, vmem_limit_bytes=VMEM_LIMIT)


def _ada_kernel(c_ref, w_ref, b_ref, o_ref):
    c = c_ref[...]
    o_ref[...] = _dot(c * _sigmoid(c), w_ref[...], precision=lax.Precision.HIGHEST) + b_ref[...]


def _ada(c, ada_w, ada_b):
    B, D = c.shape
    N = ada_w.shape[1]
    return pl.pallas_call(
        _ada_kernel,
        grid=(N // D,),
        in_specs=[_full((B, D)), pl.BlockSpec((D, D), lambda j: (0, j)), pl.BlockSpec((1, D), lambda j: (0, j))],
        out_specs=pl.BlockSpec((B, D), lambda j: (0, j)),
        out_shape=jax.ShapeDtypeStruct((B, N), F32),
        compiler_params=_params(("arbitrary",)),
        name="ada",
    )(c, ada_w, ada_b.reshape(1, N))


def _in_kernel(x_ref, mod_ref, g1_ref, ang_ref, qg_ref, kg_ref, wq_ref, wkc_ref, wkr_ref, wg_ref, wlx_ref, wlg_ref,
               wm_ref, qn_ref, qr_ref, kvc_ref, kvr_ref, gt_ref, lx_ref, lg_ref, mg_ref, kvc_sc):
    x = x_ref[...]
    shift1 = mod_ref[0:1, :]
    scale1 = mod_ref[1:2, :]
    h = _rms(x, g1_ref[...]) * (1.0 + scale1) + shift1
    hb = h.astype(BF16)

    ang = ang_ref[...]
    cos = jnp.cos(ang)
    sin = jnp.sin(ang)
    lane = lax.broadcasted_iota(jnp.int32, ang.shape, 1)
    sin_signed = jnp.where(lane < HEAD_DIM // 2, -sin, sin)

    def rope(v):
        return v * cos + pltpu.roll(v, HEAD_DIM // 2, 1) * sin_signed

    q = _dot(hb, wq_ref[...])
    for hh in range(N_HEADS):
        sl = slice(hh * HEAD_DIM, (hh + 1) * HEAD_DIM)
        qh = _rms(q[:, sl], qg_ref[...])
        qn_ref[:, sl] = (qh * HEAD_DIM ** -0.5).astype(BF16)
        qr_ref[:, sl] = (rope(qh) * (HEAD_DIM ** -0.5 * LOG2_E)).astype(BF16)

    kvc = _dot(hb, wkc_ref[...])
    for part in range(kvc_ref.shape[0]):
        kvc_sc[part] = kvc[:, part * HEAD_DIM:(part + 1) * HEAD_DIM]
        for tok in range(CMP_STRIDE):
            piece = kvc_sc[part, pl.ds(tok, kvc_ref.shape[1], stride=CMP_STRIDE), :]
            kvc_ref[part, :, tok * HEAD_DIM:(tok + 1) * HEAD_DIM] = piece.astype(BF16)

    kvr = _dot(hb, wkr_ref[...])
    kvw = N_KV_GROUPS * HEAD_DIM
    for part in range(4):
        for gg in range(N_KV_GROUPS):
            sl = slice(part * kvw + gg * HEAD_DIM, part * kvw + (gg + 1) * HEAD_DIM)
            v = kvr[:, sl]
            if part % 2 == 0:
                row = 1 + part // 2
                v = rope(_rms(v, kg_ref[row:row + 1, :]))
            kvr_ref[:, sl] = v.astype(BF16)

    gt_ref[...] = _sigmoid(_dot(hb, wg_ref[...]))
    lx_ref[...] = _dot(hb, wlx_ref[...]).astype(BF16)
    lg_ref[...] = _dot(hb, wlg_ref[...]).astype(BF16)
    mg_ref[...] = _sigmoid(_dot(hb, wm_ref[...])).astype(BF16)


def _in_proj(x, mod3, norm1_g, ang, q_norm_g, k_norm_g, wq, wkc, wkr, wg, wlx, wlg, wm):
    B, S, D = x.shape
    tm = min(IN_ROWS, S)
    row = lambda w: pl.BlockSpec((None, tm, w), lambda b, s: (b, s, 0))
    widths = (wq.shape[1], wq.shape[1], wkc.shape[1], wkr.shape[1], wg.shape[1], wlx.shape[1], wlg.shape[1], wm.shape[1])
    dtypes = (BF16, BF16, BF16, BF16, F32, BF16, BF16, BF16)
    weights = (wq, wkc, wkr, wg, wlx, wlg, wm)
    out_specs = [row(w) for w in widths]
    out_shape = [jax.ShapeDtypeStruct((B, S, w), dt) for w, dt in zip(widths, dtypes)]
    n_part = wkc.shape[1] // HEAD_DIM
    out_specs[2] = pl.BlockSpec((None, n_part, tm // CMP_STRIDE, CMP_STRIDE * HEAD_DIM), lambda b, s: (b, 0, s, 0))
    out_shape[2] = jax.ShapeDtypeStruct((B, n_part, S // CMP_STRIDE, CMP_STRIDE * HEAD_DIM), BF16)
    return pl.pallas_call(
        _in_kernel,
        grid=(B, S // tm),
        in_specs=[row(D), pl.BlockSpec((None, 6, D), lambda b, s: (b, 0, 0)), _full((1, D)), row(HEAD_DIM),
                  _full((1, HEAD_DIM)), _full(k_norm_g.shape)] + [_full(w.shape) for w in weights],
        out_specs=out_specs,
        out_shape=out_shape,
        scratch_shapes=[pltpu.VMEM((n_part, tm, HEAD_DIM), F32)],
        compiler_params=_params(("arbitrary", "arbitrary")),
        name="in_proj",
    )(x, mod3, norm1_g.reshape(1, D), ang, q_norm_g.reshape(1, HEAD_DIM), k_norm_g, *weights)


def _cmp_kernel(ak_ref, av_ref, pek_ref, pev_ref, wk1_ref, wk2_ref, wv1_ref, wv2_ref, kg_ref, kc_ref, vc_ref):
    half = CMP_STRIDE * HEAD_DIM

    def compress(a_ref, pe_ref, w1_ref, w2_ref):
        a = a_ref[...]
        u = _dot(a, w1_ref[0:half, :])
        v = _dot(a, w1_ref[half:2 * half, :])
        pw = _dot(pe_ref[...], w1_ref[...])
        pre = u + pltpu.roll(v, v.shape[0] - 1, 0) + pw[0:1, :]
        return _dot(_gelu_tanh(pre).astype(BF16), w2_ref[...])

    kc_ref[...] = _rms(compress(ak_ref, pek_ref, wk1_ref, wk2_ref), kg_ref[0:1, :]).astype(BF16)
    vc_ref[...] = compress(av_ref, pev_ref, wv1_ref, wv2_ref).T.astype(BF16)


def _compress(a, pe_k, pe_v, wk1, wk2, wv1, wv2, k_norm_g):
    B, _, NC, W = a.shape
    G = N_KV_GROUPS
    out = pl.BlockSpec((None, None, NC, HEAD_DIM), lambda b, g: (b, g, 0, 0))
    consts = (pe_k, pe_v, wk1, wk2, wv1, wv2, k_norm_g)
    return pl.pallas_call(
        _cmp_kernel,
        grid=(B, G),
        in_specs=[pl.BlockSpec((None, None, NC, W), lambda b, g: (b, g, 0, 0)),
                  pl.BlockSpec((None, None, NC, W), lambda b, g: (b, G + g, 0, 0))] + [_full(c.shape) for c in consts],
        out_specs=[out, out],
        out_shape=[jax.ShapeDtypeStruct((B, G, NC, HEAD_DIM), BF16)] * 2,
        compiler_params=_params(("arbitrary", "arbitrary")),
        name="compress",
    )(a, a, *consts)


def _select_kernel(qn_ref, kc_ref, vc_ref, gt_ref, ov_ref, ocmp_ref, bias_ref, *, n_blk):
    qi = pl.program_id(2)
    tq = qn_ref.shape[0]
    hpg = HEADS_PER_GROUP
    qn = jnp.concatenate([qn_ref[:, h * HEAD_DIM:(h + 1) * HEAD_DIM] for h in range(hpg)], axis=0)
    row = lax.broadcasted_iota(jnp.int32, (kc_ref.shape[0], tq), 0)
    t_q = qi * tq + lax.broadcasted_iota(jnp.int32, (kc_ref.shape[0], tq), 1)

    cmask = jnp.concatenate([row * CMP_STRIDE + (CMP_BLOCK - 1) <= t_q] * hpg, axis=1)
    s = jnp.where(cmask, _dot_nt(kc_ref[...], qn), NEG_INF)
    e = jnp.exp(s - jnp.max(s, axis=0, keepdims=True))
    p = jnp.where(cmask, e * (1.0 / jnp.sum(e, axis=0, keepdims=True)), 0.0)
    o_cmp = _dot(vc_ref[...], p.astype(BF16))
    gt = gt_ref[...]
    for h in range(hpg):
        c0 = h * N_NSA_BRANCHES
        ocmp_ref[:, h * HEAD_DIM:(h + 1) * HEAD_DIM] = (gt[:, c0:c0 + 1] * o_cmp[:, h * tq:(h + 1) * tq].T).astype(BF16)

    psum = p[:, 0:tq]
    for h in range(1, hpg):
        psum = psum + p[:, h * tq:(h + 1) * tq]
    imp = _dot(ov_ref[...], psum, precision=lax.Precision.HIGHEST)[0:n_blk]
    blk = lax.broadcasted_iota(jnp.int32, (n_blk, tq), 0).astype(F32)
    cur = ((qi * tq + lax.broadcasted_iota(jnp.int32, (n_blk, tq), 1)) // SEL_BLOCK).astype(F32)
    forced = (blk == 0.0) | (blk == cur) | (blk == cur - 1.0)
    score = jnp.where(forced, 1e6, jnp.where(blk <= cur, imp, -1e6))
    bias = jnp.full((n_blk, tq), NEG_INF, F32)
    for _ in range(min(N_SELECT, n_blk)):
        best = jnp.max(score, axis=0, keepdims=True)
        idx = jnp.min(jnp.where(score == best, blk, float(LANES)), axis=0, keepdims=True)
        hit = blk == idx
        bias = jnp.where(hit, 0.0, bias)
        score = jnp.where(hit, LOWEST, score)
    bias = jnp.concatenate([bias, jnp.zeros((LANES - n_blk, tq), F32)], axis=0)
    bias_ref[...] = bias.T.astype(BF16)


def _select(qn, kc, vc, gates, overlap, n_blk):
    B, S, _ = qn.shape
    G = N_KV_GROUPS
    tq = min(SELECT_ROWS, S)
    gw = HEADS_PER_GROUP * HEAD_DIM
    qspec = pl.BlockSpec((None, tq, gw), lambda b, g, i: (b, i, g))
    lane_spec = pl.BlockSpec((None, tq, LANES), lambda b, g, i: (b, i, g))
    cspec = pl.BlockSpec((None, None) + kc.shape[2:], lambda b, g, i: (b, g, 0, 0))
    return pl.pallas_call(
        functools.partial(_select_kernel, n_blk=n_blk),
        grid=(B, G, S // tq),
        in_specs=[qspec, cspec, cspec, lane_spec, _full(overlap.shape)],
        out_specs=[qspec, lane_spec],
        out_shape=[jax.ShapeDtypeStruct(qn.shape, BF16), jax.ShapeDtypeStruct((B, S, G * LANES), BF16)],
        compiler_params=_params(("arbitrary", "arbitrary", "arbitrary")),
        name="select",
    )(qn, kc, vc, gates, overlap)


def _nsa_kernel(qr_ref, bias_ref, ocmp_ref, ks_ref, vs_ref, kw_ref, vw_ref, gt_ref, ex_ref, o_ref, vst_sc, vwt_sc):
    qi = pl.program_id(2)
    tq = qr_ref.shape[0]
    hpg = HEADS_PER_GROUP
    seq = ks_ref.shape[0]

    @pl.when(qi == 0)
    def _():
        for kt in range(seq // KEY_TILE):
            rows = slice(kt * KEY_TILE, (kt + 1) * KEY_TILE)
            vst_sc[0:HEAD_DIM, rows] = vs_ref[rows, :].astype(F32).T.astype(BF16)
            vwt_sc[0:HEAD_DIM, rows] = vw_ref[rows, :].astype(F32).T.astype(BF16)
        ones = jnp.ones((vst_sc.shape[0] - HEAD_DIM, seq), BF16)
        vst_sc[HEAD_DIM:, :] = ones
        vwt_sc[HEAD_DIM:, :] = ones

    def pair(v):
        return jnp.concatenate([v] * HEAD_PAIR, axis=1)

    qr = jnp.concatenate([qr_ref[:, h * HEAD_DIM:(h + 1) * HEAD_DIM] for h in range(hpg)], axis=0)
    q_aug = jnp.concatenate([qr, jnp.concatenate([bias_ref[...]] * hpg, axis=0)], axis=1)
    pair_cols = [slice(hp * HEAD_PAIR * tq, (hp + 1) * HEAD_PAIR * tq) for hp in range(hpg // HEAD_PAIR)]

    span = WINDOW + tq
    w0 = pl.multiple_of(jnp.maximum(qi * tq - WINDOW, 0), KEY_TILE)
    kpos = w0 + lax.broadcasted_iota(jnp.int32, (span, tq), 0)
    t_w = qi * tq + lax.broadcasted_iota(jnp.int32, (span, tq), 1)
    window_bias = pair(jnp.where((kpos <= t_w) & (kpos > t_w - WINDOW), 0.0, NEG_INF))
    k_win = kw_ref[pl.ds(w0, span), :]
    win_scores = [_dot_nt(k_win, qr[cols]) + window_bias for cols in pair_cols]

    def softmax_pv(sc, vt):
        pr = jnp.exp2((sc - jnp.max(sc, axis=0, keepdims=True)).astype(BF16))
        o = _dot(vt, pr)
        return o[0:HEAD_DIM] * (1.0 / o[HEAD_DIM:HEAD_DIM + 1])

    n_full = (qi * tq) // SLC_TILE
    vt_win = vwt_sc[:, pl.ds(w0, span)]
    gt = gt_ref[...].T
    for v in range(seq // SLC_TILE):
        @pl.when(n_full == v)
        def _():
            ext = (v + 1) * SLC_TILE
            k_aug = jnp.concatenate([ks_ref[0:ext, :], ex_ref[0:ext, :]], axis=1)
            kpos = v * SLC_TILE + lax.broadcasted_iota(jnp.int32, (SLC_TILE, tq), 0)
            t_s = qi * tq + lax.broadcasted_iota(jnp.int32, (SLC_TILE, tq), 1)
            causal = pair(jnp.where(kpos <= t_s, 0.0, NEG_INF))
            scores = []
            for cols in pair_cols:
                sc = _dot_nt(k_aug, q_aug[cols])
                last = sc[ext - SLC_TILE:ext] + causal
                scores.append(last if v == 0 else jnp.concatenate([sc[0:ext - SLC_TILE], last], axis=0))
            o_wins = [softmax_pv(sc, vt_win) for sc in win_scores]
            o_slcs = [softmax_pv(sc, vst_sc[:, 0:ext]) for sc in scores]
            for h in range(hpg):
                hp, j = divmod(h, HEAD_PAIR)
                sub = slice(j * tq, (j + 1) * tq)
                c0 = h * N_NSA_BRANCHES
                o = gt[c0 + 1:c0 + 2, :] * o_slcs[hp][:, sub] + gt[c0 + 2:c0 + 3, :] * o_wins[hp][:, sub]
                hd = slice(h * HEAD_DIM, (h + 1) * HEAD_DIM)
                o_ref[:, hd] = (o.T + ocmp_ref[:, hd].astype(F32)).astype(BF16)


def _nsa(qr, bias, ocmp, kvr, gates, onehot):
    B, S, _ = qr.shape
    G = N_KV_GROUPS
    tq = Q_TILE
    gw = HEADS_PER_GROUP * HEAD_DIM
    qspec = pl.BlockSpec((None, tq, gw), lambda b, g, i: (b, i, g))
    lane_spec = pl.BlockSpec((None, tq, LANES), lambda b, g, i: (b, i, g))
    kv = lambda part: pl.BlockSpec((None, S, HEAD_DIM), lambda b, g, i: (b, 0, part * G + g))
    return pl.pallas_call(
        _nsa_kernel,
        grid=(B, G, S // tq),
        in_specs=[qspec, lane_spec, qspec, kv(0), kv(1), kv(2), kv(3), lane_spec, _full(onehot.shape)],
        out_specs=qspec,
        out_shape=jax.ShapeDtypeStruct(qr.shape, BF16),
        scratch_shapes=[pltpu.VMEM((HEAD_DIM + ONES_ROWS, S), BF16), pltpu.VMEM((HEAD_DIM + ONES_ROWS, S), BF16)],
        compiler_params=_params(("arbitrary", "arbitrary", "arbitrary")),
        name="nsa",
    )(qr, bias, ocmp, kvr, kvr, kvr, kvr, gates, onehot)


def _lru_kernel(lx_ref, lg_ref, cw_ref, cb_ref, wa_ref, ba_ref, wx_ref, bx_ref, lam_ref, o_ref,
                xs_sc, a_sc, u_sc, h_sc, carry_sc):
    ts = lx_ref.shape[0]
    width = lx_ref.shape[1]
    bw = width // LRU_BLOCKS

    @pl.when(pl.program_id(1) == 0)
    def _():
        xs_sc[0:SUBLANES, :] = jnp.zeros((SUBLANES, width), F32)
        carry_sc[...] = jnp.zeros(carry_sc.shape, F32)

    xs_sc[SUBLANES:SUBLANES + ts, :] = lx_ref[...].astype(F32)
    xc = cb_ref[...] + cw_ref[CONV_WIDTH - 1:CONV_WIDTH, :] * xs_sc[SUBLANES:SUBLANES + ts, :]
    for d in range(1, CONV_WIDTH):
        w = cw_ref[CONV_WIDTH - 1 - d:CONV_WIDTH - d, :]
        xc = xc + w * xs_sc[SUBLANES - d:SUBLANES - d + ts, :]
    xs_sc[0:SUBLANES, :] = xs_sc[ts:ts + SUBLANES, :]

    lam = -lam_ref[...]
    neg_c_softplus = -LRU_C * (jnp.maximum(lam, 0.0) + jnp.log1p(jnp.exp(-jnp.abs(lam))))
    xcb = xc.astype(BF16)
    for blk in range(LRU_BLOCKS):
        sl = slice(blk * bw, (blk + 1) * bw)
        r = _sigmoid(_dot(xcb[:, sl], wa_ref[blk]) + ba_ref[:, sl])
        i = _sigmoid(_dot(xcb[:, sl], wx_ref[blk]) + bx_ref[:, sl])
        log_a = r * neg_c_softplus[:, sl]
        a = jnp.exp(log_a)
        a_sc[:, sl] = a
        u_sc[:, sl] = jnp.sqrt(-jnp.tanh(log_a) * (a * a + 1.0)) * (i * xc[:, sl])

    row = lax.broadcasted_iota(jnp.int32, (SUBLANES, width), 0)

    def chunk(c, h_prev):
        r0 = pl.multiple_of(c * SUBLANES, SUBLANES)
        a = a_sc[pl.ds(r0, SUBLANES), :]
        u = u_sc[pl.ds(r0, SUBLANES), :]
        for d in (1, 2, 4):
            keep = row >= d
            u = jnp.where(keep, a * pltpu.roll(u, d, 0) + u, u)
            a = jnp.where(keep, a * pltpu.roll(a, d, 0), a)
        h = a * h_prev + u
        h_sc[pl.ds(r0, SUBLANES), :] = h
        return jnp.broadcast_to(h[SUBLANES - 1:SUBLANES, :], h.shape)

    carry_sc[...] = lax.fori_loop(0, ts // SUBLANES, chunk, carry_sc[...])
    o_ref[...] = (_gelu_tanh(lg_ref[...].astype(F32)) * h_sc[...]).astype(BF16)


def _lru(lx, lg, conv_w, conv_b, wa, ba, wx, bx, lam):
    B, S, W = lx.shape
    ts = min(LRU_ROWS, S)
    row = pl.BlockSpec((None, ts, W), lambda b, s: (b, s, 0))
    consts = (conv_w, conv_b.reshape(1, W), wa, ba.reshape(1, W), wx, bx.reshape(1, W), lam.reshape(1, W))
    return pl.pallas_call(
        _lru_kernel,
        grid=(B, S // ts),
        in_specs=[row, row] + [_full(a.shape) for a in consts],
        out_specs=row,
        out_shape=jax.ShapeDtypeStruct((B, S, W), BF16),
        scratch_shapes=[pltpu.VMEM((ts + SUBLANES, W), F32), pltpu.VMEM((ts, W), F32), pltpu.VMEM((ts, W), F32),
                        pltpu.VMEM((ts, W), F32), pltpu.VMEM((SUBLANES, W), F32)],
        compiler_params=_params(("arbitrary", "arbitrary")),
        name="lru",
    )(lx, lg, *consts)


def _merge_kernel(on_ref, ol_ref, mg_ref, x_ref, mod_ref, g2_ref, wn_ref, wl_ref, wo_ref, rw_ref, rb_ref, tri_ref,
                  x1_ref, h2_ref, rt_ref, cnt_ref, carry_sc):
    first = (pl.program_id(0) == 0) & (pl.program_id(1) == 0)

    @pl.when(first)
    def _():
        carry_sc[...] = jnp.zeros(carry_sc.shape, F32)

    d = x_ref.shape[1]
    y_nsa = _dot(on_ref[...], wn_ref[...])
    y_lru = _dot(ol_ref[...], wl_ref[...])
    merged = mg_ref[:, 0:d].astype(F32) * y_nsa + mg_ref[:, d:2 * d].astype(F32) * y_lru
    gate1 = mod_ref[2:3, :]
    shift2 = mod_ref[3:4, :]
    scale2 = mod_ref[4:5, :]
    x1 = x_ref[...] + gate1 * _dot(merged.astype(BF16), wo_ref[...])
    x1_ref[...] = x1
    h2 = _rms(x1, g2_ref[...]) * (1.0 + scale2) + shift2
    h2_ref[...] = _pack_bf16_pairs(h2)

    rw = rw_ref[...]
    rw_hi = rw.astype(BF16)
    rw_lo = (rw - rw_hi.astype(F32)).astype(BF16)
    h2_hi = h2.astype(BF16)
    h2_lo = (h2 - h2_hi.astype(F32)).astype(BF16)
    logits = _dot_nt(rw_hi, h2_hi) + (_dot_nt(rw_hi, h2_lo) + _dot_nt(rw_lo, h2_hi))
    logits = logits[0:N_EXPERTS] + rb_ref[...]
    tm = logits.shape[1]
    eid = lax.broadcasted_iota(jnp.int32, logits.shape, 0).astype(F32)
    score = logits
    picks = []
    onehot = jnp.zeros(logits.shape, F32)
    for _ in range(TOP_K):
        best = jnp.max(score, axis=0, keepdims=True)
        idx = jnp.min(jnp.where(score == best, eid, float(LANES)), axis=0, keepdims=True)
        hit = eid == idx
        picks.append((idx, best, hit))
        onehot = jnp.where(hit, 1.0, onehot)
        score = jnp.where(hit, LOWEST, score)
    ew = [jnp.exp(v - picks[0][1]) for _, v, _ in picks]
    den = ew[0]
    for v in ew[1:]:
        den = den + v
    inv_den = 1.0 / den

    before = _dot(onehot.astype(BF16), tri_ref[...]) + carry_sc[:, 0:1]
    carry_sc[...] = carry_sc[...] + jnp.sum(onehot, axis=1, keepdims=True)
    cnt_ref[...] = carry_sc[...]

    rows = [idx for idx, _, _ in picks]
    rows += [jnp.sum(jnp.where(hit, before, 0.0), axis=0, keepdims=True) for _, _, hit in picks]
    rows += [e * inv_den for e in ew]
    rows.append(jnp.zeros((LANES - len(rows), tm), F32))
    rt_ref[...] = jnp.concatenate(rows, axis=0).T


def _merge(o_nsa, o_lru, mg, x, mod3, norm2_g, wn, wl, wo, rw, rb, tri):
    B, S, D = x.shape
    tm = tri.shape[0]
    row = lambda w: pl.BlockSpec((None, tm, w), lambda b, s: (b, s, 0))
    consts = (norm2_g.reshape(1, D), wn, wl, wo, rw, rb, tri)
    return pl.pallas_call(
        _merge_kernel,
        grid=(B, S // tm),
        in_specs=[row(D), row(D), row(2 * D), row(D), pl.BlockSpec((None, 6, D), lambda b, s: (b, 0, 0))]
                 + [_full(a.shape) for a in consts],
        out_specs=[row(D), row(D // 2), row(LANES), _full((N_EXPERTS, LANES))],
        out_shape=[jax.ShapeDtypeStruct((B, S, D), F32), jax.ShapeDtypeStruct((B, S, D // 2), jnp.int32),
                   jax.ShapeDtypeStruct((B, S, LANES), F32), jax.ShapeDtypeStruct((N_EXPERTS, LANES), F32)],
        scratch_shapes=[pltpu.VMEM((N_EXPERTS, LANES), F32)],
        compiler_params=_params(("arbitrary", "arbitrary")),
        name="merge",
    )(o_nsa, o_lru, mg, x, mod3, *consts)


def _expert_kernel(be_ref, nb_ref, x_ref, w1_ref, b1_ref, w2_ref, b2_ref, pick_ref, y_ref, w1_sc, w2_sc):
    i = pl.program_id(0)
    ff = w2_ref.shape[0]
    chunk = pick_ref.shape[0]

    @pl.when((i == 0) | (be_ref[i] != be_ref[jnp.maximum(i - 1, 0)]))
    def _():
        for c in range(2 * ff // chunk):
            r = _dot(w1_ref[:, c * chunk:(c + 1) * chunk].astype(BF16), pick_ref[...])
            half = chunk // 2
            w1_sc[:, c * half:(c + 1) * half] = r[:, 0:half].astype(BF16)
            w1_sc[:, ff + c * half:ff + (c + 1) * half] = r[:, half:chunk].astype(BF16)
        w2_sc[...] = w2_ref[...].astype(BF16)

    @pl.when(i < nb_ref[0])
    def _():
        h = _dot(_unpack_bf16_pairs(x_ref[...]).astype(BF16), w1_sc[...]) + b1_ref[...]
        x_glu = jnp.minimum(h[:, 0:ff], SWIGLU_LIMIT)
        x_lin = jnp.clip(h[:, ff:2 * ff], -SWIGLU_LIMIT, SWIGLU_LIMIT)
        act = x_glu * _sigmoid(SWIGLU_ALPHA * x_glu) * (x_lin + 1.0)
        y_ref[...] = _pack_bf16_pairs(_dot(act.astype(BF16), w2_sc[...]) + b2_ref[...])

    @pl.when(i >= nb_ref[0])
    def _():
        y_ref[...] = jnp.zeros(y_ref.shape, jnp.int32)


def _experts(blk_e, n_used, buf, w1, b1, w2, b2):
    n_rows = buf.shape[0]
    F, D = w2.shape[1], w2.shape[2]
    tm = EXPERT_ROWS
    chunk = 2 * LANES
    pick = np.zeros((chunk, chunk), np.float32)
    pick[2 * np.arange(LANES), np.arange(LANES)] = 1.0
    pick[2 * np.arange(LANES) + 1, LANES + np.arange(LANES)] = 1.0
    wspec = lambda a, b: pl.BlockSpec((None, a, b), lambda i, be, nb: (be[i], 0, 0))
    return pl.pallas_call(
        _expert_kernel,
        grid_spec=pltpu.PrefetchScalarGridSpec(
            num_scalar_prefetch=2,
            grid=(n_rows // tm,),
            in_specs=[pl.BlockSpec((tm, D // 2), lambda i, be, nb: (i, 0)), wspec(D, 2 * F), wspec(1, 2 * F), wspec(F, D),
                      wspec(1, D), pl.BlockSpec((chunk, chunk), lambda i, be, nb: (0, 0))],
            out_specs=pl.BlockSpec((tm, D // 2), lambda i, be, nb: (i, 0)),
            scratch_shapes=[pltpu.VMEM((D, 2 * F), BF16), pltpu.VMEM((F, D), BF16)],
        ),
        out_shape=jax.ShapeDtypeStruct((n_rows, D // 2), jnp.int32),
        compiler_params=_params(("arbitrary",)),
        name="experts",
    )(blk_e, n_used, buf, w1, b1, w2, b2, jnp.asarray(pick, dtype=BF16))


def _sc_mesh():
    return plsc.VectorSubcoreMesh(core_axis_name="c", subcore_axis_name="s")


def _sc_worker():
    return lax.axis_index("s") * SC_CORES + lax.axis_index("c")


def _sc_scatter_rows(rows, dest, n_rows):
    T, W = rows.shape
    K = dest.shape[0]
    n_workers = SC_CORES * SC_SUBCORES
    per_w = T // n_workers
    ch = SC_CHUNK
    n_ch = per_w // ch
    assert per_w * n_workers == T and n_ch * ch == per_w and n_ch % 2 == 0
    dest4 = dest.reshape(K, n_workers, n_ch, ch).transpose(1, 2, 0, 3)

    @functools.partial(
        pl.kernel, mesh=_sc_mesh(), out_type=jax.ShapeDtypeStruct((n_rows, W), rows.dtype),
        scratch_types=[pltpu.VMEM((n_ch, K, ch), jnp.int32), pltpu.VMEM((2, ch, W), rows.dtype),
                       pltpu.SemaphoreType.DMA((2,)), pltpu.SemaphoreType.DMA((2,))])
    def scatter(rows_hbm, dest_hbm, out_hbm, idx_v, rows_v, read_sem, write_sem):
        wid = _sc_worker()
        base = wid * per_w
        pltpu.sync_copy(dest_hbm.at[wid], idx_v)

        def read(c, b):
            src = rows_hbm.at[pl.ds(pl.multiple_of(base + c * ch, ch), ch)]
            return pltpu.make_async_copy(src, rows_v.at[b], read_sem.at[b])

        def write(c, b, k):
            return pltpu.make_async_copy(rows_v.at[b], out_hbm.at[idx_v.at[c, k]], write_sem.at[b])

        read(0, 0).start()

        @pl.loop(0, n_ch, step=2)
        def _(i):
            for b in range(2):
                c = i + b
                read(c, b).wait()

                @pl.when(c >= 1)
                def _():
                    for k in range(K):
                        write(c - 1, 1 - b, k).wait()

                @pl.when(c + 1 < n_ch)
                def _():
                    read(c + 1, 1 - b).start()

                for k in range(K):
                    write(c, b, k).start()

        for k in range(K):
            write(n_ch - 1, 1, k).wait()

    return scatter(rows, dest4)


def _sc_gather_rows(table, idx):
    N = idx.shape[0]
    W = table.shape[1]
    n_workers = SC_CORES * SC_SUBCORES
    per_w = N // n_workers
    ch = SC_CHUNK
    n_ch = per_w // ch
    assert per_w * n_workers == N and n_ch * ch == per_w and n_ch % 2 == 0
    idx3 = idx.reshape(n_workers, n_ch, ch)

    @functools.partial(
        pl.kernel, mesh=_sc_mesh(), out_type=jax.ShapeDtypeStruct((N, W), table.dtype),
        scratch_types=[pltpu.VMEM((n_ch, ch), jnp.int32), pltpu.VMEM((2, ch, W), table.dtype),
                       pltpu.SemaphoreType.DMA((2,)), pltpu.SemaphoreType.DMA((2,))])
    def gather(table_hbm, idx_hbm, out_hbm, idx_v, rows_v, gather_sem, write_sem):
        wid = _sc_worker()
        base = wid * per_w
        pltpu.sync_copy(idx_hbm.at[wid], idx_v)

        def fetch(c, b):
            return pltpu.make_async_copy(table_hbm.at[idx_v.at[c]], rows_v.at[b], gather_sem.at[b])

        def write(c, b):
            dst = out_hbm.at[pl.ds(pl.multiple_of(base + c * ch, ch), ch)]
            return pltpu.make_async_copy(rows_v.at[b], dst, write_sem.at[b])

        fetch(0, 0).start()

        @pl.loop(0, n_ch, step=2)
        def _(i):
            for b in range(2):
                c = i + b
                fetch(c, b).wait()

                @pl.when(c >= 1)
                def _():
                    write(c - 1, 1 - b).wait()

                @pl.when(c + 1 < n_ch)
                def _():
                    fetch(c + 1, 1 - b).start()

                write(c, b).start()

        write(n_ch - 1, 1).wait()

    return gather(table, idx3)


def _combine_kernel(x1_ref, yg_ref, rt_ref, mod_ref, *rest):
    o_ref = rest[-1]
    gate2 = mod_ref[5:6, :]
    rt = rt_ref[...]
    acc = rt[:, 2 * TOP_K:2 * TOP_K + 1] * _unpack_bf16_pairs(yg_ref[0])
    for k in range(1, TOP_K):
        acc = acc + rt[:, 2 * TOP_K + k:2 * TOP_K + k + 1] * _unpack_bf16_pairs(yg_ref[k])
    o_ref[...] = x1_ref[...] + gate2 * acc


def _combine(x1, yg, rt, mod3, prev, b0):
    B, S, D = x1.shape
    nb = yg.shape[1]
    tm = min(COMBINE_ROWS, S)
    row = lambda w: pl.BlockSpec((None, tm, w), lambda b, s: (b0 + b, s, 0))
    in_specs = [row(D), pl.BlockSpec((TOP_K, None, tm, D // 2), lambda b, s: (0, b, s, 0)), row(LANES),
                pl.BlockSpec((None, 6, D), lambda b, s: (b0 + b, 0, 0))]
    operands = [x1, yg, rt, mod3]
    if prev is not None:
        in_specs.append(pl.BlockSpec(memory_space=pl.ANY))
        operands.append(prev)
    return pl.pallas_call(
        _combine_kernel,
        grid=(nb, S // tm),
        in_specs=in_specs,
        out_specs=row(D),
        out_shape=jax.ShapeDtypeStruct((B, S, D), F32),
        input_output_aliases={} if prev is None else {4: 0},
        compiler_params=_params(("arbitrary", "arbitrary")),
        name="combine",
    )(*operands)


def _overlap_matrix(n_cmp_pad, n_blk):
    cs = np.arange(n_cmp_pad)[:, None] * CMP_STRIDE
    js = np.arange(LANES)[None, :] * SEL_BLOCK
    m = (cs <= js + SEL_BLOCK - 1) & (cs + CMP_BLOCK - 1 >= js) & (np.arange(LANES)[None, :] < n_blk)
    return jnp.asarray(m.astype(np.float32).T)


def _block_onehot(seq):
    key = np.arange(seq)[:, None]
    j = np.arange(LANES)[None, :]
    return jnp.asarray((j == key // SEL_BLOCK).astype(np.float32), dtype=BF16)


def _layer(x, c, ang, ada_w, ada_b, norm1_g, w_in, q_norm_g, k_norm_g, cmp_pe_k, cmp_pe_v, cmp_wk1, cmp_wk2, cmp_wv1,
           cmp_wv2, conv_w, conv_b, lru_wa, lru_ba, lru_wx, lru_bx, lru_lambda, w_branch_out, w_out, norm2_g,
           router_w, router_b, moe_w1, moe_b1, moe_w2, moe_b2):
    B, S, D = x.shape
    T = B * S
    G = N_KV_GROUPS
    nsa_w = N_HEADS * HEAD_DIM
    kv_w = G * HEAD_DIM

    mod3 = _ada(c, ada_w, ada_b).reshape(B, 6, D)

    o = 0
    wq = w_in[:, o:o + nsa_w]; o += nsa_w
    wkc = w_in[:, o:o + 2 * kv_w]; o += 2 * kv_w
    wkr = w_in[:, o:o + 4 * kv_w]; o += 4 * kv_w
    n_gate = N_HEADS * N_NSA_BRANCHES
    wg_raw = w_in[:, o:o + n_gate].reshape(D, G, n_gate // G); o += n_gate
    wg = jnp.pad(wg_raw, ((0, 0), (0, 0), (0, LANES - n_gate // G))).reshape(D, G * LANES)
    wlx = w_in[:, o:o + D]; o += D
    wlg = w_in[:, o:o + D]; o += D
    wm = w_in[:, o:o + 2 * D]
    bf = lambda a: a.astype(BF16)

    qn, qr, kvc, kvr, gates, lx, lg, mg = _in_proj(x, mod3, norm1_g, ang, q_norm_g, k_norm_g, bf(wq), bf(wkc), bf(wkr),
                                                   bf(wg), bf(wlx), bf(wlg), bf(wm))

    nc = S // CMP_STRIDE
    pe = lambda p: jnp.broadcast_to(p.reshape(1, CMP_BLOCK * HEAD_DIM), (SUBLANES, CMP_BLOCK * HEAD_DIM)).astype(BF16)
    kc, vc = _compress(kvc, pe(cmp_pe_k), pe(cmp_pe_v), bf(cmp_wk1), bf(cmp_wk2), bf(cmp_wv1), bf(cmp_wv2), k_norm_g)

    ocmp, bias = _select(qn, kc, vc, gates, _overlap_matrix(nc, S // SEL_BLOCK), S // SEL_BLOCK)
    o_nsa = _nsa(qr, bias, ocmp, kvr, gates, _block_onehot(S))
    o_lru = _lru(lx, lg, conv_w, conv_b, bf(lru_wa), lru_ba, bf(lru_wx), lru_bx, lru_lambda)

    tm = min(IN_ROWS, S)
    tri = jnp.asarray(np.triu(np.ones((tm, tm), np.float32), 1), dtype=BF16)
    rw = jnp.pad(router_w.T, ((0, LANES - N_EXPERTS), (0, 0)))
    rb = router_b.reshape(N_EXPERTS, 1)
    x1, h2, rt, cnt = _merge(o_nsa, o_lru, mg, x, mod3, norm2_g, bf(w_branch_out[:nsa_w]), bf(w_branch_out[nsa_w:]),
                             bf(w_out), rw, rb, tri)

    rt2 = rt.reshape(T, LANES)
    top_e = rt2[:, 0:TOP_K].astype(jnp.int32)
    rank = rt2[:, TOP_K:2 * TOP_K].astype(jnp.int32)
    counts = cnt[:, 0].astype(jnp.int32)
    padded = (counts + EXPERT_ROWS - 1) // EXPERT_ROWS * EXPERT_ROWS
    ends = jnp.cumsum(padded)
    start = ends - padded
    dest = start[top_e] + rank
    n_rows = -(-(T * TOP_K + N_EXPERTS * (EXPERT_ROWS - 1)) // EXPERT_ROWS) * EXPERT_ROWS
    n_blocks = n_rows // EXPERT_ROWS
    blk_first = jnp.arange(n_blocks, dtype=jnp.int32) * EXPERT_ROWS
    blk_e = jnp.minimum(jnp.sum(ends[None, :] <= blk_first[:, None], axis=1), N_EXPERTS - 1).astype(jnp.int32)
    n_used = (ends[-1:] // EXPERT_ROWS).astype(jnp.int32)

    dest_kt = dest.T
    buf = _sc_scatter_rows(h2.reshape(T, D // 2), dest_kt, n_rows)

    b1 = jnp.concatenate([moe_b1[:, 0::2], moe_b1[:, 1::2]], axis=1)
    ybuf = _experts(blk_e, n_used, buf, moe_w1, b1.reshape(N_EXPERTS, 1, -1), moe_w2, moe_b2.reshape(N_EXPERTS, 1, D))
    nb = B // COMBINE_PARTS if B % COMBINE_PARTS == 0 else B
    dest_parts = dest_kt.reshape(TOP_K, B // nb, nb * S)
    out = None
    for p in range(B // nb):
        yg = _sc_gather_rows(ybuf, dest_parts[:, p].reshape(-1)).reshape(TOP_K, nb, S, D // 2)
        out = _combine(x1, yg, rt, mod3, out, p * nb)
    return out


def kernel(x, c, positions, ada_w, ada_b, norm1_g, w_in, q_norm_g, k_norm_g, cmp_pe_k, cmp_pe_v, cmp_wk1, cmp_wk2, cmp_wv1, cmp_wv2, conv_w, conv_b, lru_wa, lru_ba, lru_wx, lru_bx, lru_lambda, w_branch_out, w_out, norm2_g, router_w, router_b, moe_w1, moe_b1, moe_w2, moe_b2):
    inv = ROPE_THETA ** (-jnp.arange(0, HEAD_DIM, 2, dtype=F32) / HEAD_DIM)
    ang = positions.astype(F32)[..., None] * inv
    ang = jnp.concatenate([ang, ang], axis=-1)
    for l in range(ada_w.shape[0]):
        x = _layer(x, c, ang, ada_w[l], ada_b[l], norm1_g[l], w_in[l], q_norm_g[l], k_norm_g[l], cmp_pe_k[l],
                   cmp_pe_v[l], cmp_wk1[l], cmp_wk2[l], cmp_wv1[l], cmp_wv2[l], conv_w[l], conv_b[l], lru_wa[l],
                   lru_ba[l], lru_wx[l], lru_bx[l], lru_lambda[l], w_branch_out[l], w_out[l], norm2_g[l], router_w[l],
                   router_b[l], moe_w1[l], moe_b1[l], moe_w2[l], moe_b2[l])
    return x
```

```python
import functools
import math

import jax
import jax.numpy as jnp
import numpy as np
from jax import lax
from jax.experimental import pallas as pl
from jax.experimental.pallas import tpu as pltpu
from jax.experimental.pallas import tpu_sc as plsc

F32 = jnp.float32
BF16 = jnp.bfloat16

N_HEADS = 8
HEAD_DIM = 128
N_KV_GROUPS = 2
HEADS_PER_GROUP = N_HEADS // N_KV_GROUPS
N_NSA_BRANCHES = 3
CMP_BLOCK = 32
CMP_STRIDE = 16
CMP_HIDDEN = 256
SEL_BLOCK = 64
N_SELECT = 8
WINDOW = 512
ROPE_THETA = 10000.0
LRU_BLOCKS = 4
CONV_WIDTH = 4
LRU_C = 8.0
N_EXPERTS = 32
TOP_K = 4
SWIGLU_LIMIT = 7.0
SWIGLU_ALPHA = 1.702
RMS_EPS = 1e-6
NEG_INF = -1e30
LOWEST = -3.0e38
LOG2_E = 1.4426950408889634

LANES = 128
SUBLANES = 8
VMEM_LIMIT = 56 * 1024 * 1024
SC_CORES = 2
SC_SUBCORES = 16
SC_CHUNK = 64

Q_TILE = 256
SELECT_ROWS = 1024
KEY_TILE = 128
ONES_ROWS = 16
HEAD_PAIR = 2
IN_ROWS = 512
LRU_ROWS = 512
EXPERT_ROWS = 512
COMBINE_ROWS = 256
COMBINE_PARTS = 4


def _sigmoid(v):
    return 0.5 * jnp.tanh(0.5 * v) + 0.5


def _gelu_tanh(v):
    c = math.sqrt(2.0 / math.pi)
    half = 0.5 * v
    return half + half * jnp.tanh(v * (c + (c * 0.044715) * (v * v)))


def _rms(v, g):
    return v * lax.rsqrt(jnp.mean(v * v, axis=-1, keepdims=True) + RMS_EPS) * g


def _dot(a, b, **kw):
    return jnp.dot(a, b, preferred_element_type=F32, **kw)


def _dot_nt(a, b, **kw):
    return lax.dot_general(a, b, (((1,), (1,)), ((), ())), preferred_element_type=F32, **kw)


def _pack_bf16_pairs(v):
    n = v.shape[1] // 2
    lo = lax.bitcast_convert_type(v[:, 0:n].astype(BF16).astype(F32), jnp.int32)
    hi = lax.bitcast_convert_type(v[:, n:2 * n].astype(BF16).astype(F32), jnp.int32)
    return lax.shift_right_logical(lo, 16) | hi


def _unpack_bf16_pairs(w):
    lo = lax.bitcast_convert_type(lax.shift_left(w, 16), F32)
    hi = lax.bitcast_convert_type(w & jnp.int32(-65536), F32)
    return jnp.concatenate([lo, hi], axis=1)


def _full(shape):
    nd = len(shape)
    return pl.BlockSpec(shape, lambda *_: (0,) * nd)


def _params(sem):
    return pltpu.CompilerParams(dimension_semantics=sem, vmem_limit_bytes=VMEM_LIMIT)


def _ada_kernel(c_ref, w_ref, b_ref, o_ref):
    c = c_ref[...]
    o_ref[...] = _dot(c * _sigmoid(c), w_ref[...], precision=lax.Precision.HIGHEST) + b_ref[...]


def _ada(c, ada_w, ada_b):
    B, D = c.shape
    N = ada_w.shape[1]
    return pl.pallas_call(
        _ada_kernel,
        grid=(N // D,),
        in_specs=[_full((B, D)), pl.BlockSpec((D, D), lambda j: (0, j)), pl.BlockSpec((1, D), lambda j: (0, j))],
        out_specs=pl.BlockSpec((B, D), lambda j: (0, j)),
        out_shape=jax.ShapeDtypeStruct((B, N), F32),
        compiler_params=_params(("arbitrary",)),
        name="ada",
    )(c, ada_w, ada_b.reshape(1, N))


def _in_kernel(x_ref, mod_ref, g1_ref, ang_ref, qg_ref, kg_ref, wq_ref, wkc_ref, wkr_ref, wg_ref, wlx_ref, wlg_ref,
               wm_ref, qn_ref, qr_ref, kvc_ref, kvr_ref, gt_ref, lx_ref, lg_ref, mg_ref, kvc_sc):
    x = x_ref[...]
    shift1 = mod_ref[0:1, :]
    scale1 = mod_ref[1:2, :]
    h = _rms(x, g1_ref[...]) * (1.0 + scale1) + shift1
    hb = h.astype(BF16)

    ang = ang_ref[...]
    cos = jnp.cos(ang)
    sin = jnp.sin(ang)
    lane = lax.broadcasted_iota(jnp.int32, ang.shape, 1)
    sin_signed = jnp.where(lane < HEAD_DIM // 2, -sin, sin)

    def rope(v):
        return v * cos + pltpu.roll(v, HEAD_DIM // 2, 1) * sin_signed

    q = _dot(hb, wq_ref[...])
    for hh in range(N_HEADS):
        sl = slice(hh * HEAD_DIM, (hh + 1) * HEAD_DIM)
        qh = _rms(q[:, sl], qg_ref[...])
        qn_ref[:, sl] = (qh * HEAD_DIM ** -0.5).astype(BF16)
        qr_ref[:, sl] = (rope(qh) * (HEAD_DIM ** -0.5 * LOG2_E)).astype(BF16)

    kvc = _dot(hb, wkc_ref[...])
    for part in range(kvc_ref.shape[0]):
        kvc_sc[part] = kvc[:, part * HEAD_DIM:(part + 1) * HEAD_DIM]
        for tok in range(CMP_STRIDE):
            piece = kvc_sc[part, pl.ds(tok, kvc_ref.shape[1], stride=CMP_STRIDE), :]
            kvc_ref[part, :, tok * HEAD_DIM:(tok + 1) * HEAD_DIM] = piece.astype(BF16)

    kvr = _dot(hb, wkr_ref[...])
    kvw = N_KV_GROUPS * HEAD_DIM
    for part in range(4):
        for gg in range(N_KV_GROUPS):
            sl = slice(part * kvw + gg * HEAD_DIM, part * kvw + (gg + 1) * HEAD_DIM)
            v = kvr[:, sl]
            if part % 2 == 0:
                row = 1 + part // 2
                v = rope(_rms(v, kg_ref[row:row + 1, :]))
            kvr_ref[:, sl] = v.astype(BF16)

    gt_ref[...] = _sigmoid(_dot(hb, wg_ref[...]))
    lx_ref[...] = _dot(hb, wlx_ref[...]).astype(BF16)
    lg_ref[...] = _dot(hb, wlg_ref[...]).astype(BF16)
    mg_ref[...] = _sigmoid(_dot(hb, wm_ref[...])).astype(BF16)


def _in_proj(x, mod3, norm1_g, ang, q_norm_g, k_norm_g, wq, wkc, wkr, wg, wlx, wlg, wm):
    B, S, D = x.shape
    tm = min(IN_ROWS, S)
    row = lambda w: pl.BlockSpec((None, tm, w), lambda b, s: (b, s, 0))
    widths = (wq.shape[1], wq.shape[1], wkc.shape[1], wkr.shape[1], wg.shape[1], wlx.shape[1], wlg.shape[1], wm.shape[1])
    dtypes = (BF16, BF16, BF16, BF16, F32, BF16, BF16, BF16)
    weights = (wq, wkc, wkr, wg, wlx, wlg, wm)
    out_specs = [row(w) for w in widths]
    out_shape = [jax.ShapeDtypeStruct((B, S, w), dt) for w, dt in zip(widths, dtypes)]
    n_part = wkc.shape[1] // HEAD_DIM
    out_specs[2] = pl.BlockSpec((None, n_part, tm // CMP_STRIDE, CMP_STRIDE * HEAD_DIM), lambda b, s: (b, 0, s, 0))
    out_shape[2] = jax.ShapeDtypeStruct((B, n_part, S // CMP_STRIDE, CMP_STRIDE * HEAD_DIM), BF16)
    return pl.pallas_call(
        _in_kernel,
        grid=(B, S // tm),
        in_specs=[row(D), pl.BlockSpec((None, 6, D), lambda b, s: (b, 0, 0)), _full((1, D)), row(HEAD_DIM),
                  _full((1, HEAD_DIM)), _full(k_norm_g.shape)] + [_full(w.shape) for w in weights],
        out_specs=out_specs,
        out_shape=out_shape,
        scratch_shapes=[pltpu.VMEM((n_part, tm, HEAD_DIM), F32)],
        compiler_params=_params(("arbitrary", "arbitrary")),
        name="in_proj",
    )(x, mod3, norm1_g.reshape(1, D), ang, q_norm_g.reshape(1, HEAD_DIM), k_norm_g, *weights)


def _cmp_kernel(ak_ref, av_ref, pek_ref, pev_ref, wk1_ref, wk2_ref, wv1_ref, wv2_ref, kg_ref, kc_ref, vc_ref):
    half = CMP_STRIDE * HEAD_DIM

    def compress(a_ref, pe_ref, w1_ref, w2_ref):
        a = a_ref[...]
        u = _dot(a, w1_ref[0:half, :])
        v = _dot(a, w1_ref[half:2 * half, :])
        pw = _dot(pe_ref[...], w1_ref[...])
        pre = u + pltpu.roll(v, v.shape[0] - 1, 0) + pw[0:1, :]
        return _dot(_gelu_tanh(pre).astype(BF16), w2_ref[...])

    kc_ref[...] = _rms(compress(ak_ref, pek_ref, wk1_ref, wk2_ref), kg_ref[0:1, :]).astype(BF16)
    vc_ref[...] = compress(av_ref, pev_ref, wv1_ref, wv2_ref).T.astype(BF16)


def _compress(a, pe_k, pe_v, wk1, wk2, wv1, wv2, k_norm_g):
    B, _, NC, W = a.shape
    G = N_KV_GROUPS
    out = pl.BlockSpec((None, None, NC, HEAD_DIM), lambda b, g: (b, g, 0, 0))
    consts = (pe_k, pe_v, wk1, wk2, wv1, wv2, k_norm_g)
    return pl.pallas_call(
        _cmp_kernel,
        grid=(B, G),
        in_specs=[pl.BlockSpec((None, None, NC, W), lambda b, g: (b, g, 0, 0)),
                  pl.BlockSpec((None, None, NC, W), lambda b, g: (b, G + g, 0, 0))] + [_full(c.shape) for c in consts],
        out_specs=[out, out],
        out_shape=[jax.ShapeDtypeStruct((B, G, NC, HEAD_DIM), BF16)] * 2,
        compiler_params=_params(("arbitrary", "arbitrary")),
        name="compress",
    )(a, a, *consts)


def _select_kernel(qn_ref, kc_ref, vc_ref, gt_ref, ov_ref, ocmp_ref, bias_ref, *, n_blk):
    qi = pl.program_id(2)
    tq = qn_ref.shape[0]
    hpg = HEADS_PER_GROUP
    qn = jnp.concatenate([qn_ref[:, h * HEAD_DIM:(h + 1) * HEAD_DIM] for h in range(hpg)], axis=0)
    row = lax.broadcasted_iota(jnp.int32, (kc_ref.shape[0], tq), 0)
    t_q = qi * tq + lax.broadcasted_iota(jnp.int32, (kc_ref.shape[0], tq), 1)

    cmask = jnp.concatenate([row * CMP_STRIDE + (CMP_BLOCK - 1) <= t_q] * hpg, axis=1)
    s = jnp.where(cmask, _dot_nt(kc_ref[...], qn), NEG_INF)
    e = jnp.exp(s - jnp.max(s, axis=0, keepdims=True))
    p = jnp.where(cmask, e * (1.0 / jnp.sum(e, axis=0, keepdims=True)), 0.0)
    o_cmp = _dot(vc_ref[...], p.astype(BF16))
    gt = gt_ref[...]
    for h in range(hpg):
        c0 = h * N_NSA_BRANCHES
        ocmp_ref[:, h * HEAD_DIM:(h + 1) * HEAD_DIM] = (gt[:, c0:c0 + 1] * o_cmp[:, h * tq:(h + 1) * tq].T).astype(BF16)

    psum = p[:, 0:tq]
    for h in range(1, hpg):
        psum = psum + p[:, h * tq:(h + 1) * tq]
    imp = _dot(ov_ref[...], psum, precision=lax.Precision.HIGHEST)[0:n_blk]
    blk = lax.broadcasted_iota(jnp.int32, (n_blk, tq), 0).astype(F32)
    cur = ((qi * tq + lax.broadcasted_iota(jnp.int32, (n_blk, tq), 1)) // SEL_BLOCK).astype(F32)
    forced = (blk == 0.0) | (blk == cur) | (blk == cur - 1.0)
    score = jnp.where(forced, 1e6, jnp.where(blk <= cur, imp, -1e6))
    bias = jnp.full((n_blk, tq), NEG_INF, F32)
    for _ in range(min(N_SELECT, n_blk)):
        best = jnp.max(score, axis=0, keepdims=True)
        idx = jnp.min(jnp.where(score == best, blk, float(LANES)), axis=0, keepdims=True)
        hit = blk == idx
        bias = jnp.where(hit, 0.0, bias)
        score = jnp.where(hit, LOWEST, score)
    bias = jnp.concatenate([bias, jnp.zeros((LANES - n_blk, tq), F32)], axis=0)
    bias_ref[...] = bias.T.astype(BF16)


def _select(qn, kc, vc, gates, overlap, n_blk):
    B, S, _ = qn.shape
    G = N_KV_GROUPS
    tq = min(SELECT_ROWS, S)
    gw = HEADS_PER_GROUP * HEAD_DIM
    qspec = pl.BlockSpec((None, tq, gw), lambda b, g, i: (b, i, g))
    lane_spec = pl.BlockSpec((None, tq, LANES), lambda b, g, i: (b, i, g))
    cspec = pl.BlockSpec((None, None) + kc.shape[2:], lambda b, g, i: (b, g, 0, 0))
    return pl.pallas_call(
        functools.partial(_select_kernel, n_blk=n_blk),
        grid=(B, G, S // tq),
        in_specs=[qspec, cspec, cspec, lane_spec, _full(overlap.shape)],
        out_specs=[qspec, lane_spec],
        out_shape=[jax.ShapeDtypeStruct(qn.shape, BF16), jax.ShapeDtypeStruct((B, S, G * LANES), BF16)],
        compiler_params=_params(("arbitrary", "arbitrary", "arbitrary")),
        name="select",
    )(qn, kc, vc, gates, overlap)


def _nsa_kernel(qr_ref, bias_ref, ocmp_ref, ks_ref, vs_ref, kw_ref, vw_ref, gt_ref, ex_ref, o_ref, vst_sc, vwt_sc):
    qi = pl.program_id(2)
    tq = qr_ref.shape[0]
    hpg = HEADS_PER_GROUP
    seq = ks_ref.shape[0]

    @pl.when(qi == 0)
    def _():
        for kt in range(seq // KEY_TILE):
            rows = slice(kt * KEY_TILE, (kt + 1) * KEY_TILE)
            vst_sc[0:HEAD_DIM, rows] = vs_ref[rows, :].astype(F32).T.astype(BF16)
            vwt_sc[0:HEAD_DIM, rows] = vw_ref[rows, :].astype(F32).T.astype(BF16)
        ones = jnp.ones((vst_sc.shape[0] - HEAD_DIM, seq), BF16)
        vst_sc[HEAD_DIM:, :] = ones
        vwt_sc[HEAD_DIM:, :] = ones

    def pair(v):
        return jnp.concatenate([v] * HEAD_PAIR, axis=1)

    qr = jnp.concatenate([qr_ref[:, h * HEAD_DIM:(h + 1) * HEAD_DIM] for h in range(hpg)], axis=0)
    q_aug = jnp.concatenate([qr, jnp.concatenate([bias_ref[...]] * hpg, axis=0)], axis=1)
    pair_cols = [slice(hp * HEAD_PAIR * tq, (hp + 1) * HEAD_PAIR * tq) for hp in range(hpg // HEAD_PAIR)]

    def softmax_pv(sc, vt):
        pr = jnp.exp2((sc - jnp.max(sc, axis=0, keepdims=True)).astype(BF16))
        o = _dot(vt, pr)
        return o[0:HEAD_DIM] * (1.0 / o[HEAD_DIM:HEAD_DIM + 1])

    def biased(sc, first_bias, last_bias):
        n = sc.shape[0]
        if first_bias is not None:
            sc = jnp.concatenate([sc[0:tq] + first_bias, sc[tq:n]], axis=0)
        return jnp.concatenate([sc[0:n - tq], sc[n - tq:n] + last_bias], axis=0) if n > tq else sc + last_bias

    r_in = lax.broadcasted_iota(jnp.int32, (tq, tq), 0)
    c_in = lax.broadcasted_iota(jnp.int32, (tq, tq), 1)
    causal = pair(jnp.where(r_in <= c_in, 0.0, NEG_INF))
    window_tail = pair(jnp.where(r_in > c_in, 0.0, NEG_INF))
    gt = gt_ref[...].T

    for v in range(seq // tq):
        @pl.when(qi == v)
        def _():
            ext = (v + 1) * tq
            k_aug = jnp.concatenate([ks_ref[0:ext, :], ex_ref[0:ext, :]], axis=1)
            w_lo = max(v * tq - WINDOW, 0)
            tail = window_tail if v * tq - WINDOW >= 0 else None
            k_win = kw_ref[w_lo:ext, :]
            scores = [biased(_dot_nt(k_aug, q_aug[cols]), None, causal) for cols in pair_cols]
            win_scores = [biased(_dot_nt(k_win, qr[cols]), tail, causal) for cols in pair_cols]
            o_slcs = [softmax_pv(sc, vst_sc[:, 0:ext]) for sc in scores]
            o_wins = [softmax_pv(sc, vwt_sc[:, w_lo:ext]) for sc in win_scores]
            for h in range(hpg):
                hp, j = divmod(h, HEAD_PAIR)
                sub = slice(j * tq, (j + 1) * tq)
                c0 = h * N_NSA_BRANCHES
                o = gt[c0 + 1:c0 + 2, :] * o_slcs[hp][:, sub] + gt[c0 + 2:c0 + 3, :] * o_wins[hp][:, sub]
                hd = slice(h * HEAD_DIM, (h + 1) * HEAD_DIM)
                o_ref[:, hd] = (o.T + ocmp_ref[:, hd].astype(F32)).astype(BF16)


def _nsa(qr, bias, ocmp, kvr, gates, onehot):
    B, S, _ = qr.shape
    G = N_KV_GROUPS
    tq = Q_TILE
    gw = HEADS_PER_GROUP * HEAD_DIM
    qspec = pl.BlockSpec((None, tq, gw), lambda b, g, i: (b, i, g))
    lane_spec = pl.BlockSpec((None, tq, LANES), lambda b, g, i: (b, i, g))
    kv = lambda part: pl.BlockSpec((None, S, HEAD_DIM), lambda b, g, i: (b, 0, part * G + g))
    return pl.pallas_call(
        _nsa_kernel,
        grid=(B, G, S // tq),
        in_specs=[qspec, lane_spec, qspec, kv(0), kv(1), kv(2), kv(3), lane_spec, _full(onehot.shape)],
        out_specs=qspec,
        out_shape=jax.ShapeDtypeStruct(qr.shape, BF16),
        scratch_shapes=[pltpu.VMEM((HEAD_DIM + ONES_ROWS, S), BF16), pltpu.VMEM((HEAD_DIM + ONES_ROWS, S), BF16)],
        compiler_params=_params(("arbitrary", "arbitrary", "arbitrary")),
        name="nsa",
    )(qr, bias, ocmp, kvr, kvr, kvr, kvr, gates, onehot)


def _lru_kernel(lx_ref, lg_ref, cw_ref, cb_ref, wa_ref, ba_ref, wx_ref, bx_ref, lam_ref, o_ref,
                xs_sc, a_sc, u_sc, h_sc, carry_sc):
    ts = lx_ref.shape[0]
    width = lx_ref.shape[1]
    bw = width // LRU_BLOCKS

    @pl.when(pl.program_id(1) == 0)
    def _():
        xs_sc[0:SUBLANES, :] = jnp.zeros((SUBLANES, width), F32)
        carry_sc[...] = jnp.zeros(carry_sc.shape, F32)

    xs_sc[SUBLANES:SUBLANES + ts, :] = lx_ref[...].astype(F32)
    xc = cb_ref[...] + cw_ref[CONV_WIDTH - 1:CONV_WIDTH, :] * xs_sc[SUBLANES:SUBLANES + ts, :]
    for d in range(1, CONV_WIDTH):
        w = cw_ref[CONV_WIDTH - 1 - d:CONV_WIDTH - d, :]
        xc = xc + w * xs_sc[SUBLANES - d:SUBLANES - d + ts, :]
    xs_sc[0:SUBLANES, :] = xs_sc[ts:ts + SUBLANES, :]

    lam = -lam_ref[...]
    neg_c_softplus = -LRU_C * (jnp.maximum(lam, 0.0) + jnp.log1p(jnp.exp(-jnp.abs(lam))))
    xcb = xc.astype(BF16)
    for blk in range(LRU_BLOCKS):
        sl = slice(blk * bw, (blk + 1) * bw)
        r = _sigmoid(_dot(xcb[:, sl], wa_ref[blk]) + ba_ref[:, sl])
        i = _sigmoid(_dot(xcb[:, sl], wx_ref[blk]) + bx_ref[:, sl])
        log_a = r * neg_c_softplus[:, sl]
        a = jnp.exp(log_a)
        a_sc[:, sl] = a
        u_sc[:, sl] = jnp.sqrt(-jnp.tanh(log_a) * (a * a + 1.0)) * (i * xc[:, sl])

    row = lax.broadcasted_iota(jnp.int32, (SUBLANES, width), 0)

    def chunk(c, h_prev):
        r0 = pl.multiple_of(c * SUBLANES, SUBLANES)
        a = a_sc[pl.ds(r0, SUBLANES), :]
        u = u_sc[pl.ds(r0, SUBLANES), :]
        for d in (1, 2, 4):
            keep = row >= d
            u = jnp.where(keep, a * pltpu.roll(u, d, 0) + u, u)
            a = jnp.where(keep, a * pltpu.roll(a, d, 0), a)
        h = a * h_prev + u
        h_sc[pl.ds(r0, SUBLANES), :] = h
        return jnp.broadcast_to(h[SUBLANES - 1:SUBLANES, :], h.shape)

    carry_sc[...] = lax.fori_loop(0, ts // SUBLANES, chunk, carry_sc[...])
    o_ref[...] = (_gelu_tanh(lg_ref[...].astype(F32)) * h_sc[...]).astype(BF16)


def _lru(lx, lg, conv_w, conv_b, wa, ba, wx, bx, lam):
    B, S, W = lx.shape
    ts = min(LRU_ROWS, S)
    row = pl.BlockSpec((None, ts, W), lambda b, s: (b, s, 0))
    consts = (conv_w, conv_b.reshape(1, W), wa, ba.reshape(1, W), wx, bx.reshape(1, W), lam.reshape(1, W))
    return pl.pallas_call(
        _lru_kernel,
        grid=(B, S // ts),
        in_specs=[row, row] + [_full(a.shape) for a in consts],
        out_specs=row,
        out_shape=jax.ShapeDtypeStruct((B, S, W), BF16),
        scratch_shapes=[pltpu.VMEM((ts + SUBLANES, W), F32), pltpu.VMEM((ts, W), F32), pltpu.VMEM((ts, W), F32),
                        pltpu.VMEM((ts, W), F32), pltpu.VMEM((SUBLANES, W), F32)],
        compiler_params=_params(("arbitrary", "arbitrary")),
        name="lru",
    )(lx, lg, *consts)


def _merge_kernel(on_ref, ol_ref, mg_ref, x_ref, mod_ref, g2_ref, wn_ref, wl_ref, wo_ref, rw_ref, rb_ref, tri_ref,
                  x1_ref, h2_ref, rt_ref, cnt_ref, carry_sc):
    first = (pl.program_id(0) == 0) & (pl.program_id(1) == 0)

    @pl.when(first)
    def _():
        carry_sc[...] = jnp.zeros(carry_sc.shape, F32)

    d = x_ref.shape[1]
    y_nsa = _dot(on_ref[...], wn_ref[...])
    y_lru = _dot(ol_ref[...], wl_ref[...])
    merged = mg_ref[:, 0:d].astype(F32) * y_nsa + mg_ref[:, d:2 * d].astype(F32) * y_lru
    gate1 = mod_ref[2:3, :]
    shift2 = mod_ref[3:4, :]
    scale2 = mod_ref[4:5, :]
    x1 = x_ref[...] + gate1 * _dot(merged.astype(BF16), wo_ref[...])
    x1_ref[...] = x1
    h2 = _rms(x1, g2_ref[...]) * (1.0 + scale2) + shift2
    h2_ref[...] = _pack_bf16_pairs(h2)

    rw = rw_ref[...]
    rw_hi = rw.astype(BF16)
    rw_lo = (rw - rw_hi.astype(F32)).astype(BF16)
    h2_hi = h2.astype(BF16)
    h2_lo = (h2 - h2_hi.astype(F32)).astype(BF16)
    logits = _dot_nt(rw_hi, h2_hi) + (_dot_nt(rw_hi, h2_lo) + _dot_nt(rw_lo, h2_hi))
    logits = logits[0:N_EXPERTS] + rb_ref[...]
    tm = logits.shape[1]
    eid = lax.broadcasted_iota(jnp.int32, logits.shape, 0).astype(F32)
    score = logits
    picks = []
    onehot = jnp.zeros(logits.shape, F32)
    for _ in range(TOP_K):
        best = jnp.max(score, axis=0, keepdims=True)
        idx = jnp.min(jnp.where(score == best, eid, float(LANES)), axis=0, keepdims=True)
        hit = eid == idx
        picks.append((idx, best, hit))
        onehot = jnp.where(hit, 1.0, onehot)
        score = jnp.where(hit, LOWEST, score)
    ew = [jnp.exp(v - picks[0][1]) for _, v, _ in picks]
    den = ew[0]
    for v in ew[1:]:
        den = den + v
    inv_den = 1.0 / den

    before = _dot(onehot.astype(BF16), tri_ref[...]) + carry_sc[:, 0:1]
    carry_sc[...] = carry_sc[...] + jnp.sum(onehot, axis=1, keepdims=True)
    cnt_ref[...] = carry_sc[...]

    rows = [idx for idx, _, _ in picks]
    rows += [jnp.sum(jnp.where(hit, before, 0.0), axis=0, keepdims=True) for _, _, hit in picks]
    rows += [e * inv_den for e in ew]
    rows.append(jnp.zeros((LANES - len(rows), tm), F32))
    rt_ref[...] = jnp.concatenate(rows, axis=0).T


def _merge(o_nsa, o_lru, mg, x, mod3, norm2_g, wn, wl, wo, rw, rb, tri):
    B, S, D = x.shape
    tm = tri.shape[0]
    row = lambda w: pl.BlockSpec((None, tm, w), lambda b, s: (b, s, 0))
    consts = (norm2_g.reshape(1, D), wn, wl, wo, rw, rb, tri)
    return pl.pallas_call(
        _merge_kernel,
        grid=(B, S // tm),
        in_specs=[row(D), row(D), row(2 * D), row(D), pl.BlockSpec((None, 6, D), lambda b, s: (b, 0, 0))]
                 + [_full(a.shape) for a in consts],
        out_specs=[row(D), row(D // 2), row(LANES), _full((N_EXPERTS, LANES))],
        out_shape=[jax.ShapeDtypeStruct((B, S, D), F32), jax.ShapeDtypeStruct((B, S, D // 2), jnp.int32),
                   jax.ShapeDtypeStruct((B, S, LANES), F32), jax.ShapeDtypeStruct((N_EXPERTS, LANES), F32)],
        scratch_shapes=[pltpu.VMEM((N_EXPERTS, LANES), F32)],
        compiler_params=_params(("arbitrary", "arbitrary")),
        name="merge",
    )(o_nsa, o_lru, mg, x, mod3, *consts)


def _expert_kernel(be_ref, nb_ref, x_ref, w1_ref, b1_ref, w2_ref, b2_ref, pick_ref, y_ref, w1_sc, w2_sc):
    i = pl.program_id(0)
    ff = w2_ref.shape[0]
    chunk = pick_ref.shape[0]

    @pl.when((i == 0) | (be_ref[i] != be_ref[jnp.maximum(i - 1, 0)]))
    def _():
        for c in range(2 * ff // chunk):
            r = _dot(w1_ref[:, c * chunk:(c + 1) * chunk].astype(BF16), pick_ref[...])
            half = chunk // 2
            w1_sc[:, c * half:(c + 1) * half] = r[:, 0:half].astype(BF16)
            w1_sc[:, ff + c * half:ff + (c + 1) * half] = r[:, half:chunk].astype(BF16)
        w2_sc[...] = w2_ref[...].astype(BF16)

    @pl.when(i < nb_ref[0])
    def _():
        h = _dot(_unpack_bf16_pairs(x_ref[...]).astype(BF16), w1_sc[...]) + b1_ref[...]
        x_glu = jnp.minimum(h[:, 0:ff], SWIGLU_LIMIT)
        x_lin = jnp.clip(h[:, ff:2 * ff], -SWIGLU_LIMIT, SWIGLU_LIMIT)
        act = x_glu * _sigmoid(SWIGLU_ALPHA * x_glu) * (x_lin + 1.0)
        y_ref[...] = _pack_bf16_pairs(_dot(act.astype(BF16), w2_sc[...]) + b2_ref[...])

    @pl.when(i >= nb_ref[0])
    def _():
        y_ref[...] = jnp.zeros(y_ref.shape, jnp.int32)


def _experts(blk_e, n_used, buf, w1, b1, w2, b2):
    n_rows = buf.shape[0]
    F, D = w2.shape[1], w2.shape[2]
    tm = EXPERT_ROWS
    chunk = 2 * LANES
    pick = np.zeros((chunk, chunk), np.float32)
    pick[2 * np.arange(LANES), np.arange(LANES)] = 1.0
    pick[2 * np.arange(LANES) + 1, LANES + np.arange(LANES)] = 1.0
    wspec = lambda a, b: pl.BlockSpec((None, a, b), lambda i, be, nb: (be[i], 0, 0))
    return pl.pallas_call(
        _expert_kernel,
        grid_spec=pltpu.PrefetchScalarGridSpec(
            num_scalar_prefetch=2,
            grid=(n_rows // tm,),
            in_specs=[pl.BlockSpec((tm, D // 2), lambda i, be, nb: (i, 0)), wspec(D, 2 * F), wspec(1, 2 * F), wspec(F, D),
                      wspec(1, D), pl.BlockSpec((chunk, chunk), lambda i, be, nb: (0, 0))],
            out_specs=pl.BlockSpec((tm, D // 2), lambda i, be, nb: (i, 0)),
            scratch_shapes=[pltpu.VMEM((D, 2 * F), BF16), pltpu.VMEM((F, D), BF16)],
        ),
        out_shape=jax.ShapeDtypeStruct((n_rows, D // 2), jnp.int32),
        compiler_params=_params(("arbitrary",)),
        name="experts",
    )(blk_e, n_used, buf, w1, b1, w2, b2, jnp.asarray(pick, dtype=BF16))


def _sc_mesh():
    return plsc.VectorSubcoreMesh(core_axis_name="c", subcore_axis_name="s")


def _sc_worker():
    return lax.axis_index("s") * SC_CORES + lax.axis_index("c")


def _sc_scatter_rows(rows, dest, n_rows):
    T, W = rows.shape
    K = dest.shape[0]
    n_workers = SC_CORES * SC_SUBCORES
    per_w = T // n_workers
    ch = SC_CHUNK
    n_ch = per_w // ch
    assert per_w * n_workers == T and n_ch * ch == per_w and n_ch % 2 == 0
    dest4 = dest.reshape(K, n_workers, n_ch, ch).transpose(1, 2, 0, 3)

    @functools.partial(
        pl.kernel, mesh=_sc_mesh(), out_type=jax.ShapeDtypeStruct((n_rows, W), rows.dtype),
        scratch_types=[pltpu.VMEM((n_ch, K, ch), jnp.int32), pltpu.VMEM((2, ch, W), rows.dtype),
                       pltpu.SemaphoreType.DMA((2,)), pltpu.SemaphoreType.DMA((2,))])
    def scatter(rows_hbm, dest_hbm, out_hbm, idx_v, rows_v, read_sem, write_sem):
        wid = _sc_worker()
        base = wid * per_w
        pltpu.sync_copy(dest_hbm.at[wid], idx_v)

        def read(c, b):
            src = rows_hbm.at[pl.ds(pl.multiple_of(base + c * ch, ch), ch)]
            return pltpu.make_async_copy(src, rows_v.at[b], read_sem.at[b])

        def write(c, b, k):
            return pltpu.make_async_copy(rows_v.at[b], out_hbm.at[idx_v.at[c, k]], write_sem.at[b])

        read(0, 0).start()

        @pl.loop(0, n_ch, step=2)
        def _(i):
            for b in range(2):
                c = i + b
                read(c, b).wait()

                @pl.when(c >= 1)
                def _():
                    for k in range(K):
                        write(c - 1, 1 - b, k).wait()

                @pl.when(c + 1 < n_ch)
                def _():
                    read(c + 1, 1 - b).start()

                for k in range(K):
                    write(c, b, k).start()

        for k in range(K):
            write(n_ch - 1, 1, k).wait()

    return scatter(rows, dest4)


def _sc_gather_rows(table, idx):
    N = idx.shape[0]
    W = table.shape[1]
    n_workers = SC_CORES * SC_SUBCORES
    per_w = N // n_workers
    ch = SC_CHUNK
    n_ch = per_w // ch
    assert per_w * n_workers == N and n_ch * ch == per_w and n_ch % 2 == 0
    idx3 = idx.reshape(n_workers, n_ch, ch)

    @functools.partial(
        pl.kernel, mesh=_sc_mesh(), out_type=jax.ShapeDtypeStruct((N, W), table.dtype),
        scratch_types=[pltpu.VMEM((n_ch, ch), jnp.int32), pltpu.VMEM((2, ch, W), table.dtype),
                       pltpu.SemaphoreType.DMA((2,)), pltpu.SemaphoreType.DMA((2,))])
    def gather(table_hbm, idx_hbm, out_hbm, idx_v, rows_v, gather_sem, write_sem):
        wid = _sc_worker()
        base = wid * per_w
        pltpu.sync_copy(idx_hbm.at[wid], idx_v)

        def fetch(c, b):
            return pltpu.make_async_copy(table_hbm.at[idx_v.at[c]], rows_v.at[b], gather_sem.at[b])

        def write(c, b):
            dst = out_hbm.at[pl.ds(pl.multiple_of(base + c * ch, ch), ch)]
            return pltpu.make_async_copy(rows_v.at[b], dst, write_sem.at[b])

        fetch(0, 0).start()

        @pl.loop(0, n_ch, step=2)
        def _(i):
            for b in range(2):
                c = i + b
                fetch(c, b).wait()

                @pl.when(c >= 1)
                def _():
                    write(c - 1, 1 - b).wait()

                @pl.when(c + 1 < n_ch)
                def _():
                    fetch(c + 1, 1 - b).start()

                write(c, b).start()

        write(n_ch - 1, 1).wait()

    return gather(table, idx3)


def _combine_kernel(x1_ref, yg_ref, rt_ref, mod_ref, *rest):
    o_ref = rest[-1]
    gate2 = mod_ref[5:6, :]
    rt = rt_ref[...]
    acc = rt[:, 2 * TOP_K:2 * TOP_K + 1] * _unpack_bf16_pairs(yg_ref[0])
    for k in range(1, TOP_K):
        acc = acc + rt[:, 2 * TOP_K + k:2 * TOP_K + k + 1] * _unpack_bf16_pairs(yg_ref[k])
    o_ref[...] = x1_ref[...] + gate2 * acc


def _combine(x1, yg, rt, mod3, prev, b0):
    B, S, D = x1.shape
    nb = yg.shape[1]
    tm = min(COMBINE_ROWS, S)
    row = lambda w: pl.BlockSpec((None, tm, w), lambda b, s: (b0 + b, s, 0))
    in_specs = [row(D), pl.BlockSpec((TOP_K, None, tm, D // 2), lambda b, s: (0, b, s, 0)), row(LANES),
                pl.BlockSpec((None, 6, D), lambda b, s: (b0 + b, 0, 0))]
    operands = [x1, yg, rt, mod3]
    if prev is not None:
        in_specs.append(pl.BlockSpec(memory_space=pl.ANY))
        operands.append(prev)
    return pl.pallas_call(
        _combine_kernel,
        grid=(nb, S // tm),
        in_specs=in_specs,
        out_specs=row(D),
        out_shape=jax.ShapeDtypeStruct((B, S, D), F32),
        input_output_aliases={} if prev is None else {4: 0},
        compiler_params=_params(("arbitrary", "arbitrary")),
        name="combine",
    )(*operands)


def _overlap_matrix(n_cmp_pad, n_blk):
    cs = np.arange(n_cmp_pad)[:, None] * CMP_STRIDE
    js = np.arange(LANES)[None, :] * SEL_BLOCK
    m = (cs <= js + SEL_BLOCK - 1) & (cs + CMP_BLOCK - 1 >= js) & (np.arange(LANES)[None, :] < n_blk)
    return jnp.asarray(m.astype(np.float32).T)


def _block_onehot(seq):
    key = np.arange(seq)[:, None]
    j = np.arange(LANES)[None, :]
    return jnp.asarray((j == key // SEL_BLOCK).astype(np.float32), dtype=BF16)


def _layer(x, c, ang, ada_w, ada_b, norm1_g, w_in, q_norm_g, k_norm_g, cmp_pe_k, cmp_pe_v, cmp_wk1, cmp_wk2, cmp_wv1,
           cmp_wv2, conv_w, conv_b, lru_wa, lru_ba, lru_wx, lru_bx, lru_lambda, w_branch_out, w_out, norm2_g,
           router_w, router_b, moe_w1, moe_b1, moe_w2, moe_b2):
    B, S, D = x.shape
    T = B * S
    G = N_KV_GROUPS
    nsa_w = N_HEADS * HEAD_DIM
    kv_w = G * HEAD_DIM

    mod3 = _ada(c, ada_w, ada_b).reshape(B, 6, D)

    o = 0
    wq = w_in[:, o:o + nsa_w]; o += nsa_w
    wkc = w_in[:, o:o + 2 * kv_w]; o += 2 * kv_w
    wkr = w_in[:, o:o + 4 * kv_w]; o += 4 * kv_w
    n_gate = N_HEADS * N_NSA_BRANCHES
    wg_raw = w_in[:, o:o + n_gate].reshape(D, G, n_gate // G); o += n_gate
    wg = jnp.pad(wg_raw, ((0, 0), (0, 0), (0, LANES - n_gate // G))).reshape(D, G * LANES)
    wlx = w_in[:, o:o + D]; o += D
    wlg = w_in[:, o:o + D]; o += D
    wm = w_in[:, o:o + 2 * D]
    bf = lambda a: a.astype(BF16)

    qn, qr, kvc, kvr, gates, lx, lg, mg = _in_proj(x, mod3, norm1_g, ang, q_norm_g, k_norm_g, bf(wq), bf(wkc), bf(wkr),
                                                   bf(wg), bf(wlx), bf(wlg), bf(wm))

    nc = S // CMP_STRIDE
    pe = lambda p: jnp.broadcast_to(p.reshape(1, CMP_BLOCK * HEAD_DIM), (SUBLANES, CMP_BLOCK * HEAD_DIM)).astype(BF16)
    kc, vc = _compress(kvc, pe(cmp_pe_k), pe(cmp_pe_v), bf(cmp_wk1), bf(cmp_wk2), bf(cmp_wv1), bf(cmp_wv2), k_norm_g)

    ocmp, bias = _select(qn, kc, vc, gates, _overlap_matrix(nc, S // SEL_BLOCK), S // SEL_BLOCK)
    o_nsa = _nsa(qr, bias, ocmp, kvr, gates, _block_onehot(S))
    o_lru = _lru(lx, lg, conv_w, conv_b, bf(lru_wa), lru_ba, bf(lru_wx), lru_bx, lru_lambda)

    tm = min(IN_ROWS, S)
    tri = jnp.asarray(np.triu(np.ones((tm, tm), np.float32), 1), dtype=BF16)
    rw = jnp.pad(router_w.T, ((0, LANES - N_EXPERTS), (0, 0)))
    rb = router_b.reshape(N_EXPERTS, 1)
    x1, h2, rt, cnt = _merge(o_nsa, o_lru, mg, x, mod3, norm2_g, bf(w_branch_out[:nsa_w]), bf(w_branch_out[nsa_w:]),
                             bf(w_out), rw, rb, tri)

    rt2 = rt.reshape(T, LANES)
    top_e = rt2[:, 0:TOP_K].astype(jnp.int32)
    rank = rt2[:, TOP_K:2 * TOP_K].astype(jnp.int32)
    counts = cnt[:, 0].astype(jnp.int32)
    padded = (counts + EXPERT_ROWS - 1) // EXPERT_ROWS * EXPERT_ROWS
    ends = jnp.cumsum(padded)
    start = ends - padded
    dest = start[top_e] + rank
    n_rows = -(-(T * TOP_K + N_EXPERTS * (EXPERT_ROWS - 1)) // EXPERT_ROWS) * EXPERT_ROWS
    n_blocks = n_rows // EXPERT_ROWS
    blk_first = jnp.arange(n_blocks, dtype=jnp.int32) * EXPERT_ROWS
    blk_e = jnp.minimum(jnp.sum(ends[None, :] <= blk_first[:, None], axis=1), N_EXPERTS - 1).astype(jnp.int32)
    n_used = (ends[-1:] // EXPERT_ROWS).astype(jnp.int32)

    dest_kt = dest.T
    buf = _sc_scatter_rows(h2.reshape(T, D // 2), dest_kt, n_rows)

    b1 = jnp.concatenate([moe_b1[:, 0::2], moe_b1[:, 1::2]], axis=1)
    ybuf = _experts(blk_e, n_used, buf, moe_w1, b1.reshape(N_EXPERTS, 1, -1), moe_w2, moe_b2.reshape(N_EXPERTS, 1, D))
    nb = B // COMBINE_PARTS if B % COMBINE_PARTS == 0 else B
    dest_parts = dest_kt.reshape(TOP_K, B // nb, nb * S)
    out = None
    for p in range(B // nb):
        yg = _sc_gather_rows(ybuf, dest_parts[:, p].reshape(-1)).reshape(TOP_K, nb, S, D // 2)
        out = _combine(x1, yg, rt, mod3, out, p * nb)
    return out


def kernel(x, c, positions, ada_w, ada_b, norm1_g, w_in, q_norm_g, k_norm_g, cmp_pe_k, cmp_pe_v, cmp_wk1, cmp_wk2, cmp_wv1, cmp_wv2, conv_w, conv_b, lru_wa, lru_ba, lru_wx, lru_bx, lru_lambda, w_branch_out, w_out, norm2_g, router_w, router_b, moe_w1, moe_b1, moe_w2, moe_b2):
    inv = ROPE_THETA ** (-jnp.arange(0, HEAD_DIM, 2, dtype=F32) / HEAD_DIM)
    ang = positions.astype(F32)[..., None] * inv
    ang = jnp.concatenate([ang, ang], axis=-1)
    for l in range(ada_w.shape[0]):
        x = _layer(x, c, ang, ada_w[l], ada_b[l], norm1_g[l], w_in[l], q_norm_g[l], k_norm_g[l], cmp_pe_k[l],
                   cmp_pe_v[l], cmp_wk1[l], cmp_wk2[l], cmp_wv1[l], cmp_wv2[l], conv_w[l], conv_b[l], lru_wa[l],
                   lru_ba[l], lru_wx[l], lru_bx[l], lru_lambda[l], w_branch_out[l], w_out[l], norm2_g[l], router_w[l],
                   router_b[l], moe_w1[l], moe_b1[l], moe_w2[l], moe_b2[l])
    return x
```

```python
import functools
import math

import jax
import jax.numpy as jnp
import numpy as np
from jax import lax
from jax.experimental import pallas as pl
from jax.experimental.pallas import tpu as pltpu
from jax.experimental.pallas import tpu_sc as plsc

F32 = jnp.float32
BF16 = jnp.bfloat16

N_HEADS = 8
HEAD_DIM = 128
N_KV_GROUPS = 2
HEADS_PER_GROUP = N_HEADS // N_KV_GROUPS
N_NSA_BRANCHES = 3
CMP_BLOCK = 32
CMP_STRIDE = 16
CMP_HIDDEN = 256
SEL_BLOCK = 64
N_SELECT = 8
WINDOW = 512
ROPE_THETA = 10000.0
LRU_BLOCKS = 4
CONV_WIDTH = 4
LRU_C = 8.0
N_EXPERTS = 32
TOP_K = 4
SWIGLU_LIMIT = 7.0
SWIGLU_ALPHA = 1.702
RMS_EPS = 1e-6
NEG_INF = -1e30
LOWEST = -3.0e38
LOG2_E = 1.4426950408889634

LANES = 128
SUBLANES = 8
VMEM_LIMIT = 56 * 1024 * 1024
SC_CORES = 2
SC_SUBCORES = 16
SC_CHUNK = 64

Q_TILE = 256
SELECT_ROWS = 1024
KEY_TILE = 128
ONES_ROWS = 16
HEAD_PAIR = 2
IN_ROWS = 512
MERGE_SPLIT = 4
LRU_ROWS = 512
EXPERT_ROWS = 512
COMBINE_ROWS = 256
COMBINE_PARTS = 8


def _sigmoid(v):
    return 0.5 * jnp.tanh(0.5 * v) + 0.5


def _gelu_tanh(v):
    c = math.sqrt(2.0 / math.pi)
    half = 0.5 * v
    return half + half * jnp.tanh(v * (c + (c * 0.044715) * (v * v)))


def _rms(v, g):
    return v * lax.rsqrt(jnp.mean(v * v, axis=-1, keepdims=True) + RMS_EPS) * g


def _dot(a, b, **kw):
    return jnp.dot(a, b, preferred_element_type=F32, **kw)


def _dot_nt(a, b, **kw):
    return lax.dot_general(a, b, (((1,), (1,)), ((), ())), preferred_element_type=F32, **kw)


def _pack_bf16_pairs(v):
    n = v.shape[1] // 2
    lo = lax.bitcast_convert_type(v[:, 0:n].astype(BF16).astype(F32), jnp.int32)
    hi = lax.bitcast_convert_type(v[:, n:2 * n].astype(BF16).astype(F32), jnp.int32)
    return lax.shift_right_logical(lo, 16) | hi


def _unpack_bf16_pairs(w):
    lo = lax.bitcast_convert_type(lax.shift_left(w, 16), F32)
    hi = lax.bitcast_convert_type(w & jnp.int32(-65536), F32)
    return jnp.concatenate([lo, hi], axis=1)


def _full(shape):
    nd = len(shape)
    return pl.BlockSpec(shape, lambda *_: (0,) * nd)


def _params(sem):
    return pltpu.CompilerParams(dimension_semantics=sem, vmem_limit_bytes=VMEM_LIMIT)


def _ada_kernel(c_ref, w_ref, b_ref, o_ref):
    c = c_ref[...]
    o_ref[...] = _dot(c * _sigmoid(c), w_ref[...], precision=lax.Precision.HIGHEST) + b_ref[...]


def _ada(c, ada_w, ada_b):
    B, D = c.shape
    N = ada_w.shape[1]
    return pl.pallas_call(
        _ada_kernel,
        grid=(N // D,),
        in_specs=[_full((B, D)), pl.BlockSpec((D, D), lambda j: (0, j)), pl.BlockSpec((1, D), lambda j: (0, j))],
        out_specs=pl.BlockSpec((B, D), lambda j: (0, j)),
        out_shape=jax.ShapeDtypeStruct((B, N), F32),
        compiler_params=_params(("arbitrary",)),
        name="ada",
    )(c, ada_w, ada_b.reshape(1, N))


def _in_kernel(x_ref, mod_ref, g1_ref, ang_ref, qg_ref, kg_ref, wq_ref, wkc_ref, wkr_ref, wg_ref, wlx_ref, wlg_ref,
               wm_ref, qn_ref, qr_ref, kvc_ref, kvr_ref, gt_ref, lx_ref, lg_ref, mg_ref, kvc_sc):
    x = x_ref[...]
    shift1 = mod_ref[0:1, :]
    scale1 = mod_ref[1:2, :]
    h = _rms(x, g1_ref[...]) * (1.0 + scale1) + shift1
    hb = h.astype(BF16)

    ang = ang_ref[...]
    cos = jnp.cos(ang)
    sin = jnp.sin(ang)
    lane = lax.broadcasted_iota(jnp.int32, ang.shape, 1)
    sin_signed = jnp.where(lane < HEAD_DIM // 2, -sin, sin)

    def rope(v):
        return v * cos + pltpu.roll(v, HEAD_DIM // 2, 1) * sin_signed

    q = _dot(hb, wq_ref[...])
    for hh in range(N_HEADS):
        sl = slice(hh * HEAD_DIM, (hh + 1) * HEAD_DIM)
        qh = _rms(q[:, sl], qg_ref[...])
        qn_ref[:, sl] = (qh * HEAD_DIM ** -0.5).astype(BF16)
        qr_ref[:, sl] = (rope(qh) * (HEAD_DIM ** -0.5 * LOG2_E)).astype(BF16)

    kvc = _dot(hb, wkc_ref[...])
    for part in range(kvc_ref.shape[0]):
        kvc_sc[part] = kvc[:, part * HEAD_DIM:(part + 1) * HEAD_DIM]
        for tok in range(CMP_STRIDE):
            piece = kvc_sc[part, pl.ds(tok, kvc_ref.shape[1], stride=CMP_STRIDE), :]
            kvc_ref[part, :, tok * HEAD_DIM:(tok + 1) * HEAD_DIM] = piece.astype(BF16)

    kvr = _dot(hb, wkr_ref[...])
    kvw = N_KV_GROUPS * HEAD_DIM
    for part in range(4):
        for gg in range(N_KV_GROUPS):
            sl = slice(part * kvw + gg * HEAD_DIM, part * kvw + (gg + 1) * HEAD_DIM)
            v = kvr[:, sl]
            if part % 2 == 0:
                row = 1 + part // 2
                v = rope(_rms(v, kg_ref[row:row + 1, :]))
            kvr_ref[:, sl] = v.astype(BF16)

    gt_ref[...] = _sigmoid(_dot(hb, wg_ref[...]))
    lx_ref[...] = _dot(hb, wlx_ref[...]).astype(BF16)
    lg_ref[...] = _dot(hb, wlg_ref[...]).astype(BF16)
    mg_ref[...] = _sigmoid(_dot(hb, wm_ref[...])).astype(BF16)


def _in_proj(x, mod3, norm1_g, ang, q_norm_g, k_norm_g, wq, wkc, wkr, wg, wlx, wlg, wm):
    B, S, D = x.shape
    tm = min(IN_ROWS, S)
    row = lambda w: pl.BlockSpec((None, tm, w), lambda b, s: (b, s, 0))
    widths = (wq.shape[1], wq.shape[1], wkc.shape[1], wkr.shape[1], wg.shape[1], wlx.shape[1], wlg.shape[1], wm.shape[1])
    dtypes = (BF16, BF16, BF16, BF16, F32, BF16, BF16, BF16)
    weights = (wq, wkc, wkr, wg, wlx, wlg, wm)
    out_specs = [row(w) for w in widths]
    out_shape = [jax.ShapeDtypeStruct((B, S, w), dt) for w, dt in zip(widths, dtypes)]
    n_part = wkc.shape[1] // HEAD_DIM
    out_specs[2] = pl.BlockSpec((None, n_part, tm // CMP_STRIDE, CMP_STRIDE * HEAD_DIM), lambda b, s: (b, 0, s, 0))
    out_shape[2] = jax.ShapeDtypeStruct((B, n_part, S // CMP_STRIDE, CMP_STRIDE * HEAD_DIM), BF16)
    return pl.pallas_call(
        _in_kernel,
        grid=(B, S // tm),
        in_specs=[row(D), pl.BlockSpec((None, 6, D), lambda b, s: (b, 0, 0)), _full((1, D)), row(HEAD_DIM),
                  _full((1, HEAD_DIM)), _full(k_norm_g.shape)] + [_full(w.shape) for w in weights],
        out_specs=out_specs,
        out_shape=out_shape,
        scratch_shapes=[pltpu.VMEM((n_part, tm, HEAD_DIM), F32)],
        compiler_params=_params(("arbitrary", "arbitrary")),
        name="in_proj",
    )(x, mod3, norm1_g.reshape(1, D), ang, q_norm_g.reshape(1, HEAD_DIM), k_norm_g, *weights)


def _cmp_kernel(ak_ref, av_ref, pek_ref, pev_ref, wk1_ref, wk2_ref, wv1_ref, wv2_ref, kg_ref, kc_ref, vc_ref):
    half = CMP_STRIDE * HEAD_DIM

    def compress(a_ref, pe_ref, w1_ref, w2_ref):
        a = a_ref[...]
        u = _dot(a, w1_ref[0:half, :])
        v = _dot(a, w1_ref[half:2 * half, :])
        pw = _dot(pe_ref[...], w1_ref[...])
        pre = u + pltpu.roll(v, v.shape[0] - 1, 0) + pw[0:1, :]
        return _dot(_gelu_tanh(pre).astype(BF16), w2_ref[...])

    kc_ref[...] = _rms(compress(ak_ref, pek_ref, wk1_ref, wk2_ref), kg_ref[0:1, :]).astype(BF16)
    vc_ref[...] = compress(av_ref, pev_ref, wv1_ref, wv2_ref).T.astype(BF16)


def _compress(a, pe_k, pe_v, wk1, wk2, wv1, wv2, k_norm_g):
    B, _, NC, W = a.shape
    G = N_KV_GROUPS
    out = pl.BlockSpec((None, None, NC, HEAD_DIM), lambda b, g: (b, g, 0, 0))
    consts = (pe_k, pe_v, wk1, wk2, wv1, wv2, k_norm_g)
    return pl.pallas_call(
        _cmp_kernel,
        grid=(B, G),
        in_specs=[pl.BlockSpec((None, None, NC, W), lambda b, g: (b, g, 0, 0)),
                  pl.BlockSpec((None, None, NC, W), lambda b, g: (b, G + g, 0, 0))] + [_full(c.shape) for c in consts],
        out_specs=[out, out],
        out_shape=[jax.ShapeDtypeStruct((B, G, NC, HEAD_DIM), BF16)] * 2,
        compiler_params=_params(("arbitrary", "arbitrary")),
        name="compress",
    )(a, a, *consts)


def _select_kernel(qn_ref, kc_ref, vc_ref, gt_ref, ov_ref, ocmp_ref, bias_ref, *, n_blk):
    qi = pl.program_id(2)
    tq = qn_ref.shape[0]
    hpg = HEADS_PER_GROUP
    qn = jnp.concatenate([qn_ref[:, h * HEAD_DIM:(h + 1) * HEAD_DIM] for h in range(hpg)], axis=0)
    row = lax.broadcasted_iota(jnp.int32, (kc_ref.shape[0], tq), 0)
    t_q = qi * tq + lax.broadcasted_iota(jnp.int32, (kc_ref.shape[0], tq), 1)

    cmask = jnp.concatenate([row * CMP_STRIDE + (CMP_BLOCK - 1) <= t_q] * hpg, axis=1)
    s = jnp.where(cmask, _dot_nt(kc_ref[...], qn), NEG_INF)
    e = jnp.exp(s - jnp.max(s, axis=0, keepdims=True))
    p = e * jnp.where(cmask[0:1, :], 1.0 / jnp.sum(e, axis=0, keepdims=True), 0.0)
    o_cmp = _dot(vc_ref[...], p.astype(BF16))
    gt = gt_ref[...]
    for h in range(hpg):
        c0 = h * N_NSA_BRANCHES
        ocmp_ref[:, h * HEAD_DIM:(h + 1) * HEAD_DIM] = (gt[:, c0:c0 + 1] * o_cmp[:, h * tq:(h + 1) * tq].T).astype(BF16)

    psum = p[:, 0:tq]
    for h in range(1, hpg):
        psum = psum + p[:, h * tq:(h + 1) * tq]
    imp = _dot(ov_ref[...], psum, precision=lax.Precision.HIGHEST)[0:n_blk]
    blk = lax.broadcasted_iota(jnp.int32, (n_blk, tq), 0).astype(F32)
    cur = ((qi * tq + lax.broadcasted_iota(jnp.int32, (n_blk, tq), 1)) // SEL_BLOCK).astype(F32)
    forced = (blk == 0.0) | (blk == cur) | (blk == cur - 1.0)
    score = jnp.where(forced, 1e6, jnp.where(blk <= cur, imp, -1e6))
    bias = jnp.full((n_blk, tq), NEG_INF, F32)
    for _ in range(min(N_SELECT, n_blk)):
        best = jnp.max(score, axis=0, keepdims=True)
        idx = jnp.min(jnp.where(score == best, blk, float(LANES)), axis=0, keepdims=True)
        hit = blk == idx
        bias = jnp.where(hit, 0.0, bias)
        score = jnp.where(hit, LOWEST, score)
    bias = jnp.concatenate([bias, jnp.zeros((LANES - n_blk, tq), F32)], axis=0)
    bias_ref[...] = bias.T.astype(BF16)


def _select(qn, kc, vc, gates, overlap, n_blk):
    B, S, _ = qn.shape
    G = N_KV_GROUPS
    tq = min(SELECT_ROWS, S)
    gw = HEADS_PER_GROUP * HEAD_DIM
    qspec = pl.BlockSpec((None, tq, gw), lambda b, g, i: (b, i, g))
    lane_spec = pl.BlockSpec((None, tq, LANES), lambda b, g, i: (b, i, g))
    cspec = pl.BlockSpec((None, None) + kc.shape[2:], lambda b, g, i: (b, g, 0, 0))
    return pl.pallas_call(
        functools.partial(_select_kernel, n_blk=n_blk),
        grid=(B, G, S // tq),
        in_specs=[qspec, cspec, cspec, lane_spec, _full(overlap.shape)],
        out_specs=[qspec, lane_spec],
        out_shape=[jax.ShapeDtypeStruct(qn.shape, BF16), jax.ShapeDtypeStruct((B, S, G * LANES), BF16)],
        compiler_params=_params(("arbitrary", "arbitrary", "arbitrary")),
        name="select",
    )(qn, kc, vc, gates, overlap)


def _nsa_kernel(qr_ref, bias_ref, ocmp_ref, ks_ref, vs_ref, kw_ref, vw_ref, gt_ref, ex_ref, o_ref, vst_sc, vwt_sc):
    qi = pl.program_id(2)
    tq = qr_ref.shape[0]
    hpg = HEADS_PER_GROUP
    seq = ks_ref.shape[0]

    @pl.when(qi == 0)
    def _():
        for kt in range(seq // KEY_TILE):
            rows = slice(kt * KEY_TILE, (kt + 1) * KEY_TILE)
            vst_sc[0:HEAD_DIM, rows] = vs_ref[rows, :].astype(F32).T.astype(BF16)
            vwt_sc[0:HEAD_DIM, rows] = vw_ref[rows, :].astype(F32).T.astype(BF16)
        ones = jnp.ones((vst_sc.shape[0] - HEAD_DIM, seq), BF16)
        vst_sc[HEAD_DIM:, :] = ones
        vwt_sc[HEAD_DIM:, :] = ones

    def pair(v):
        return jnp.concatenate([v] * HEAD_PAIR, axis=1)

    qr = jnp.concatenate([qr_ref[:, h * HEAD_DIM:(h + 1) * HEAD_DIM] for h in range(hpg)], axis=0)
    q_aug = jnp.concatenate([qr, jnp.concatenate([bias_ref[...]] * hpg, axis=0)], axis=1)
    pair_cols = [slice(hp * HEAD_PAIR * tq, (hp + 1) * HEAD_PAIR * tq) for hp in range(hpg // HEAD_PAIR)]

    def softmax_pv(sc, vt):
        pr = jnp.exp2((sc - jnp.max(sc, axis=0, keepdims=True)).astype(BF16))
        o = _dot(vt, pr)
        return o[0:HEAD_DIM] * (1.0 / o[HEAD_DIM:HEAD_DIM + 1])

    def biased(sc, first_bias, last_bias):
        n = sc.shape[0]
        if first_bias is not None:
            sc = jnp.concatenate([sc[0:tq] + first_bias, sc[tq:n]], axis=0)
        return jnp.concatenate([sc[0:n - tq], sc[n - tq:n] + last_bias], axis=0) if n > tq else sc + last_bias

    r_in = lax.broadcasted_iota(jnp.int32, (tq, tq), 0)
    c_in = lax.broadcasted_iota(jnp.int32, (tq, tq), 1)
    causal = pair(jnp.where(r_in <= c_in, 0.0, NEG_INF))
    window_tail = pair(jnp.where(r_in > c_in, 0.0, NEG_INF))
    gt = gt_ref[...].T

    for v in range(seq // tq):
        @pl.when(qi == v)
        def _():
            ext = (v + 1) * tq
            k_aug = jnp.concatenate([ks_ref[0:ext, :], ex_ref[0:ext, :]], axis=1)
            w_lo = max(v * tq - WINDOW, 0)
            tail = window_tail if v * tq - WINDOW >= 0 else None
            k_win = kw_ref[w_lo:ext, :]
            scores = [biased(_dot_nt(k_aug, q_aug[cols]), None, causal) for cols in pair_cols]
            win_scores = [biased(_dot_nt(k_win, qr[cols]), tail, causal) for cols in pair_cols]
            o_slcs = [softmax_pv(sc, vst_sc[:, 0:ext]) for sc in scores]
            o_wins = [softmax_pv(sc, vwt_sc[:, w_lo:ext]) for sc in win_scores]
            for h in range(hpg):
                hp, j = divmod(h, HEAD_PAIR)
                sub = slice(j * tq, (j + 1) * tq)
                c0 = h * N_NSA_BRANCHES
                o = gt[c0 + 1:c0 + 2, :] * o_slcs[hp][:, sub] + gt[c0 + 2:c0 + 3, :] * o_wins[hp][:, sub]
                hd = slice(h * HEAD_DIM, (h + 1) * HEAD_DIM)
                o_ref[:, hd] = (o.T + ocmp_ref[:, hd].astype(F32)).astype(BF16)


def _nsa(qr, bias, ocmp, kvr, gates, onehot):
    B, S, _ = qr.shape
    G = N_KV_GROUPS
    tq = Q_TILE
    gw = HEADS_PER_GROUP * HEAD_DIM
    qspec = pl.BlockSpec((None, tq, gw), lambda b, g, i: (b, i, g))
    lane_spec = pl.BlockSpec((None, tq, LANES), lambda b, g, i: (b, i, g))
    kv = lambda part: pl.BlockSpec((None, S, HEAD_DIM), lambda b, g, i: (b, 0, part * G + g))
    return pl.pallas_call(
        _nsa_kernel,
        grid=(B, G, S // tq),
        in_specs=[qspec, lane_spec, qspec, kv(0), kv(1), kv(2), kv(3), lane_spec, _full(onehot.shape)],
        out_specs=qspec,
        out_shape=jax.ShapeDtypeStruct(qr.shape, BF16),
        scratch_shapes=[pltpu.VMEM((HEAD_DIM + ONES_ROWS, S), BF16), pltpu.VMEM((HEAD_DIM + ONES_ROWS, S), BF16)],
        compiler_params=_params(("arbitrary", "arbitrary", "arbitrary")),
        name="nsa",
    )(qr, bias, ocmp, kvr, kvr, kvr, kvr, gates, onehot)


def _lru_kernel(lx_ref, lg_ref, cw_ref, cb_ref, wa_ref, ba_ref, wx_ref, bx_ref, lam_ref, o_ref,
                xs_sc, a_sc, u_sc, h_sc, carry_sc):
    ts = lx_ref.shape[0]
    width = lx_ref.shape[1]
    bw = width // LRU_BLOCKS

    @pl.when(pl.program_id(1) == 0)
    def _():
        xs_sc[0:SUBLANES, :] = jnp.zeros((SUBLANES, width), F32)
        carry_sc[...] = jnp.zeros(carry_sc.shape, F32)

    xs_sc[SUBLANES:SUBLANES + ts, :] = lx_ref[...].astype(F32)
    xc = cb_ref[...] + cw_ref[CONV_WIDTH - 1:CONV_WIDTH, :] * xs_sc[SUBLANES:SUBLANES + ts, :]
    for d in range(1, CONV_WIDTH):
        w = cw_ref[CONV_WIDTH - 1 - d:CONV_WIDTH - d, :]
        xc = xc + w * xs_sc[SUBLANES - d:SUBLANES - d + ts, :]
    xs_sc[0:SUBLANES, :] = xs_sc[ts:ts + SUBLANES, :]

    lam = -lam_ref[...]
    neg_c_softplus = -LRU_C * (jnp.maximum(lam, 0.0) + jnp.log1p(jnp.exp(-jnp.abs(lam))))
    xcb = xc.astype(BF16)
    for blk in range(LRU_BLOCKS):
        sl = slice(blk * bw, (blk + 1) * bw)
        r = _sigmoid(_dot(xcb[:, sl], wa_ref[blk]) + ba_ref[:, sl])
        i = _sigmoid(_dot(xcb[:, sl], wx_ref[blk]) + bx_ref[:, sl])
        log_a = r * neg_c_softplus[:, sl]
        a = jnp.exp(log_a)
        a_sc[:, sl] = a
        u_sc[:, sl] = jnp.sqrt(-jnp.tanh(log_a) * (a * a + 1.0)) * (i * xc[:, sl])

    row = lax.broadcasted_iota(jnp.int32, (SUBLANES, width), 0)

    def chunk(c, h_prev):
        r0 = pl.multiple_of(c * SUBLANES, SUBLANES)
        a = a_sc[pl.ds(r0, SUBLANES), :]
        u = u_sc[pl.ds(r0, SUBLANES), :]
        for d in (1, 2, 4):
            keep = row >= d
            u = jnp.where(keep, a * pltpu.roll(u, d, 0) + u, u)
            a = jnp.where(keep, a * pltpu.roll(a, d, 0), a)
        h = a * h_prev + u
        h_sc[pl.ds(r0, SUBLANES), :] = h
        return jnp.broadcast_to(h[SUBLANES - 1:SUBLANES, :], h.shape)

    carry_sc[...] = lax.fori_loop(0, ts // SUBLANES, chunk, carry_sc[...], unroll=2)
    o_ref[...] = (_gelu_tanh(lg_ref[...].astype(F32)) * h_sc[...]).astype(BF16)


def _lru(lx, lg, conv_w, conv_b, wa, ba, wx, bx, lam):
    B, S, W = lx.shape
    ts = min(LRU_ROWS, S)
    row = pl.BlockSpec((None, ts, W), lambda b, s: (b, s, 0))
    consts = (conv_w, conv_b.reshape(1, W), wa, ba.reshape(1, W), wx, bx.reshape(1, W), lam.reshape(1, W))
    return pl.pallas_call(
        _lru_kernel,
        grid=(B, S // ts),
        in_specs=[row, row] + [_full(a.shape) for a in consts],
        out_specs=row,
        out_shape=jax.ShapeDtypeStruct((B, S, W), BF16),
        scratch_shapes=[pltpu.VMEM((ts + SUBLANES, W), F32), pltpu.VMEM((ts, W), F32), pltpu.VMEM((ts, W), F32),
                        pltpu.VMEM((ts, W), F32), pltpu.VMEM((SUBLANES, W), F32)],
        compiler_params=_params(("arbitrary", "arbitrary")),
        name="lru",
    )(lx, lg, *consts)


def _merge_kernel(on_ref, ol_ref, mg_ref, x_ref, mod_ref, g2_ref, wn_ref, wl_ref, wo_ref, rw_ref, rb_ref, tri_ref,
                  x1_ref, h2_ref, rt_ref, cnt_ref, carry_sc):
    first = (pl.program_id(0) == 0) & (pl.program_id(1) == 0)

    @pl.when(first)
    def _():
        carry_sc[...] = jnp.zeros(carry_sc.shape, F32)

    d = x_ref.shape[1]
    gate1 = mod_ref[2:3, :]
    shift2 = mod_ref[3:4, :]
    scale2 = mod_ref[4:5, :]
    rw = rw_ref[...]
    rw_hi = rw.astype(BF16)
    rw_lo = (rw - rw_hi.astype(F32)).astype(BF16)

    n_rows = x_ref.shape[0] // MERGE_SPLIT
    groups = [slice(i * n_rows, (i + 1) * n_rows) for i in range(MERGE_SPLIT)]
    y_nsa = [_dot(on_ref[r, :], wn_ref[...]) for r in groups]
    y_lru = [_dot(ol_ref[r, :], wl_ref[...]) for r in groups]
    merged = [mg_ref[r, 0:d].astype(F32) * yn + mg_ref[r, d:2 * d].astype(F32) * yl
              for r, yn, yl in zip(groups, y_nsa, y_lru)]
    x1 = [x_ref[r, :] + gate1 * _dot(m.astype(BF16), wo_ref[...]) for r, m in zip(groups, merged)]
    h2 = [_rms(v, g2_ref[...]) * (1.0 + scale2) + shift2 for v in x1]
    logit_parts = []
    for r, v1, v2 in zip(groups, x1, h2):
        x1_ref[r, :] = v1
        h2_ref[r, :] = _pack_bf16_pairs(v2)
        h2_hi = v2.astype(BF16)
        h2_lo = (v2 - h2_hi.astype(F32)).astype(BF16)
        logit_parts.append(_dot_nt(rw_hi, h2_hi) + (_dot_nt(rw_hi, h2_lo) + _dot_nt(rw_lo, h2_hi)))
    logits = jnp.concatenate(logit_parts, axis=1)[0:N_EXPERTS] + rb_ref[...]
    tm = logits.shape[1]
    eid = lax.broadcasted_iota(jnp.int32, logits.shape, 0).astype(F32)
    score = logits
    picks = []
    onehot = jnp.zeros(logits.shape, F32)
    for _ in range(TOP_K):
        best = jnp.max(score, axis=0, keepdims=True)
        idx = jnp.min(jnp.where(score == best, eid, float(LANES)), axis=0, keepdims=True)
        hit = eid == idx
        picks.append((idx, best, hit))
        onehot = jnp.where(hit, 1.0, onehot)
        score = jnp.where(hit, LOWEST, score)
    ew = [jnp.exp(v - picks[0][1]) for _, v, _ in picks]
    den = ew[0]
    for v in ew[1:]:
        den = den + v
    inv_den = 1.0 / den

    before = _dot(onehot.astype(BF16), tri_ref[...]) + carry_sc[:, 0:1]
    carry_sc[...] = carry_sc[...] + jnp.sum(onehot, axis=1, keepdims=True)
    cnt_ref[...] = carry_sc[...]

    rows = [idx for idx, _, _ in picks]
    rows += [jnp.sum(jnp.where(hit, before, 0.0), axis=0, keepdims=True) for _, _, hit in picks]
    rows += [e * inv_den for e in ew]
    rows.append(jnp.zeros((LANES - len(rows), tm), F32))
    rt_ref[...] = jnp.concatenate(rows, axis=0).T


def _merge(o_nsa, o_lru, mg, x, mod3, norm2_g, wn, wl, wo, rw, rb, tri):
    B, S, D = x.shape
    tm = tri.shape[0]
    row = lambda w: pl.BlockSpec((None, tm, w), lambda b, s: (b, s, 0))
    consts = (norm2_g.reshape(1, D), wn, wl, wo, rw, rb, tri)
    return pl.pallas_call(
        _merge_kernel,
        grid=(B, S // tm),
        in_specs=[row(D), row(D), row(2 * D), row(D), pl.BlockSpec((None, 6, D), lambda b, s: (b, 0, 0))]
                 + [_full(a.shape) for a in consts],
        out_specs=[row(D), row(D // 2), row(LANES), _full((N_EXPERTS, LANES))],
        out_shape=[jax.ShapeDtypeStruct((B, S, D), F32), jax.ShapeDtypeStruct((B, S, D // 2), jnp.int32),
                   jax.ShapeDtypeStruct((B, S, LANES), F32), jax.ShapeDtypeStruct((N_EXPERTS, LANES), F32)],
        scratch_shapes=[pltpu.VMEM((N_EXPERTS, LANES), F32)],
        compiler_params=_params(("arbitrary", "arbitrary")),
        name="merge",
    )(o_nsa, o_lru, mg, x, mod3, *consts)


def _expert_kernel(be_ref, nb_ref, x_ref, w1_ref, b1_ref, w2_ref, b2_ref, pick_ref, y_ref, w1_sc, w2_sc):
    i = pl.program_id(0)
    ff = w2_ref.shape[0]
    chunk = pick_ref.shape[0]

    @pl.when((i == 0) | (be_ref[i] != be_ref[jnp.maximum(i - 1, 0)]))
    def _():
        for c in range(2 * ff // chunk):
            r = _dot(w1_ref[:, c * chunk:(c + 1) * chunk].astype(BF16), pick_ref[...])
            half = chunk // 2
            w1_sc[:, c * half:(c + 1) * half] = r[:, 0:half].astype(BF16)
            w1_sc[:, ff + c * half:ff + (c + 1) * half] = r[:, half:chunk].astype(BF16)
        w2_sc[...] = w2_ref[...].astype(BF16)

    @pl.when(i < nb_ref[0])
    def _():
        h = _dot(_unpack_bf16_pairs(x_ref[...]).astype(BF16), w1_sc[...]) + b1_ref[...]
        x_glu = jnp.minimum(h[:, 0:ff], SWIGLU_LIMIT)
        x_lin = jnp.clip(h[:, ff:2 * ff], -SWIGLU_LIMIT, SWIGLU_LIMIT)
        act = x_glu * _sigmoid(SWIGLU_ALPHA * x_glu) * (x_lin + 1.0)
        y_ref[...] = _pack_bf16_pairs(_dot(act.astype(BF16), w2_sc[...]) + b2_ref[...])

    @pl.when(i >= nb_ref[0])
    def _():
        y_ref[...] = jnp.zeros(y_ref.shape, jnp.int32)


def _experts(blk_e, n_used, buf, w1, b1, w2, b2):
    n_rows = buf.shape[0]
    F, D = w2.shape[1], w2.shape[2]
    tm = EXPERT_ROWS
    chunk = 2 * LANES
    pick = np.zeros((chunk, chunk), np.float32)
    pick[2 * np.arange(LANES), np.arange(LANES)] = 1.0
    pick[2 * np.arange(LANES) + 1, LANES + np.arange(LANES)] = 1.0
    wspec = lambda a, b: pl.BlockSpec((None, a, b), lambda i, be, nb: (be[i], 0, 0))
    return pl.pallas_call(
        _expert_kernel,
        grid_spec=pltpu.PrefetchScalarGridSpec(
            num_scalar_prefetch=2,
            grid=(n_rows // tm,),
            in_specs=[pl.BlockSpec((tm, D // 2), lambda i, be, nb: (i, 0)), wspec(D, 2 * F), wspec(1, 2 * F), wspec(F, D),
                      wspec(1, D), pl.BlockSpec((chunk, chunk), lambda i, be, nb: (0, 0))],
            out_specs=pl.BlockSpec((tm, D // 2), lambda i, be, nb: (i, 0)),
            scratch_shapes=[pltpu.VMEM((D, 2 * F), BF16), pltpu.VMEM((F, D), BF16)],
        ),
        out_shape=jax.ShapeDtypeStruct((n_rows, D // 2), jnp.int32),
        compiler_params=_params(("arbitrary",)),
        name="experts",
    )(blk_e, n_used, buf, w1, b1, w2, b2, jnp.asarray(pick, dtype=BF16))


def _sc_mesh():
    return plsc.VectorSubcoreMesh(core_axis_name="c", subcore_axis_name="s")


def _sc_worker():
    return lax.axis_index("s") * SC_CORES + lax.axis_index("c")


def _sc_scatter_rows(rows, dest, n_rows):
    T, W = rows.shape
    K = dest.shape[0]
    n_workers = SC_CORES * SC_SUBCORES
    per_w = T // n_workers
    ch = SC_CHUNK
    n_ch = per_w // ch
    assert per_w * n_workers == T and n_ch * ch == per_w and n_ch % 2 == 0
    dest4 = dest.reshape(K, n_workers, n_ch, ch).transpose(1, 2, 0, 3)

    @functools.partial(
        pl.kernel, mesh=_sc_mesh(), out_type=jax.ShapeDtypeStruct((n_rows, W), rows.dtype),
        scratch_types=[pltpu.VMEM((n_ch, K, ch), jnp.int32), pltpu.VMEM((2, ch, W), rows.dtype),
                       pltpu.SemaphoreType.DMA((2,)), pltpu.SemaphoreType.DMA((2,))])
    def scatter(rows_hbm, dest_hbm, out_hbm, idx_v, rows_v, read_sem, write_sem):
        wid = _sc_worker()
        base = wid * per_w
        pltpu.sync_copy(dest_hbm.at[wid], idx_v)

        def read(c, b):
            src = rows_hbm.at[pl.ds(pl.multiple_of(base + c * ch, ch), ch)]
            return pltpu.make_async_copy(src, rows_v.at[b], read_sem.at[b])

        def write(c, b, k):
            return pltpu.make_async_copy(rows_v.at[b], out_hbm.at[idx_v.at[c, k]], write_sem.at[b])

        read(0, 0).start()

        @pl.loop(0, n_ch, step=2)
        def _(i):
            for b in range(2):
                c = i + b
                read(c, b).wait()

                @pl.when(c >= 1)
                def _():
                    for k in range(K):
                        write(c - 1, 1 - b, k).wait()

                @pl.when(c + 1 < n_ch)
                def _():
                    read(c + 1, 1 - b).start()

                for k in range(K):
                    write(c, b, k).start()

        for k in range(K):
            write(n_ch - 1, 1, k).wait()

    return scatter(rows, dest4)


def _sc_gather_rows(table, idx):
    N = idx.shape[0]
    W = table.shape[1]
    n_workers = SC_CORES * SC_SUBCORES
    per_w = N // n_workers
    ch = SC_CHUNK
    n_ch = per_w // ch
    assert per_w * n_workers == N and n_ch * ch == per_w and n_ch % 2 == 0
    idx3 = idx.reshape(n_workers, n_ch, ch)

    @functools.partial(
        pl.kernel, mesh=_sc_mesh(), out_type=jax.ShapeDtypeStruct((N, W), table.dtype),
        scratch_types=[pltpu.VMEM((n_ch, ch), jnp.int32), pltpu.VMEM((2, ch, W), table.dtype),
                       pltpu.SemaphoreType.DMA((2,)), pltpu.SemaphoreType.DMA((2,))])
    def gather(table_hbm, idx_hbm, out_hbm, idx_v, rows_v, gather_sem, write_sem):
        wid = _sc_worker()
        base = wid * per_w
        pltpu.sync_copy(idx_hbm.at[wid], idx_v)

        def fetch(c, b):
            return pltpu.make_async_copy(table_hbm.at[idx_v.at[c]], rows_v.at[b], gather_sem.at[b])

        def write(c, b):
            dst = out_hbm.at[pl.ds(pl.multiple_of(base + c * ch, ch), ch)]
            return pltpu.make_async_copy(rows_v.at[b], dst, write_sem.at[b])

        fetch(0, 0).start()

        @pl.loop(0, n_ch, step=2)
        def _(i):
            for b in range(2):
                c = i + b
                fetch(c, b).wait()

                @pl.when(c >= 1)
                def _():
                    write(c - 1, 1 - b).wait()

                @pl.when(c + 1 < n_ch)
                def _():
                    fetch(c + 1, 1 - b).start()

                write(c, b).start()

        write(n_ch - 1, 1).wait()

    return gather(table, idx3)


def _combine_kernel(x1_ref, yg_ref, rt_ref, mod_ref, *rest):
    o_ref = rest[-1]
    gate2 = mod_ref[5:6, :]
    rt = rt_ref[...]
    acc = rt[:, 2 * TOP_K:2 * TOP_K + 1] * _unpack_bf16_pairs(yg_ref[0])
    for k in range(1, TOP_K):
        acc = acc + rt[:, 2 * TOP_K + k:2 * TOP_K + k + 1] * _unpack_bf16_pairs(yg_ref[k])
    o_ref[...] = x1_ref[...] + gate2 * acc


def _combine(x1, yg, rt, mod3, prev, b0):
    B, S, D = x1.shape
    nb = yg.shape[1]
    tm = min(COMBINE_ROWS, S)
    row = lambda w: pl.BlockSpec((None, tm, w), lambda b, s: (b0 + b, s, 0))
    in_specs = [row(D), pl.BlockSpec((TOP_K, None, tm, D // 2), lambda b, s: (0, b, s, 0)), row(LANES),
                pl.BlockSpec((None, 6, D), lambda b, s: (b0 + b, 0, 0))]
    operands = [x1, yg, rt, mod3]
    if prev is not None:
        in_specs.append(pl.BlockSpec(memory_space=pl.ANY))
        operands.append(prev)
    return pl.pallas_call(
        _combine_kernel,
        grid=(nb, S // tm),
        in_specs=in_specs,
        out_specs=row(D),
        out_shape=jax.ShapeDtypeStruct((B, S, D), F32),
        input_output_aliases={} if prev is None else {4: 0},
        compiler_params=_params(("arbitrary", "arbitrary")),
        name="combine",
    )(*operands)


def _overlap_matrix(n_cmp_pad, n_blk):
    cs = np.arange(n_cmp_pad)[:, None] * CMP_STRIDE
    js = np.arange(LANES)[None, :] * SEL_BLOCK
    m = (cs <= js + SEL_BLOCK - 1) & (cs + CMP_BLOCK - 1 >= js) & (np.arange(LANES)[None, :] < n_blk)
    return jnp.asarray(m.astype(np.float32).T)


def _block_onehot(seq):
    key = np.arange(seq)[:, None]
    j = np.arange(LANES)[None, :]
    return jnp.asarray((j == key // SEL_BLOCK).astype(np.float32), dtype=BF16)


def _layer(x, c, ang, ada_w, ada_b, norm1_g, w_in, q_norm_g, k_norm_g, cmp_pe_k, cmp_pe_v, cmp_wk1, cmp_wk2, cmp_wv1,
           cmp_wv2, conv_w, conv_b, lru_wa, lru_ba, lru_wx, lru_bx, lru_lambda, w_branch_out, w_out, norm2_g,
           router_w, router_b, moe_w1, moe_b1, moe_w2, moe_b2):
    B, S, D = x.shape
    T = B * S
    G = N_KV_GROUPS
    nsa_w = N_HEADS * HEAD_DIM
    kv_w = G * HEAD_DIM

    mod3 = _ada(c, ada_w, ada_b).reshape(B, 6, D)

    o = 0
    wq = w_in[:, o:o + nsa_w]; o += nsa_w
    wkc = w_in[:, o:o + 2 * kv_w]; o += 2 * kv_w
    wkr = w_in[:, o:o + 4 * kv_w]; o += 4 * kv_w
    n_gate = N_HEADS * N_NSA_BRANCHES
    wg_raw = w_in[:, o:o + n_gate].reshape(D, G, n_gate // G); o += n_gate
    wg = jnp.pad(wg_raw, ((0, 0), (0, 0), (0, LANES - n_gate // G))).reshape(D, G * LANES)
    wlx = w_in[:, o:o + D]; o += D
    wlg = w_in[:, o:o + D]; o += D
    wm = w_in[:, o:o + 2 * D]
    bf = lambda a: a.astype(BF16)

    qn, qr, kvc, kvr, gates, lx, lg, mg = _in_proj(x, mod3, norm1_g, ang, q_norm_g, k_norm_g, bf(wq), bf(wkc), bf(wkr),
                                                   bf(wg), bf(wlx), bf(wlg), bf(wm))

    nc = S // CMP_STRIDE
    pe = lambda p: jnp.broadcast_to(p.reshape(1, CMP_BLOCK * HEAD_DIM), (SUBLANES, CMP_BLOCK * HEAD_DIM)).astype(BF16)
    kc, vc = _compress(kvc, pe(cmp_pe_k), pe(cmp_pe_v), bf(cmp_wk1), bf(cmp_wk2), bf(cmp_wv1), bf(cmp_wv2), k_norm_g)

    ocmp, bias = _select(qn, kc, vc, gates, _overlap_matrix(nc, S // SEL_BLOCK), S // SEL_BLOCK)
    o_nsa = _nsa(qr, bias, ocmp, kvr, gates, _block_onehot(S))
    o_lru = _lru(lx, lg, conv_w, conv_b, bf(lru_wa), lru_ba, bf(lru_wx), lru_bx, lru_lambda)

    tm = min(IN_ROWS, S)
    tri = jnp.asarray(np.triu(np.ones((tm, tm), np.float32), 1), dtype=BF16)
    rw = jnp.pad(router_w.T, ((0, LANES - N_EXPERTS), (0, 0)))
    rb = router_b.reshape(N_EXPERTS, 1)
    x1, h2, rt, cnt = _merge(o_nsa, o_lru, mg, x, mod3, norm2_g, bf(w_branch_out[:nsa_w]), bf(w_branch_out[nsa_w:]),
                             bf(w_out), rw, rb, tri)

    rt2 = rt.reshape(T, LANES)
    top_e = rt2[:, 0:TOP_K].astype(jnp.int32)
    rank = rt2[:, TOP_K:2 * TOP_K].astype(jnp.int32)
    counts = cnt[:, 0].astype(jnp.int32)
    padded = (counts + EXPERT_ROWS - 1) // EXPERT_ROWS * EXPERT_ROWS
    ends = jnp.cumsum(padded)
    start = ends - padded
    dest = start[top_e] + rank
    n_rows = -(-(T * TOP_K + N_EXPERTS * (EXPERT_ROWS - 1)) // EXPERT_ROWS) * EXPERT_ROWS
    n_blocks = n_rows // EXPERT_ROWS
    blk_first = jnp.arange(n_blocks, dtype=jnp.int32) * EXPERT_ROWS
    blk_e = jnp.minimum(jnp.sum(ends[None, :] <= blk_first[:, None], axis=1), N_EXPERTS - 1).astype(jnp.int32)
    n_used = (ends[-1:] // EXPERT_ROWS).astype(jnp.int32)

    dest_kt = dest.T
    buf = _sc_scatter_rows(h2.reshape(T, D // 2), dest_kt, n_rows)

    b1 = jnp.concatenate([moe_b1[:, 0::2], moe_b1[:, 1::2]], axis=1)
    ybuf = _experts(blk_e, n_used, buf, moe_w1, b1.reshape(N_EXPERTS, 1, -1), moe_w2, moe_b2.reshape(N_EXPERTS, 1, D))
    nb = B // COMBINE_PARTS if B % COMBINE_PARTS == 0 else B
    dest_parts = dest_kt.reshape(TOP_K, B // nb, nb * S)
    out = None
    for p in range(B // nb):
        yg = _sc_gather_rows(ybuf, dest_parts[:, p].reshape(-1)).reshape(TOP_K, nb, S, D // 2)
        out = _combine(x1, yg, rt, mod3, out, p * nb)
    return out


def kernel(x, c, positions, ada_w, ada_b, norm1_g, w_in, q_norm_g, k_norm_g, cmp_pe_k, cmp_pe_v, cmp_wk1, cmp_wk2, cmp_wv1, cmp_wv2, conv_w, conv_b, lru_wa, lru_ba, lru_wx, lru_bx, lru_lambda, w_branch_out, w_out, norm2_g, router_w, router_b, moe_w1, moe_b1, moe_w2, moe_b2):
    inv = ROPE_THETA ** (-jnp.arange(0, HEAD_DIM, 2, dtype=F32) / HEAD_DIM)
    ang = positions.astype(F32)[..., None] * inv
    ang = jnp.concatenate([ang, ang], axis=-1)
    for l in range(ada_w.shape[0]):
        x = _layer(x, c, ang, ada_w[l], ada_b[l], norm1_g[l], w_in[l], q_norm_g[l], k_norm_g[l], cmp_pe_k[l],
                   cmp_pe_v[l], cmp_wk1[l], cmp_wk2[l], cmp_wv1[l], cmp_wv2[l], conv_w[l], conv_b[l], lru_wa[l],
                   lru_ba[l], lru_wx[l], lru_bx[l], lru_lambda[l], w_branch_out[l], w_out[l], norm2_g[l], router_w[l],
                   router_b[l], moe_w1[l], moe_b1[l], moe_w2[l], moe_b2[l])
    return x
```

```python
import functools
import math

import jax
import jax.numpy as jnp
import numpy as np
from jax import lax
from jax.experimental import pallas as pl
from jax.experimental.pallas import tpu as pltpu
from jax.experimental.pallas import tpu_sc as plsc

F32 = jnp.float32
BF16 = jnp.bfloat16

N_HEADS = 8
HEAD_DIM = 128
N_KV_GROUPS = 2
HEADS_PER_GROUP = N_HEADS // N_KV_GROUPS
N_NSA_BRANCHES = 3
CMP_BLOCK = 32
CMP_STRIDE = 16
CMP_HIDDEN = 256
SEL_BLOCK = 64
N_SELECT = 8
WINDOW = 512
ROPE_THETA = 10000.0
LRU_BLOCKS = 4
CONV_WIDTH = 4
LRU_C = 8.0
N_EXPERTS = 32
TOP_K = 4
SWIGLU_LIMIT = 7.0
SWIGLU_ALPHA = 1.702
RMS_EPS = 1e-6
NEG_INF = -1e30
LOWEST = -3.0e38
LOG2_E = 1.4426950408889634

LANES = 128
SUBLANES = 8
VMEM_LIMIT = 56 * 1024 * 1024
SC_CORES = 2
SC_SUBCORES = 16
SC_CHUNK = 64

Q_TILE = 256
SELECT_ROWS = 1024
KEY_TILE = 128
ONES_ROWS = 16
HEAD_PAIR = 2
IN_ROWS = 512
MERGE_SPLIT = 4
LRU_ROWS = 1024
EXPERT_ROWS = 512
COMBINE_ROWS = 1024
COMBINE_PARTS = 8


def _sigmoid(v):
    return 0.5 * jnp.tanh(0.5 * v) + 0.5


def _gelu_tanh(v):
    c = math.sqrt(2.0 / math.pi)
    half = 0.5 * v
    return half + half * jnp.tanh(v * (c + (c * 0.044715) * (v * v)))


def _rms(v, g):
    return v * lax.rsqrt(jnp.mean(v * v, axis=-1, keepdims=True) + RMS_EPS) * g


def _dot(a, b, **kw):
    return jnp.dot(a, b, preferred_element_type=F32, **kw)


def _dot_nt(a, b, **kw):
    return lax.dot_general(a, b, (((1,), (1,)), ((), ())), preferred_element_type=F32, **kw)


def _pack_bf16_pairs(v):
    n = v.shape[1] // 2
    lo = lax.bitcast_convert_type(v[:, 0:n].astype(BF16).astype(F32), jnp.int32)
    hi = lax.bitcast_convert_type(v[:, n:2 * n].astype(BF16).astype(F32), jnp.int32)
    return lax.shift_right_logical(lo, 16) | hi


def _unpack_bf16_pairs(w):
    lo = lax.bitcast_convert_type(lax.shift_left(w, 16), F32)
    hi = lax.bitcast_convert_type(w & jnp.int32(-65536), F32)
    return jnp.concatenate([lo, hi], axis=1)


def _full(shape):
    nd = len(shape)
    return pl.BlockSpec(shape, lambda *_: (0,) * nd)


def _params(sem):
    return pltpu.CompilerParams(dimension_semantics=sem, vmem_limit_bytes=VMEM_LIMIT)


def _ada_kernel(c_ref, w_ref, b_ref, o_ref):
    c = c_ref[...]
    o_ref[...] = _dot(c * _sigmoid(c), w_ref[...], precision=lax.Precision.HIGHEST) + b_ref[...]


def _ada(c, ada_w, ada_b):
    B, D = c.shape
    N = ada_w.shape[1]
    return pl.pallas_call(
        _ada_kernel,
        grid=(N // D,),
        in_specs=[_full((B, D)), pl.BlockSpec((D, D), lambda j: (0, j)), pl.BlockSpec((1, D), lambda j: (0, j))],
        out_specs=pl.BlockSpec((B, D), lambda j: (0, j)),
        out_shape=jax.ShapeDtypeStruct((B, N), F32),
        compiler_params=_params(("arbitrary",)),
        name="ada",
    )(c, ada_w, ada_b.reshape(1, N))


def _in_kernel(x_ref, mod_ref, g1_ref, ang_ref, qg_ref, kg_ref, wq_ref, wkc_ref, wkr_ref, wg_ref, wlx_ref, wlg_ref,
               wm_ref, qn_ref, qr_ref, kvc_ref, kvr_ref, gt_ref, lx_ref, lg_ref, mg_ref, kvc_sc):
    x = x_ref[...]
    shift1 = mod_ref[0:1, :]
    scale1 = mod_ref[1:2, :]
    h = _rms(x, g1_ref[...]) * (1.0 + scale1) + shift1
    hb = h.astype(BF16)

    ang = ang_ref[...]
    cos = jnp.cos(ang)
    sin = jnp.sin(ang)
    lane = lax.broadcasted_iota(jnp.int32, ang.shape, 1)
    sin_signed = jnp.where(lane < HEAD_DIM // 2, -sin, sin)

    def rope(v):
        return v * cos + pltpu.roll(v, HEAD_DIM // 2, 1) * sin_signed

    q = _dot(hb, wq_ref[...])
    for hh in range(N_HEADS):
        sl = slice(hh * HEAD_DIM, (hh + 1) * HEAD_DIM)
        qh = _rms(q[:, sl], qg_ref[...])
        qn_ref[:, sl] = (qh * HEAD_DIM ** -0.5).astype(BF16)
        qr_ref[:, sl] = (rope(qh) * (HEAD_DIM ** -0.5 * LOG2_E)).astype(BF16)

    kvc = _dot(hb, wkc_ref[...])
    for part in range(kvc_ref.shape[0]):
        kvc_sc[part] = kvc[:, part * HEAD_DIM:(part + 1) * HEAD_DIM]
        for tok in range(CMP_STRIDE):
            piece = kvc_sc[part, pl.ds(tok, kvc_ref.shape[1], stride=CMP_STRIDE), :]
            kvc_ref[part, :, tok * HEAD_DIM:(tok + 1) * HEAD_DIM] = piece.astype(BF16)

    kvr = _dot(hb, wkr_ref[...])
    kvw = N_KV_GROUPS * HEAD_DIM
    for part in range(4):
        for gg in range(N_KV_GROUPS):
            sl = slice(part * kvw + gg * HEAD_DIM, part * kvw + (gg + 1) * HEAD_DIM)
            v = kvr[:, sl]
            if part % 2 == 0:
                row = 1 + part // 2
                v = rope(_rms(v, kg_ref[row:row + 1, :]))
            kvr_ref[:, sl] = v.astype(BF16)

    gt_ref[...] = _sigmoid(_dot(hb, wg_ref[...]))
    lx_ref[...] = _dot(hb, wlx_ref[...]).astype(BF16)
    lg_ref[...] = _dot(hb, wlg_ref[...]).astype(BF16)
    mg_ref[...] = _sigmoid(_dot(hb, wm_ref[...])).astype(BF16)


def _in_proj(x, mod3, norm1_g, ang, q_norm_g, k_norm_g, wq, wkc, wkr, wg, wlx, wlg, wm):
    B, S, D = x.shape
    tm = min(IN_ROWS, S)
    row = lambda w: pl.BlockSpec((None, tm, w), lambda b, s: (b, s, 0))
    widths = (wq.shape[1], wq.shape[1], wkc.shape[1], wkr.shape[1], wg.shape[1], wlx.shape[1], wlg.shape[1], wm.shape[1])
    dtypes = (BF16, BF16, BF16, BF16, F32, BF16, BF16, BF16)
    weights = (wq, wkc, wkr, wg, wlx, wlg, wm)
    out_specs = [row(w) for w in widths]
    out_shape = [jax.ShapeDtypeStruct((B, S, w), dt) for w, dt in zip(widths, dtypes)]
    n_part = wkc.shape[1] // HEAD_DIM
    out_specs[2] = pl.BlockSpec((None, n_part, tm // CMP_STRIDE, CMP_STRIDE * HEAD_DIM), lambda b, s: (b, 0, s, 0))
    out_shape[2] = jax.ShapeDtypeStruct((B, n_part, S // CMP_STRIDE, CMP_STRIDE * HEAD_DIM), BF16)
    return pl.pallas_call(
        _in_kernel,
        grid=(B, S // tm),
        in_specs=[row(D), pl.BlockSpec((None, 6, D), lambda b, s: (b, 0, 0)), _full((1, D)), row(HEAD_DIM),
                  _full((1, HEAD_DIM)), _full(k_norm_g.shape)] + [_full(w.shape) for w in weights],
        out_specs=out_specs,
        out_shape=out_shape,
        scratch_shapes=[pltpu.VMEM((n_part, tm, HEAD_DIM), F32)],
        compiler_params=_params(("arbitrary", "arbitrary")),
        name="in_proj",
    )(x, mod3, norm1_g.reshape(1, D), ang, q_norm_g.reshape(1, HEAD_DIM), k_norm_g, *weights)


def _cmp_kernel(ak_ref, av_ref, pek_ref, pev_ref, wk1_ref, wk2_ref, wv1_ref, wv2_ref, kg_ref, kc_ref, vc_ref):
    half = CMP_STRIDE * HEAD_DIM

    def compress(a_ref, pe_ref, w1_ref, w2_ref):
        a = a_ref[...]
        u = _dot(a, w1_ref[0:half, :])
        v = _dot(a, w1_ref[half:2 * half, :])
        pw = _dot(pe_ref[...], w1_ref[...])
        pre = u + pltpu.roll(v, v.shape[0] - 1, 0) + pw[0:1, :]
        return _dot(_gelu_tanh(pre).astype(BF16), w2_ref[...])

    kc_ref[...] = _rms(compress(ak_ref, pek_ref, wk1_ref, wk2_ref), kg_ref[0:1, :]).astype(BF16)
    vc_ref[...] = compress(av_ref, pev_ref, wv1_ref, wv2_ref).T.astype(BF16)


def _compress(a, pe_k, pe_v, wk1, wk2, wv1, wv2, k_norm_g):
    B, _, NC, W = a.shape
    G = N_KV_GROUPS
    out = pl.BlockSpec((None, None, NC, HEAD_DIM), lambda b, g: (b, g, 0, 0))
    consts = (pe_k, pe_v, wk1, wk2, wv1, wv2, k_norm_g)
    return pl.pallas_call(
        _cmp_kernel,
        grid=(B, G),
        in_specs=[pl.BlockSpec((None, None, NC, W), lambda b, g: (b, g, 0, 0)),
                  pl.BlockSpec((None, None, NC, W), lambda b, g: (b, G + g, 0, 0))] + [_full(c.shape) for c in consts],
        out_specs=[out, out],
        out_shape=[jax.ShapeDtypeStruct((B, G, NC, HEAD_DIM), BF16)] * 2,
        compiler_params=_params(("arbitrary", "arbitrary")),
        name="compress",
    )(a, a, *consts)


def _select_kernel(qn_ref, kc_ref, vc_ref, gt_ref, ov_ref, ocmp_ref, bias_ref, *, n_blk):
    qi = pl.program_id(2)
    tq = qn_ref.shape[0]
    hpg = HEADS_PER_GROUP
    qn = jnp.concatenate([qn_ref[:, h * HEAD_DIM:(h + 1) * HEAD_DIM] for h in range(hpg)], axis=0)
    row = lax.broadcasted_iota(jnp.int32, (kc_ref.shape[0], tq), 0)
    t_q = qi * tq + lax.broadcasted_iota(jnp.int32, (kc_ref.shape[0], tq), 1)

    cmask = jnp.concatenate([row * CMP_STRIDE + (CMP_BLOCK - 1) <= t_q] * hpg, axis=1)
    s = jnp.where(cmask, _dot_nt(kc_ref[...], qn), NEG_INF)
    e = jnp.exp(s - jnp.max(s, axis=0, keepdims=True))
    p = e * jnp.where(cmask[0:1, :], 1.0 / jnp.sum(e, axis=0, keepdims=True), 0.0)
    o_cmp = _dot(vc_ref[...], p.astype(BF16))
    gt = gt_ref[...]
    for h in range(hpg):
        c0 = h * N_NSA_BRANCHES
        ocmp_ref[:, h * HEAD_DIM:(h + 1) * HEAD_DIM] = (gt[:, c0:c0 + 1] * o_cmp[:, h * tq:(h + 1) * tq].T).astype(BF16)

    psum = p[:, 0:tq]
    for h in range(1, hpg):
        psum = psum + p[:, h * tq:(h + 1) * tq]
    imp = _dot(ov_ref[...], psum, precision=lax.Precision.HIGHEST)[0:n_blk]
    blk = lax.broadcasted_iota(jnp.int32, (n_blk, tq), 0).astype(F32)
    cur = ((qi * tq + lax.broadcasted_iota(jnp.int32, (n_blk, tq), 1)) // SEL_BLOCK).astype(F32)
    forced = (blk == 0.0) | (blk == cur) | (blk == cur - 1.0)
    score = jnp.where(forced, 1e6, jnp.where(blk <= cur, imp, -1e6))
    bias = jnp.full((n_blk, tq), NEG_INF, F32)
    for _ in range(min(N_SELECT, n_blk)):
        best = jnp.max(score, axis=0, keepdims=True)
        idx = jnp.min(jnp.where(score == best, blk, float(LANES)), axis=0, keepdims=True)
        hit = blk == idx
        bias = jnp.where(hit, 0.0, bias)
        score = jnp.where(hit, LOWEST, score)
    bias = jnp.concatenate([bias, jnp.zeros((LANES - n_blk, tq), F32)], axis=0)
    bias_ref[...] = bias.T.astype(BF16)


def _select(qn, kc, vc, gates, overlap, n_blk):
    B, S, _ = qn.shape
    G = N_KV_GROUPS
    tq = min(SELECT_ROWS, S)
    gw = HEADS_PER_GROUP * HEAD_DIM
    qspec = pl.BlockSpec((None, tq, gw), lambda b, g, i: (b, i, g))
    lane_spec = pl.BlockSpec((None, tq, LANES), lambda b, g, i: (b, i, g))
    cspec = pl.BlockSpec((None, None) + kc.shape[2:], lambda b, g, i: (b, g, 0, 0))
    return pl.pallas_call(
        functools.partial(_select_kernel, n_blk=n_blk),
        grid=(B, G, S // tq),
        in_specs=[qspec, cspec, cspec, lane_spec, _full(overlap.shape)],
        out_specs=[qspec, lane_spec],
        out_shape=[jax.ShapeDtypeStruct(qn.shape, BF16), jax.ShapeDtypeStruct((B, S, G * LANES), BF16)],
        compiler_params=_params(("arbitrary", "arbitrary", "arbitrary")),
        name="select",
    )(qn, kc, vc, gates, overlap)


def _nsa_kernel(qr_ref, bias_ref, ocmp_ref, ks_ref, vs_ref, kw_ref, vw_ref, gt_ref, ex_ref, o_ref, vst_sc, vwt_sc):
    qi = pl.program_id(2)
    tq = qr_ref.shape[0]
    hpg = HEADS_PER_GROUP
    seq = ks_ref.shape[0]

    @pl.when(qi == 0)
    def _():
        for kt in range(seq // KEY_TILE):
            rows = slice(kt * KEY_TILE, (kt + 1) * KEY_TILE)
            vst_sc[0:HEAD_DIM, rows] = vs_ref[rows, :].astype(F32).T.astype(BF16)
            vwt_sc[0:HEAD_DIM, rows] = vw_ref[rows, :].astype(F32).T.astype(BF16)
        ones = jnp.ones((vst_sc.shape[0] - HEAD_DIM, seq), BF16)
        vst_sc[HEAD_DIM:, :] = ones
        vwt_sc[HEAD_DIM:, :] = ones

    def pair(v):
        return jnp.concatenate([v] * HEAD_PAIR, axis=1)

    qr = jnp.concatenate([qr_ref[:, h * HEAD_DIM:(h + 1) * HEAD_DIM] for h in range(hpg)], axis=0)
    q_aug = jnp.concatenate([qr, jnp.concatenate([bias_ref[...]] * hpg, axis=0)], axis=1)
    pair_cols = [slice(hp * HEAD_PAIR * tq, (hp + 1) * HEAD_PAIR * tq) for hp in range(hpg // HEAD_PAIR)]

    def softmax_pv(sc, vt):
        pr = jnp.exp2((sc - jnp.max(sc, axis=0, keepdims=True)).astype(BF16))
        o = _dot(vt, pr)
        return o[0:HEAD_DIM] * (1.0 / o[HEAD_DIM:HEAD_DIM + 1])

    def biased(sc, first_bias, last_bias):
        n = sc.shape[0]
        if first_bias is not None:
            sc = jnp.concatenate([sc[0:tq] + first_bias, sc[tq:n]], axis=0)
        return jnp.concatenate([sc[0:n - tq], sc[n - tq:n] + last_bias], axis=0) if n > tq else sc + last_bias

    r_in = lax.broadcasted_iota(jnp.int32, (tq, tq), 0)
    c_in = lax.broadcasted_iota(jnp.int32, (tq, tq), 1)
    causal = pair(jnp.where(r_in <= c_in, 0.0, NEG_INF))
    window_tail = pair(jnp.where(r_in > c_in, 0.0, NEG_INF))
    gt = gt_ref[...].T

    for v in range(seq // tq):
        @pl.when(qi == v)
        def _():
            ext = (v + 1) * tq
            k_aug = jnp.concatenate([ks_ref[0:ext, :], ex_ref[0:ext, :]], axis=1)
            w_lo = max(v * tq - WINDOW, 0)
            tail = window_tail if v * tq - WINDOW >= 0 else None
            k_win = kw_ref[w_lo:ext, :]
            scores = [biased(_dot_nt(k_aug, q_aug[cols]), None, causal) for cols in pair_cols]
            win_scores = [biased(_dot_nt(k_win, qr[cols]), tail, causal) for cols in pair_cols]
            o_slcs = [softmax_pv(sc, vst_sc[:, 0:ext]) for sc in scores]
            o_wins = [softmax_pv(sc, vwt_sc[:, w_lo:ext]) for sc in win_scores]
            for h in range(hpg):
                hp, j = divmod(h, HEAD_PAIR)
                sub = slice(j * tq, (j + 1) * tq)
                c0 = h * N_NSA_BRANCHES
                o = gt[c0 + 1:c0 + 2, :] * o_slcs[hp][:, sub] + gt[c0 + 2:c0 + 3, :] * o_wins[hp][:, sub]
                hd = slice(h * HEAD_DIM, (h + 1) * HEAD_DIM)
                o_ref[:, hd] = (o.T + ocmp_ref[:, hd].astype(F32)).astype(BF16)


def _nsa(qr, bias, ocmp, kvr, gates, onehot):
    B, S, _ = qr.shape
    G = N_KV_GROUPS
    tq = Q_TILE
    gw = HEADS_PER_GROUP * HEAD_DIM
    qspec = pl.BlockSpec((None, tq, gw), lambda b, g, i: (b, i, g))
    lane_spec = pl.BlockSpec((None, tq, LANES), lambda b, g, i: (b, i, g))
    kv = lambda part: pl.BlockSpec((None, S, HEAD_DIM), lambda b, g, i: (b, 0, part * G + g))
    return pl.pallas_call(
        _nsa_kernel,
        grid=(B, G, S // tq),
        in_specs=[qspec, lane_spec, qspec, kv(0), kv(1), kv(2), kv(3), lane_spec, _full(onehot.shape)],
        out_specs=qspec,
        out_shape=jax.ShapeDtypeStruct(qr.shape, BF16),
        scratch_shapes=[pltpu.VMEM((HEAD_DIM + ONES_ROWS, S), BF16), pltpu.VMEM((HEAD_DIM + ONES_ROWS, S), BF16)],
        compiler_params=_params(("arbitrary", "arbitrary", "arbitrary")),
        name="nsa",
    )(qr, bias, ocmp, kvr, kvr, kvr, kvr, gates, onehot)


def _lru_kernel(lx_ref, lg_ref, cw_ref, cb_ref, wa_ref, ba_ref, wx_ref, bx_ref, lam_ref, o_ref,
                xs_sc, a_sc, u_sc, h_sc, carry_sc):
    ts = lx_ref.shape[0]
    width = lx_ref.shape[1]
    bw = width // LRU_BLOCKS

    @pl.when(pl.program_id(1) == 0)
    def _():
        xs_sc[0:SUBLANES, :] = jnp.zeros((SUBLANES, width), F32)
        carry_sc[...] = jnp.zeros(carry_sc.shape, F32)

    xs_sc[SUBLANES:SUBLANES + ts, :] = lx_ref[...].astype(F32)
    xc = cb_ref[...] + cw_ref[CONV_WIDTH - 1:CONV_WIDTH, :] * xs_sc[SUBLANES:SUBLANES + ts, :]
    for d in range(1, CONV_WIDTH):
        w = cw_ref[CONV_WIDTH - 1 - d:CONV_WIDTH - d, :]
        xc = xc + w * xs_sc[SUBLANES - d:SUBLANES - d + ts, :]
    xs_sc[0:SUBLANES, :] = xs_sc[ts:ts + SUBLANES, :]

    lam = -lam_ref[...]
    neg_c_softplus = -LRU_C * (jnp.maximum(lam, 0.0) + jnp.log1p(jnp.exp(-jnp.abs(lam))))
    xcb = xc.astype(BF16)
    for blk in range(LRU_BLOCKS):
        sl = slice(blk * bw, (blk + 1) * bw)
        r = _sigmoid(_dot(xcb[:, sl], wa_ref[blk]) + ba_ref[:, sl])
        i = _sigmoid(_dot(xcb[:, sl], wx_ref[blk]) + bx_ref[:, sl])
        log_a = r * neg_c_softplus[:, sl]
        a = jnp.exp(log_a)
        a_sc[:, sl] = a
        u_sc[:, sl] = jnp.sqrt(-jnp.tanh(log_a) * (a * a + 1.0)) * (i * xc[:, sl])

    row = lax.broadcasted_iota(jnp.int32, (SUBLANES, width), 0)

    def chunk(c, h_prev):
        r0 = pl.multiple_of(c * SUBLANES, SUBLANES)
        a = a_sc[pl.ds(r0, SUBLANES), :]
        u = u_sc[pl.ds(r0, SUBLANES), :]
        for d in (1, 2, 4):
            keep = row >= d
            u = jnp.where(keep, a * pltpu.roll(u, d, 0) + u, u)
            a = jnp.where(keep, a * pltpu.roll(a, d, 0), a)
        h = a * h_prev + u
        h_sc[pl.ds(r0, SUBLANES), :] = h
        return jnp.broadcast_to(h[SUBLANES - 1:SUBLANES, :], h.shape)

    carry_sc[...] = lax.fori_loop(0, ts // SUBLANES, chunk, carry_sc[...], unroll=2)
    o_ref[...] = (_gelu_tanh(lg_ref[...].astype(F32)) * h_sc[...]).astype(BF16)


def _lru(lx, lg, conv_w, conv_b, wa, ba, wx, bx, lam):
    B, S, W = lx.shape
    ts = min(LRU_ROWS, S)
    row = pl.BlockSpec((None, ts, W), lambda b, s: (b, s, 0))
    consts = (conv_w, conv_b.reshape(1, W), wa, ba.reshape(1, W), wx, bx.reshape(1, W), lam.reshape(1, W))
    return pl.pallas_call(
        _lru_kernel,
        grid=(B, S // ts),
        in_specs=[row, row] + [_full(a.shape) for a in consts],
        out_specs=row,
        out_shape=jax.ShapeDtypeStruct((B, S, W), BF16),
        scratch_shapes=[pltpu.VMEM((ts + SUBLANES, W), F32), pltpu.VMEM((ts, W), F32), pltpu.VMEM((ts, W), F32),
                        pltpu.VMEM((ts, W), F32), pltpu.VMEM((SUBLANES, W), F32)],
        compiler_params=_params(("arbitrary", "arbitrary")),
        name="lru",
    )(lx, lg, *consts)


def _merge_kernel(on_ref, ol_ref, mg_ref, x_ref, mod_ref, g2_ref, wn_ref, wl_ref, wo_ref, rw_ref, rb_ref, tri_ref,
                  x1_ref, h2_ref, rt_ref, cnt_ref, carry_sc):
    first = (pl.program_id(0) == 0) & (pl.program_id(1) == 0)

    @pl.when(first)
    def _():
        carry_sc[...] = jnp.zeros(carry_sc.shape, F32)

    d = x_ref.shape[1]
    gate1 = mod_ref[2:3, :]
    shift2 = mod_ref[3:4, :]
    scale2 = mod_ref[4:5, :]
    rw = rw_ref[...]
    rw_hi = rw.astype(BF16)
    rw_lo = (rw - rw_hi.astype(F32)).astype(BF16)

    n_rows = x_ref.shape[0] // MERGE_SPLIT
    groups = [slice(i * n_rows, (i + 1) * n_rows) for i in range(MERGE_SPLIT)]
    y_nsa = [_dot(on_ref[r, :], wn_ref[...]) for r in groups]
    y_lru = [_dot(ol_ref[r, :], wl_ref[...]) for r in groups]
    merged = [mg_ref[r, 0:d].astype(F32) * yn + mg_ref[r, d:2 * d].astype(F32) * yl
              for r, yn, yl in zip(groups, y_nsa, y_lru)]
    x1 = [x_ref[r, :] + gate1 * _dot(m.astype(BF16), wo_ref[...]) for r, m in zip(groups, merged)]
    h2 = [_rms(v, g2_ref[...]) * (1.0 + scale2) + shift2 for v in x1]
    logit_parts = []
    for r, v1, v2 in zip(groups, x1, h2):
        x1_ref[r, :] = v1
        h2_ref[r, :] = _pack_bf16_pairs(v2)
        h2_hi = v2.astype(BF16)
        h2_lo = (v2 - h2_hi.astype(F32)).astype(BF16)
        logit_parts.append(_dot_nt(rw_hi, h2_hi) + (_dot_nt(rw_hi, h2_lo) + _dot_nt(rw_lo, h2_hi)))
    logits = jnp.concatenate(logit_parts, axis=1)[0:N_EXPERTS] + rb_ref[...]
    tm = logits.shape[1]
    eid = lax.broadcasted_iota(jnp.int32, logits.shape, 0).astype(F32)
    score = logits
    picks = []
    onehot = jnp.zeros(logits.shape, F32)
    for _ in range(TOP_K):
        best = jnp.max(score, axis=0, keepdims=True)
        idx = jnp.min(jnp.where(score == best, eid, float(LANES)), axis=0, keepdims=True)
        hit = eid == idx
        picks.append((idx, best, hit))
        onehot = jnp.where(hit, 1.0, onehot)
        score = jnp.where(hit, LOWEST, score)
    ew = [jnp.exp(v - picks[0][1]) for _, v, _ in picks]
    den = ew[0]
    for v in ew[1:]:
        den = den + v
    inv_den = 1.0 / den

    before = _dot(onehot.astype(BF16), tri_ref[...]) + carry_sc[:, 0:1]
    carry_sc[...] = carry_sc[...] + jnp.sum(onehot, axis=1, keepdims=True)
    cnt_ref[...] = carry_sc[...]

    rows = [idx for idx, _, _ in picks]
    rows += [jnp.sum(jnp.where(hit, before, 0.0), axis=0, keepdims=True) for _, _, hit in picks]
    rows += [e * inv_den for e in ew]
    rows.append(jnp.zeros((LANES - len(rows), tm), F32))
    rt_ref[...] = jnp.concatenate(rows, axis=0).T


def _merge(o_nsa, o_lru, mg, x, mod3, norm2_g, wn, wl, wo, rw, rb, tri):
    B, S, D = x.shape
    tm = tri.shape[0]
    row = lambda w: pl.BlockSpec((None, tm, w), lambda b, s: (b, s, 0))
    consts = (norm2_g.reshape(1, D), wn, wl, wo, rw, rb, tri)
    return pl.pallas_call(
        _merge_kernel,
        grid=(B, S // tm),
        in_specs=[row(D), row(D), row(2 * D), row(D), pl.BlockSpec((None, 6, D), lambda b, s: (b, 0, 0))]
                 + [_full(a.shape) for a in consts],
        out_specs=[row(D), row(D // 2), row(LANES), _full((N_EXPERTS, LANES))],
        out_shape=[jax.ShapeDtypeStruct((B, S, D), F32), jax.ShapeDtypeStruct((B, S, D // 2), jnp.int32),
                   jax.ShapeDtypeStruct((B, S, LANES), F32), jax.ShapeDtypeStruct((N_EXPERTS, LANES), F32)],
        scratch_shapes=[pltpu.VMEM((N_EXPERTS, LANES), F32)],
        compiler_params=_params(("arbitrary", "arbitrary")),
        name="merge",
    )(o_nsa, o_lru, mg, x, mod3, *consts)


def _expert_kernel(be_ref, nb_ref, x_ref, w1_ref, b1_ref, w2_ref, b2_ref, pick_ref, y_ref, w1_sc, w2_sc):
    i = pl.program_id(0)
    ff = w2_ref.shape[0]
    chunk = pick_ref.shape[0]

    @pl.when((i == 0) | (be_ref[i] != be_ref[jnp.maximum(i - 1, 0)]))
    def _():
        for c in range(2 * ff // chunk):
            r = _dot(w1_ref[:, c * chunk:(c + 1) * chunk].astype(BF16), pick_ref[...])
            half = chunk // 2
            w1_sc[:, c * half:(c + 1) * half] = r[:, 0:half].astype(BF16)
            w1_sc[:, ff + c * half:ff + (c + 1) * half] = r[:, half:chunk].astype(BF16)
        w2_sc[...] = w2_ref[...].astype(BF16)

    @pl.when(i < nb_ref[0])
    def _():
        h = _dot(_unpack_bf16_pairs(x_ref[...]).astype(BF16), w1_sc[...]) + b1_ref[...]
        x_glu = jnp.minimum(h[:, 0:ff], SWIGLU_LIMIT)
        x_lin = jnp.clip(h[:, ff:2 * ff], -SWIGLU_LIMIT, SWIGLU_LIMIT)
        act = x_glu * _sigmoid(SWIGLU_ALPHA * x_glu) * (x_lin + 1.0)
        y_ref[...] = _pack_bf16_pairs(_dot(act.astype(BF16), w2_sc[...]) + b2_ref[...])

    @pl.when(i >= nb_ref[0])
    def _():
        y_ref[...] = jnp.zeros(y_ref.shape, jnp.int32)


def _experts(blk_e, n_used, buf, w1, b1, w2, b2):
    n_rows = buf.shape[0]
    F, D = w2.shape[1], w2.shape[2]
    tm = EXPERT_ROWS
    chunk = 2 * LANES
    pick = np.zeros((chunk, chunk), np.float32)
    pick[2 * np.arange(LANES), np.arange(LANES)] = 1.0
    pick[2 * np.arange(LANES) + 1, LANES + np.arange(LANES)] = 1.0
    wspec = lambda a, b: pl.BlockSpec((None, a, b), lambda i, be, nb: (be[i], 0, 0))
    return pl.pallas_call(
        _expert_kernel,
        grid_spec=pltpu.PrefetchScalarGridSpec(
            num_scalar_prefetch=2,
            grid=(n_rows // tm,),
            in_specs=[pl.BlockSpec((tm, D // 2), lambda i, be, nb: (i, 0)), wspec(D, 2 * F), wspec(1, 2 * F), wspec(F, D),
                      wspec(1, D), pl.BlockSpec((chunk, chunk), lambda i, be, nb: (0, 0))],
            out_specs=pl.BlockSpec((tm, D // 2), lambda i, be, nb: (i, 0)),
            scratch_shapes=[pltpu.VMEM((D, 2 * F), BF16), pltpu.VMEM((F, D), BF16)],
        ),
        out_shape=jax.ShapeDtypeStruct((n_rows, D // 2), jnp.int32),
        compiler_params=_params(("arbitrary",)),
        name="experts",
    )(blk_e, n_used, buf, w1, b1, w2, b2, jnp.asarray(pick, dtype=BF16))


def _sc_mesh():
    return plsc.VectorSubcoreMesh(core_axis_name="c", subcore_axis_name="s")


def _sc_worker():
    return lax.axis_index("s") * SC_CORES + lax.axis_index("c")


def _sc_scatter_rows(rows, dest, n_rows):
    T, W = rows.shape
    K = dest.shape[0]
    n_workers = SC_CORES * SC_SUBCORES
    per_w = T // n_workers
    ch = SC_CHUNK
    n_ch = per_w // ch
    assert per_w * n_workers == T and n_ch * ch == per_w and n_ch % 2 == 0
    dest4 = dest.reshape(K, n_workers, n_ch, ch).transpose(1, 2, 0, 3)

    @functools.partial(
        pl.kernel, mesh=_sc_mesh(), out_type=jax.ShapeDtypeStruct((n_rows, W), rows.dtype),
        scratch_types=[pltpu.VMEM((n_ch, K, ch), jnp.int32), pltpu.VMEM((2, ch, W), rows.dtype),
                       pltpu.SemaphoreType.DMA((2,)), pltpu.SemaphoreType.DMA((2,))])
    def scatter(rows_hbm, dest_hbm, out_hbm, idx_v, rows_v, read_sem, write_sem):
        wid = _sc_worker()
        base = wid * per_w
        pltpu.sync_copy(dest_hbm.at[wid], idx_v)

        def read(c, b):
            src = rows_hbm.at[pl.ds(pl.multiple_of(base + c * ch, ch), ch)]
            return pltpu.make_async_copy(src, rows_v.at[b], read_sem.at[b])

        def write(c, b, k):
            return pltpu.make_async_copy(rows_v.at[b], out_hbm.at[idx_v.at[c, k]], write_sem.at[b])

        read(0, 0).start()

        @pl.loop(0, n_ch, step=2)
        def _(i):
            for b in range(2):
                c = i + b
                read(c, b).wait()

                @pl.when(c >= 1)
                def _():
                    for k in range(K):
                        write(c - 1, 1 - b, k).wait()

                @pl.when(c + 1 < n_ch)
                def _():
                    read(c + 1, 1 - b).start()

                for k in range(K):
                    write(c, b, k).start()

        for k in range(K):
            write(n_ch - 1, 1, k).wait()

    return scatter(rows, dest4)


def _sc_gather_rows(table, idx):
    N = idx.shape[0]
    W = table.shape[1]
    n_workers = SC_CORES * SC_SUBCORES
    per_w = N // n_workers
    ch = SC_CHUNK
    n_ch = per_w // ch
    assert per_w * n_workers == N and n_ch * ch == per_w and n_ch % 2 == 0
    idx3 = idx.reshape(n_workers, n_ch, ch)

    @functools.partial(
        pl.kernel, mesh=_sc_mesh(), out_type=jax.ShapeDtypeStruct((N, W), table.dtype),
        scratch_types=[pltpu.VMEM((n_ch, ch), jnp.int32), pltpu.VMEM((2, ch, W), table.dtype),
                       pltpu.SemaphoreType.DMA((2,)), pltpu.SemaphoreType.DMA((2,))])
    def gather(table_hbm, idx_hbm, out_hbm, idx_v, rows_v, gather_sem, write_sem):
        wid = _sc_worker()
        base = wid * per_w
        pltpu.sync_copy(idx_hbm.at[wid], idx_v)

        def fetch(c, b):
            return pltpu.make_async_copy(table_hbm.at[idx_v.at[c]], rows_v.at[b], gather_sem.at[b])

        def write(c, b):
            dst = out_hbm.at[pl.ds(pl.multiple_of(base + c * ch, ch), ch)]
            return pltpu.make_async_copy(rows_v.at[b], dst, write_sem.at[b])

        fetch(0, 0).start()

        @pl.loop(0, n_ch, step=2)
        def _(i):
            for b in range(2):
                c = i + b
                fetch(c, b).wait()

                @pl.when(c >= 1)
                def _():
                    write(c - 1, 1 - b).wait()

                @pl.when(c + 1 < n_ch)
                def _():
                    fetch(c + 1, 1 - b).start()

                write(c, b).start()

        write(n_ch - 1, 1).wait()

    return gather(table, idx3)


def _combine_kernel(x1_ref, yg_ref, rt_ref, mod_ref, *rest):
    o_ref = rest[-1]
    gate2 = mod_ref[5:6, :]
    rt = rt_ref[...]
    acc = rt[:, 2 * TOP_K:2 * TOP_K + 1] * _unpack_bf16_pairs(yg_ref[0])
    for k in range(1, TOP_K):
        acc = acc + rt[:, 2 * TOP_K + k:2 * TOP_K + k + 1] * _unpack_bf16_pairs(yg_ref[k])
    o_ref[...] = x1_ref[...] + gate2 * acc


def _combine(x1, yg, rt, mod3, prev, b0):
    B, S, D = x1.shape
    nb = yg.shape[1]
    tm = min(COMBINE_ROWS, S)
    row = lambda w: pl.BlockSpec((None, tm, w), lambda b, s: (b0 + b, s, 0))
    in_specs = [row(D), pl.BlockSpec((TOP_K, None, tm, D // 2), lambda b, s: (0, b, s, 0)), row(LANES),
                pl.BlockSpec((None, 6, D), lambda b, s: (b0 + b, 0, 0))]
    operands = [x1, yg, rt, mod3]
    if prev is not None:
        in_specs.append(pl.BlockSpec(memory_space=pl.ANY))
        operands.append(prev)
    return pl.pallas_call(
        _combine_kernel,
        grid=(nb, S // tm),
        in_specs=in_specs,
        out_specs=row(D),
        out_shape=jax.ShapeDtypeStruct((B, S, D), F32),
        input_output_aliases={} if prev is None else {4: 0},
        compiler_params=_params(("arbitrary", "arbitrary")),
        name="combine",
    )(*operands)


def _overlap_matrix(n_cmp_pad, n_blk):
    cs = np.arange(n_cmp_pad)[:, None] * CMP_STRIDE
    js = np.arange(LANES)[None, :] * SEL_BLOCK
    m = (cs <= js + SEL_BLOCK - 1) & (cs + CMP_BLOCK - 1 >= js) & (np.arange(LANES)[None, :] < n_blk)
    return jnp.asarray(m.astype(np.float32).T)


def _block_onehot(seq):
    key = np.arange(seq)[:, None]
    j = np.arange(LANES)[None, :]
    return jnp.asarray((j == key // SEL_BLOCK).astype(np.float32), dtype=BF16)


def _layer(x, c, ang, ada_w, ada_b, norm1_g, w_in, q_norm_g, k_norm_g, cmp_pe_k, cmp_pe_v, cmp_wk1, cmp_wk2, cmp_wv1,
           cmp_wv2, conv_w, conv_b, lru_wa, lru_ba, lru_wx, lru_bx, lru_lambda, w_branch_out, w_out, norm2_g,
           router_w, router_b, moe_w1, moe_b1, moe_w2, moe_b2):
    B, S, D = x.shape
    T = B * S
    G = N_KV_GROUPS
    nsa_w = N_HEADS * HEAD_DIM
    kv_w = G * HEAD_DIM

    mod3 = _ada(c, ada_w, ada_b).reshape(B, 6, D)

    o = 0
    wq = w_in[:, o:o + nsa_w]; o += nsa_w
    wkc = w_in[:, o:o + 2 * kv_w]; o += 2 * kv_w
    wkr = w_in[:, o:o + 4 * kv_w]; o += 4 * kv_w
    n_gate = N_HEADS * N_NSA_BRANCHES
    wg_raw = w_in[:, o:o + n_gate].reshape(D, G, n_gate // G); o += n_gate
    wg = jnp.pad(wg_raw, ((0, 0), (0, 0), (0, LANES - n_gate // G))).reshape(D, G * LANES)
    wlx = w_in[:, o:o + D]; o += D
    wlg = w_in[:, o:o + D]; o += D
    wm = w_in[:, o:o + 2 * D]
    bf = lambda a: a.astype(BF16)

    qn, qr, kvc, kvr, gates, lx, lg, mg = _in_proj(x, mod3, norm1_g, ang, q_norm_g, k_norm_g, bf(wq), bf(wkc), bf(wkr),
                                                   bf(wg), bf(wlx), bf(wlg), bf(wm))

    nc = S // CMP_STRIDE
    pe = lambda p: jnp.broadcast_to(p.reshape(1, CMP_BLOCK * HEAD_DIM), (SUBLANES, CMP_BLOCK * HEAD_DIM)).astype(BF16)
    kc, vc = _compress(kvc, pe(cmp_pe_k), pe(cmp_pe_v), bf(cmp_wk1), bf(cmp_wk2), bf(cmp_wv1), bf(cmp_wv2), k_norm_g)

    ocmp, bias = _select(qn, kc, vc, gates, _overlap_matrix(nc, S // SEL_BLOCK), S // SEL_BLOCK)
    o_nsa = _nsa(qr, bias, ocmp, kvr, gates, _block_onehot(S))
    o_lru = _lru(lx, lg, conv_w, conv_b, bf(lru_wa), lru_ba, bf(lru_wx), lru_bx, lru_lambda)

    tm = min(IN_ROWS, S)
    tri = jnp.asarray(np.triu(np.ones((tm, tm), np.float32), 1), dtype=BF16)
    rw = jnp.pad(router_w.T, ((0, LANES - N_EXPERTS), (0, 0)))
    rb = router_b.reshape(N_EXPERTS, 1)
    x1, h2, rt, cnt = _merge(o_nsa, o_lru, mg, x, mod3, norm2_g, bf(w_branch_out[:nsa_w]), bf(w_branch_out[nsa_w:]),
                             bf(w_out), rw, rb, tri)

    rt2 = rt.reshape(T, LANES)
    top_e = rt2[:, 0:TOP_K].astype(jnp.int32)
    rank = rt2[:, TOP_K:2 * TOP_K].astype(jnp.int32)
    counts = cnt[:, 0].astype(jnp.int32)
    padded = (counts + EXPERT_ROWS - 1) // EXPERT_ROWS * EXPERT_ROWS
    ends = jnp.cumsum(padded)
    start = ends - padded
    dest = start[top_e] + rank
    n_rows = -(-(T * TOP_K + N_EXPERTS * (EXPERT_ROWS - 1)) // EXPERT_ROWS) * EXPERT_ROWS
    n_blocks = n_rows // EXPERT_ROWS
    blk_first = jnp.arange(n_blocks, dtype=jnp.int32) * EXPERT_ROWS
    blk_e = jnp.minimum(jnp.sum(ends[None, :] <= blk_first[:, None], axis=1), N_EXPERTS - 1).astype(jnp.int32)
    n_used = (ends[-1:] // EXPERT_ROWS).astype(jnp.int32)

    dest_kt = dest.T
    buf = _sc_scatter_rows(h2.reshape(T, D // 2), dest_kt, n_rows)

    b1 = jnp.concatenate([moe_b1[:, 0::2], moe_b1[:, 1::2]], axis=1)
    ybuf = _experts(blk_e, n_used, buf, moe_w1, b1.reshape(N_EXPERTS, 1, -1), moe_w2, moe_b2.reshape(N_EXPERTS, 1, D))
    nb = B // COMBINE_PARTS if B % COMBINE_PARTS == 0 else B
    dest_parts = dest_kt.reshape(TOP_K, B // nb, nb * S)
    out = None
    for p in range(B // nb):
        yg = _sc_gather_rows(ybuf, dest_parts[:, p].reshape(-1)).reshape(TOP_K, nb, S, D // 2)
        out = _combine(x1, yg, rt, mod3, out, p * nb)
    return out


def kernel(x, c, positions, ada_w, ada_b, norm1_g, w_in, q_norm_g, k_norm_g, cmp_pe_k, cmp_pe_v, cmp_wk1, cmp_wk2, cmp_wv1, cmp_wv2, conv_w, conv_b, lru_wa, lru_ba, lru_wx, lru_bx, lru_lambda, w_branch_out, w_out, norm2_g, router_w, router_b, moe_w1, moe_b1, moe_w2, moe_b2):
    inv = ROPE_THETA ** (-jnp.arange(0, HEAD_DIM, 2, dtype=F32) / HEAD_DIM)
    ang = positions.astype(F32)[..., None] * inv
    ang = jnp.concatenate([ang, ang], axis=-1)
    for l in range(ada_w.shape[0]):
        x = _layer(x, c, ang, ada_w[l], ada_b[l], norm1_g[l], w_in[l], q_norm_g[l], k_norm_g[l], cmp_pe_k[l],
                   cmp_pe_v[l], cmp_wk1[l], cmp_wk2[l], cmp_wv1[l], cmp_wv2[l], conv_w[l], conv_b[l], lru_wa[l],
                   lru_ba[l], lru_wx[l], lru_bx[l], lru_lambda[l], w_branch_out[l], w_out[l], norm2_g[l], router_w[l],
                   router_b[l], moe_w1[l], moe_b1[l], moe_w2[l], moe_b2[l])
    return x
```

```python
import functools
import math

import jax
import jax.numpy as jnp
import numpy as np
from jax import lax
from jax.experimental import pallas as pl
from jax.experimental.pallas import tpu as pltpu
from jax.experimental.pallas import tpu_sc as plsc

F32 = jnp.float32
BF16 = jnp.bfloat16

N_HEADS = 8
HEAD_DIM = 128
N_KV_GROUPS = 2
HEADS_PER_GROUP = N_HEADS // N_KV_GROUPS
N_NSA_BRANCHES = 3
CMP_BLOCK = 32
CMP_STRIDE = 16
CMP_HIDDEN = 256
SEL_BLOCK = 64
N_SELECT = 8
WINDOW = 512
ROPE_THETA = 10000.0
LRU_BLOCKS = 4
CONV_WIDTH = 4
LRU_C = 8.0
N_EXPERTS = 32
TOP_K = 4
SWIGLU_LIMIT = 7.0
SWIGLU_ALPHA = 1.702
RMS_EPS = 1e-6
NEG_INF = -1e30
LOWEST = -3.0e38
LOG2_E = 1.4426950408889634

LANES = 128
SUBLANES = 8
VMEM_LIMIT = 56 * 1024 * 1024
SC_CORES = 2
SC_SUBCORES = 16
SC_CHUNK = 64

Q_TILE = 256
SELECT_ROWS = 2048
KEY_TILE = 128
ONES_ROWS = 16
HEAD_PAIR = 2
IN_ROWS = 512
MERGE_SPLIT = 4
LRU_ROWS = 1024
EXPERT_ROWS = 512
COMBINE_ROWS = 1024
COMBINE_PARTS = 8


def _sigmoid(v):
    return 0.5 * jnp.tanh(0.5 * v) + 0.5


def _gelu_tanh(v):
    c = math.sqrt(2.0 / math.pi)
    half = 0.5 * v
    return half + half * jnp.tanh(v * (c + (c * 0.044715) * (v * v)))


def _rms(v, g):
    return v * lax.rsqrt(jnp.mean(v * v, axis=-1, keepdims=True) + RMS_EPS) * g


def _dot(a, b, **kw):
    return jnp.dot(a, b, preferred_element_type=F32, **kw)


def _dot_nt(a, b, **kw):
    return lax.dot_general(a, b, (((1,), (1,)), ((), ())), preferred_element_type=F32, **kw)


def _pack_bf16_pairs(v):
    n = v.shape[1] // 2
    lo = lax.bitcast_convert_type(v[:, 0:n].astype(BF16).astype(F32), jnp.int32)
    hi = lax.bitcast_convert_type(v[:, n:2 * n].astype(BF16).astype(F32), jnp.int32)
    return lax.shift_right_logical(lo, 16) | hi


def _unpack_bf16_pairs(w):
    lo = lax.bitcast_convert_type(lax.shift_left(w, 16), F32)
    hi = lax.bitcast_convert_type(w & jnp.int32(-65536), F32)
    return jnp.concatenate([lo, hi], axis=1)


def _full(shape):
    nd = len(shape)
    return pl.BlockSpec(shape, lambda *_: (0,) * nd)


def _params(sem):
    return pltpu.CompilerParams(dimension_semantics=sem, vmem_limit_bytes=VMEM_LIMIT)


def _ada_kernel(c_ref, w_ref, b_ref, o_ref):
    c = c_ref[...]
    o_ref[...] = _dot(c * _sigmoid(c), w_ref[...], precision=lax.Precision.HIGHEST) + b_ref[...]


def _ada(c, ada_w, ada_b):
    B, D = c.shape
    N = ada_w.shape[1]
    return pl.pallas_call(
        _ada_kernel,
        grid=(N // D,),
        in_specs=[_full((B, D)), pl.BlockSpec((D, D), lambda j: (0, j)), pl.BlockSpec((1, D), lambda j: (0, j))],
        out_specs=pl.BlockSpec((B, D), lambda j: (0, j)),
        out_shape=jax.ShapeDtypeStruct((B, N), F32),
        compiler_params=_params(("arbitrary",)),
        name="ada",
    )(c, ada_w, ada_b.reshape(1, N))


def _in_kernel(x_ref, mod_ref, g1_ref, ang_ref, qg_ref, kg_ref, wq_ref, wkc_ref, wkr_ref, wg_ref, wlx_ref, wlg_ref,
               wm_ref, qn_ref, qr_ref, kvc_ref, kvr_ref, gt_ref, lx_ref, lg_ref, mg_ref, kvc_sc):
    x = x_ref[...]
    shift1 = mod_ref[0:1, :]
    scale1 = mod_ref[1:2, :]
    h = _rms(x, g1_ref[...]) * (1.0 + scale1) + shift1
    hb = h.astype(BF16)

    ang = ang_ref[...]
    cos = jnp.cos(ang)
    sin = jnp.sin(ang)
    lane = lax.broadcasted_iota(jnp.int32, ang.shape, 1)
    sin_signed = jnp.where(lane < HEAD_DIM // 2, -sin, sin)

    def rope(v):
        return v * cos + pltpu.roll(v, HEAD_DIM // 2, 1) * sin_signed

    q = _dot(hb, wq_ref[...])
    for hh in range(N_HEADS):
        sl = slice(hh * HEAD_DIM, (hh + 1) * HEAD_DIM)
        qh = _rms(q[:, sl], qg_ref[...])
        qn_ref[:, sl] = (qh * HEAD_DIM ** -0.5).astype(BF16)
        qr_ref[:, sl] = (rope(qh) * (HEAD_DIM ** -0.5 * LOG2_E)).astype(BF16)

    kvc = _dot(hb, wkc_ref[...])
    for part in range(kvc_ref.shape[0]):
        kvc_sc[part] = kvc[:, part * HEAD_DIM:(part + 1) * HEAD_DIM]
        for tok in range(CMP_STRIDE):
            piece = kvc_sc[part, pl.ds(tok, kvc_ref.shape[1], stride=CMP_STRIDE), :]
            kvc_ref[part, :, tok * HEAD_DIM:(tok + 1) * HEAD_DIM] = piece.astype(BF16)

    kvr = _dot(hb, wkr_ref[...])
    kvw = N_KV_GROUPS * HEAD_DIM
    for part in range(4):
        for gg in range(N_KV_GROUPS):
            sl = slice(part * kvw + gg * HEAD_DIM, part * kvw + (gg + 1) * HEAD_DIM)
            v = kvr[:, sl]
            if part % 2 == 0:
                row = 1 + part // 2
                v = rope(_rms(v, kg_ref[row:row + 1, :]))
            kvr_ref[:, sl] = v.astype(BF16)

    gt_ref[...] = _sigmoid(_dot(hb, wg_ref[...]))
    lx_ref[...] = _dot(hb, wlx_ref[...]).astype(BF16)
    lg_ref[...] = _dot(hb, wlg_ref[...]).astype(BF16)
    mg_ref[...] = _sigmoid(_dot(hb, wm_ref[...])).astype(BF16)


def _in_proj(x, mod3, norm1_g, ang, q_norm_g, k_norm_g, wq, wkc, wkr, wg, wlx, wlg, wm):
    B, S, D = x.shape
    tm = min(IN_ROWS, S)
    row = lambda w: pl.BlockSpec((None, tm, w), lambda b, s: (b, s, 0))
    widths = (wq.shape[1], wq.shape[1], wkc.shape[1], wkr.shape[1], wg.shape[1], wlx.shape[1], wlg.shape[1], wm.shape[1])
    dtypes = (BF16, BF16, BF16, BF16, F32, BF16, BF16, BF16)
    weights = (wq, wkc, wkr, wg, wlx, wlg, wm)
    out_specs = [row(w) for w in widths]
    out_shape = [jax.ShapeDtypeStruct((B, S, w), dt) for w, dt in zip(widths, dtypes)]
    n_part = wkc.shape[1] // HEAD_DIM
    out_specs[2] = pl.BlockSpec((None, n_part, tm // CMP_STRIDE, CMP_STRIDE * HEAD_DIM), lambda b, s: (b, 0, s, 0))
    out_shape[2] = jax.ShapeDtypeStruct((B, n_part, S // CMP_STRIDE, CMP_STRIDE * HEAD_DIM), BF16)
    return pl.pallas_call(
        _in_kernel,
        grid=(B, S // tm),
        in_specs=[row(D), pl.BlockSpec((None, 6, D), lambda b, s: (b, 0, 0)), _full((1, D)), row(HEAD_DIM),
                  _full((1, HEAD_DIM)), _full(k_norm_g.shape)] + [_full(w.shape) for w in weights],
        out_specs=out_specs,
        out_shape=out_shape,
        scratch_shapes=[pltpu.VMEM((n_part, tm, HEAD_DIM), F32)],
        compiler_params=_params(("arbitrary", "arbitrary")),
        name="in_proj",
    )(x, mod3, norm1_g.reshape(1, D), ang, q_norm_g.reshape(1, HEAD_DIM), k_norm_g, *weights)


def _cmp_kernel(ak_ref, av_ref, pek_ref, pev_ref, wk1_ref, wk2_ref, wv1_ref, wv2_ref, kg_ref, kc_ref, vc_ref):
    half = CMP_STRIDE * HEAD_DIM

    def compress(a_ref, pe_ref, w1_ref, w2_ref):
        a = a_ref[...]
        u = _dot(a, w1_ref[0:half, :])
        v = _dot(a, w1_ref[half:2 * half, :])
        pw = _dot(pe_ref[...], w1_ref[...])
        pre = u + pltpu.roll(v, v.shape[0] - 1, 0) + pw[0:1, :]
        return _dot(_gelu_tanh(pre).astype(BF16), w2_ref[...])

    kc_ref[...] = _rms(compress(ak_ref, pek_ref, wk1_ref, wk2_ref), kg_ref[0:1, :]).astype(BF16)
    vc_ref[...] = compress(av_ref, pev_ref, wv1_ref, wv2_ref).T.astype(BF16)


def _compress(a, pe_k, pe_v, wk1, wk2, wv1, wv2, k_norm_g):
    B, _, NC, W = a.shape
    G = N_KV_GROUPS
    out = pl.BlockSpec((None, None, NC, HEAD_DIM), lambda b, g: (b, g, 0, 0))
    consts = (pe_k, pe_v, wk1, wk2, wv1, wv2, k_norm_g)
    return pl.pallas_call(
        _cmp_kernel,
        grid=(B, G),
        in_specs=[pl.BlockSpec((None, None, NC, W), lambda b, g: (b, g, 0, 0)),
                  pl.BlockSpec((None, None, NC, W), lambda b, g: (b, G + g, 0, 0))] + [_full(c.shape) for c in consts],
        out_specs=[out, out],
        out_shape=[jax.ShapeDtypeStruct((B, G, NC, HEAD_DIM), BF16)] * 2,
        compiler_params=_params(("arbitrary", "arbitrary")),
        name="compress",
    )(a, a, *consts)


def _select_kernel(qn_ref, kc_ref, vc_ref, gt_ref, ov_ref, ocmp_ref, bias_ref, *, n_blk):
    qi = pl.program_id(2)
    tq = qn_ref.shape[0]
    hpg = HEADS_PER_GROUP
    qn = jnp.concatenate([qn_ref[:, h * HEAD_DIM:(h + 1) * HEAD_DIM] for h in range(hpg)], axis=0)
    row = lax.broadcasted_iota(jnp.int32, (kc_ref.shape[0], tq), 0)
    t_q = qi * tq + lax.broadcasted_iota(jnp.int32, (kc_ref.shape[0], tq), 1)

    cmask = jnp.concatenate([row * CMP_STRIDE + (CMP_BLOCK - 1) <= t_q] * hpg, axis=1)
    s = jnp.where(cmask, _dot_nt(kc_ref[...], qn), NEG_INF)
    e = jnp.exp(s - jnp.max(s, axis=0, keepdims=True))
    p = e * jnp.where(cmask[0:1, :], 1.0 / jnp.sum(e, axis=0, keepdims=True), 0.0)
    o_cmp = _dot(vc_ref[...], p.astype(BF16))
    gt = gt_ref[...]
    for h in range(hpg):
        c0 = h * N_NSA_BRANCHES
        ocmp_ref[:, h * HEAD_DIM:(h + 1) * HEAD_DIM] = (gt[:, c0:c0 + 1] * o_cmp[:, h * tq:(h + 1) * tq].T).astype(BF16)

    psum = p[:, 0:tq]
    for h in range(1, hpg):
        psum = psum + p[:, h * tq:(h + 1) * tq]
    imp = _dot(ov_ref[...], psum, precision=lax.Precision.HIGHEST)[0:n_blk]
    blk = lax.broadcasted_iota(jnp.int32, (n_blk, tq), 0).astype(F32)
    cur = ((qi * tq + lax.broadcasted_iota(jnp.int32, (n_blk, tq), 1)) // SEL_BLOCK).astype(F32)
    forced = (blk == 0.0) | (blk == cur) | (blk == cur - 1.0)
    score = jnp.where(forced, 1e6, jnp.where(blk <= cur, imp, -1e6))
    bias = jnp.full((n_blk, tq), NEG_INF, F32)
    for _ in range(min(N_SELECT, n_blk)):
        best = jnp.max(score, axis=0, keepdims=True)
        idx = jnp.min(jnp.where(score == best, blk, float(LANES)), axis=0, keepdims=True)
        hit = blk == idx
        bias = jnp.where(hit, 0.0, bias)
        score = jnp.where(hit, LOWEST, score)
    bias = jnp.concatenate([bias, jnp.zeros((LANES - n_blk, tq), F32)], axis=0)
    bias_ref[...] = bias.T.astype(BF16)


def _select(qn, kc, vc, gates, overlap, n_blk):
    B, S, _ = qn.shape
    G = N_KV_GROUPS
    tq = min(SELECT_ROWS, S)
    gw = HEADS_PER_GROUP * HEAD_DIM
    qspec = pl.BlockSpec((None, tq, gw), lambda b, g, i: (b, i, g))
    lane_spec = pl.BlockSpec((None, tq, LANES), lambda b, g, i: (b, i, g))
    cspec = pl.BlockSpec((None, None) + kc.shape[2:], lambda b, g, i: (b, g, 0, 0))
    return pl.pallas_call(
        functools.partial(_select_kernel, n_blk=n_blk),
        grid=(B, G, S // tq),
        in_specs=[qspec, cspec, cspec, lane_spec, _full(overlap.shape)],
        out_specs=[qspec, lane_spec],
        out_shape=[jax.ShapeDtypeStruct(qn.shape, BF16), jax.ShapeDtypeStruct((B, S, G * LANES), BF16)],
        compiler_params=_params(("arbitrary", "arbitrary", "arbitrary")),
        name="select",
    )(qn, kc, vc, gates, overlap)


def _nsa_kernel(qr_ref, bias_ref, ocmp_ref, ks_ref, vs_ref, kw_ref, vw_ref, gt_ref, ex_ref, o_ref, vst_sc, vwt_sc):
    qi = pl.program_id(2)
    tq = qr_ref.shape[0]
    hpg = HEADS_PER_GROUP
    seq = ks_ref.shape[0]

    @pl.when(qi == 0)
    def _():
        for kt in range(seq // KEY_TILE):
            rows = slice(kt * KEY_TILE, (kt + 1) * KEY_TILE)
            vst_sc[0:HEAD_DIM, rows] = vs_ref[rows, :].astype(F32).T.astype(BF16)
            vwt_sc[0:HEAD_DIM, rows] = vw_ref[rows, :].astype(F32).T.astype(BF16)
        ones = jnp.ones((vst_sc.shape[0] - HEAD_DIM, seq), BF16)
        vst_sc[HEAD_DIM:, :] = ones
        vwt_sc[HEAD_DIM:, :] = ones

    def pair(v):
        return jnp.concatenate([v] * HEAD_PAIR, axis=1)

    qr = jnp.concatenate([qr_ref[:, h * HEAD_DIM:(h + 1) * HEAD_DIM] for h in range(hpg)], axis=0)
    q_aug = jnp.concatenate([qr, jnp.concatenate([bias_ref[...]] * hpg, axis=0)], axis=1)
    pair_cols = [slice(hp * HEAD_PAIR * tq, (hp + 1) * HEAD_PAIR * tq) for hp in range(hpg // HEAD_PAIR)]

    def softmax_pv(sc, vt):
        pr = jnp.exp2((sc - jnp.max(sc, axis=0, keepdims=True)).astype(BF16))
        o = _dot(vt, pr)
        return o[0:HEAD_DIM] * (1.0 / o[HEAD_DIM:HEAD_DIM + 1])

    def biased(sc, first_bias, last_bias):
        n = sc.shape[0]
        if first_bias is not None:
            sc = jnp.concatenate([sc[0:tq] + first_bias, sc[tq:n]], axis=0)
        return jnp.concatenate([sc[0:n - tq], sc[n - tq:n] + last_bias], axis=0) if n > tq else sc + last_bias

    r_in = lax.broadcasted_iota(jnp.int32, (tq, tq), 0)
    c_in = lax.broadcasted_iota(jnp.int32, (tq, tq), 1)
    causal = pair(jnp.where(r_in <= c_in, 0.0, NEG_INF))
    window_tail = pair(jnp.where(r_in > c_in, 0.0, NEG_INF))
    gt = gt_ref[...].T

    for v in range(seq // tq):
        @pl.when(qi == v)
        def _():
            ext = (v + 1) * tq
            k_aug = jnp.concatenate([ks_ref[0:ext, :], ex_ref[0:ext, :]], axis=1)
            w_lo = max(v * tq - WINDOW, 0)
            tail = window_tail if v * tq - WINDOW >= 0 else None
            k_win = kw_ref[w_lo:ext, :]
            scores = [biased(_dot_nt(k_aug, q_aug[cols]), None, causal) for cols in pair_cols]
            win_scores = [biased(_dot_nt(k_win, qr[cols]), tail, causal) for cols in pair_cols]
            o_slcs = [softmax_pv(sc, vst_sc[:, 0:ext]) for sc in scores]
            o_wins = [softmax_pv(sc, vwt_sc[:, w_lo:ext]) for sc in win_scores]
            for h in range(hpg):
                hp, j = divmod(h, HEAD_PAIR)
                sub = slice(j * tq, (j + 1) * tq)
                c0 = h * N_NSA_BRANCHES
                o = gt[c0 + 1:c0 + 2, :] * o_slcs[hp][:, sub] + gt[c0 + 2:c0 + 3, :] * o_wins[hp][:, sub]
                hd = slice(h * HEAD_DIM, (h + 1) * HEAD_DIM)
                o_ref[:, hd] = (o.T + ocmp_ref[:, hd].astype(F32)).astype(BF16)


def _nsa(qr, bias, ocmp, kvr, gates, onehot):
    B, S, _ = qr.shape
    G = N_KV_GROUPS
    tq = Q_TILE
    gw = HEADS_PER_GROUP * HEAD_DIM
    qspec = pl.BlockSpec((None, tq, gw), lambda b, g, i: (b, i, g))
    lane_spec = pl.BlockSpec((None, tq, LANES), lambda b, g, i: (b, i, g))
    kv = lambda part: pl.BlockSpec((None, S, HEAD_DIM), lambda b, g, i: (b, 0, part * G + g))
    return pl.pallas_call(
        _nsa_kernel,
        grid=(B, G, S // tq),
        in_specs=[qspec, lane_spec, qspec, kv(0), kv(1), kv(2), kv(3), lane_spec, _full(onehot.shape)],
        out_specs=qspec,
        out_shape=jax.ShapeDtypeStruct(qr.shape, BF16),
        scratch_shapes=[pltpu.VMEM((HEAD_DIM + ONES_ROWS, S), BF16), pltpu.VMEM((HEAD_DIM + ONES_ROWS, S), BF16)],
        compiler_params=_params(("arbitrary", "arbitrary", "arbitrary")),
        name="nsa",
    )(qr, bias, ocmp, kvr, kvr, kvr, kvr, gates, onehot)


def _lru_kernel(lx_ref, lg_ref, cw_ref, cb_ref, wa_ref, ba_ref, wx_ref, bx_ref, lam_ref, o_ref,
                xs_sc, a_sc, u_sc, h_sc, carry_sc):
    ts = lx_ref.shape[0]
    width = lx_ref.shape[1]
    bw = width // LRU_BLOCKS

    @pl.when(pl.program_id(1) == 0)
    def _():
        xs_sc[0:SUBLANES, :] = jnp.zeros((SUBLANES, width), F32)
        carry_sc[...] = jnp.zeros(carry_sc.shape, F32)

    xs_sc[SUBLANES:SUBLANES + ts, :] = lx_ref[...].astype(F32)
    xc = cb_ref[...] + cw_ref[CONV_WIDTH - 1:CONV_WIDTH, :] * xs_sc[SUBLANES:SUBLANES + ts, :]
    for d in range(1, CONV_WIDTH):
        w = cw_ref[CONV_WIDTH - 1 - d:CONV_WIDTH - d, :]
        xc = xc + w * xs_sc[SUBLANES - d:SUBLANES - d + ts, :]
    xs_sc[0:SUBLANES, :] = xs_sc[ts:ts + SUBLANES, :]

    lam = -lam_ref[...]
    neg_c_softplus = -LRU_C * (jnp.maximum(lam, 0.0) + jnp.log1p(jnp.exp(-jnp.abs(lam))))
    xcb = xc.astype(BF16)
    for blk in range(LRU_BLOCKS):
        sl = slice(blk * bw, (blk + 1) * bw)
        r = _sigmoid(_dot(xcb[:, sl], wa_ref[blk]) + ba_ref[:, sl])
        i = _sigmoid(_dot(xcb[:, sl], wx_ref[blk]) + bx_ref[:, sl])
        log_a = r * neg_c_softplus[:, sl]
        a = jnp.exp(log_a)
        a_sc[:, sl] = a
        u_sc[:, sl] = jnp.sqrt(-jnp.tanh(log_a) * (a * a + 1.0)) * (i * xc[:, sl])

    row = lax.broadcasted_iota(jnp.int32, (SUBLANES, width), 0)

    def chunk(c, h_prev):
        r0 = pl.multiple_of(c * SUBLANES, SUBLANES)
        a = a_sc[pl.ds(r0, SUBLANES), :]
        u = u_sc[pl.ds(r0, SUBLANES), :]
        for d in (1, 2, 4):
            keep = row >= d
            u = jnp.where(keep, a * pltpu.roll(u, d, 0) + u, u)
            a = jnp.where(keep, a * pltpu.roll(a, d, 0), a)
        h = a * h_prev + u
        h_sc[pl.ds(r0, SUBLANES), :] = h
        return jnp.broadcast_to(h[SUBLANES - 1:SUBLANES, :], h.shape)

    carry_sc[...] = lax.fori_loop(0, ts // SUBLANES, chunk, carry_sc[...], unroll=2)
    o_ref[...] = (_gelu_tanh(lg_ref[...].astype(F32)) * h_sc[...]).astype(BF16)


def _lru(lx, lg, conv_w, conv_b, wa, ba, wx, bx, lam):
    B, S, W = lx.shape
    ts = min(LRU_ROWS, S)
    row = pl.BlockSpec((None, ts, W), lambda b, s: (b, s, 0))
    consts = (conv_w, conv_b.reshape(1, W), wa, ba.reshape(1, W), wx, bx.reshape(1, W), lam.reshape(1, W))
    return pl.pallas_call(
        _lru_kernel,
        grid=(B, S // ts),
        in_specs=[row, row] + [_full(a.shape) for a in consts],
        out_specs=row,
        out_shape=jax.ShapeDtypeStruct((B, S, W), BF16),
        scratch_shapes=[pltpu.VMEM((ts + SUBLANES, W), F32), pltpu.VMEM((ts, W), F32), pltpu.VMEM((ts, W), F32),
                        pltpu.VMEM((ts, W), F32), pltpu.VMEM((SUBLANES, W), F32)],
        compiler_params=_params(("arbitrary", "arbitrary")),
        name="lru",
    )(lx, lg, *consts)


def _merge_kernel(on_ref, ol_ref, mg_ref, x_ref, mod_ref, g2_ref, wn_ref, wl_ref, wo_ref, rw_ref, rb_ref, tri_ref,
                  x1_ref, h2_ref, rt_ref, cnt_ref, carry_sc):
    first = (pl.program_id(0) == 0) & (pl.program_id(1) == 0)

    @pl.when(first)
    def _():
        carry_sc[...] = jnp.zeros(carry_sc.shape, F32)

    d = x_ref.shape[1]
    gate1 = mod_ref[2:3, :]
    shift2 = mod_ref[3:4, :]
    scale2 = mod_ref[4:5, :]
    rw = rw_ref[...]
    rw_hi = rw.astype(BF16)
    rw_lo = (rw - rw_hi.astype(F32)).astype(BF16)

    n_rows = x_ref.shape[0] // MERGE_SPLIT
    groups = [slice(i * n_rows, (i + 1) * n_rows) for i in range(MERGE_SPLIT)]
    y_nsa = [_dot(on_ref[r, :], wn_ref[...]) for r in groups]
    y_lru = [_dot(ol_ref[r, :], wl_ref[...]) for r in groups]
    merged = [mg_ref[r, 0:d].astype(F32) * yn + mg_ref[r, d:2 * d].astype(F32) * yl
              for r, yn, yl in zip(groups, y_nsa, y_lru)]
    x1 = [x_ref[r, :] + gate1 * _dot(m.astype(BF16), wo_ref[...]) for r, m in zip(groups, merged)]
    h2 = [_rms(v, g2_ref[...]) * (1.0 + scale2) + shift2 for v in x1]
    logit_parts = []
    for r, v1, v2 in zip(groups, x1, h2):
        x1_ref[r, :] = v1
        h2_ref[r, :] = _pack_bf16_pairs(v2)
        h2_hi = v2.astype(BF16)
        h2_lo = (v2 - h2_hi.astype(F32)).astype(BF16)
        logit_parts.append(_dot_nt(rw_hi, h2_hi) + (_dot_nt(rw_hi, h2_lo) + _dot_nt(rw_lo, h2_hi)))
    logits = jnp.concatenate(logit_parts, axis=1)[0:N_EXPERTS] + rb_ref[...]
    tm = logits.shape[1]
    eid = lax.broadcasted_iota(jnp.int32, logits.shape, 0).astype(F32)
    score = logits
    picks = []
    onehot = jnp.zeros(logits.shape, F32)
    for _ in range(TOP_K):
        best = jnp.max(score, axis=0, keepdims=True)
        idx = jnp.min(jnp.where(score == best, eid, float(LANES)), axis=0, keepdims=True)
        hit = eid == idx
        picks.append((idx, best, hit))
        onehot = jnp.where(hit, 1.0, onehot)
        score = jnp.where(hit, LOWEST, score)
    ew = [jnp.exp(v - picks[0][1]) for _, v, _ in picks]
    den = ew[0]
    for v in ew[1:]:
        den = den + v
    inv_den = 1.0 / den

    before = _dot(onehot.astype(BF16), tri_ref[...]) + carry_sc[:, 0:1]
    carry_sc[...] = carry_sc[...] + jnp.sum(onehot, axis=1, keepdims=True)
    cnt_ref[...] = carry_sc[...]

    rows = [idx for idx, _, _ in picks]
    rows += [jnp.sum(jnp.where(hit, before, 0.0), axis=0, keepdims=True) for _, _, hit in picks]
    rows += [e * inv_den for e in ew]
    rows.append(jnp.zeros((LANES - len(rows), tm), F32))
    rt_ref[...] = jnp.concatenate(rows, axis=0).T


def _merge(o_nsa, o_lru, mg, x, mod3, norm2_g, wn, wl, wo, rw, rb, tri):
    B, S, D = x.shape
    tm = tri.shape[0]
    row = lambda w: pl.BlockSpec((None, tm, w), lambda b, s: (b, s, 0))
    consts = (norm2_g.reshape(1, D), wn, wl, wo, rw, rb, tri)
    return pl.pallas_call(
        _merge_kernel,
        grid=(B, S // tm),
        in_specs=[row(D), row(D), row(2 * D), row(D), pl.BlockSpec((None, 6, D), lambda b, s: (b, 0, 0))]
                 + [_full(a.shape) for a in consts],
        out_specs=[row(D), row(D // 2), row(LANES), _full((N_EXPERTS, LANES))],
        out_shape=[jax.ShapeDtypeStruct((B, S, D), F32), jax.ShapeDtypeStruct((B, S, D // 2), jnp.int32),
                   jax.ShapeDtypeStruct((B, S, LANES), F32), jax.ShapeDtypeStruct((N_EXPERTS, LANES), F32)],
        scratch_shapes=[pltpu.VMEM((N_EXPERTS, LANES), F32)],
        compiler_params=_params(("arbitrary", "arbitrary")),
        name="merge",
    )(o_nsa, o_lru, mg, x, mod3, *consts)


def _expert_kernel(be_ref, nb_ref, x_ref, w1_ref, b1_ref, w2_ref, b2_ref, pick_ref, y_ref, w1_sc, w2_sc):
    i = pl.program_id(0)
    ff = w2_ref.shape[0]
    chunk = pick_ref.shape[0]

    @pl.when((i == 0) | (be_ref[i] != be_ref[jnp.maximum(i - 1, 0)]))
    def _():
        for c in range(2 * ff // chunk):
            r = _dot(w1_ref[:, c * chunk:(c + 1) * chunk].astype(BF16), pick_ref[...])
            half = chunk // 2
            w1_sc[:, c * half:(c + 1) * half] = r[:, 0:half].astype(BF16)
            w1_sc[:, ff + c * half:ff + (c + 1) * half] = r[:, half:chunk].astype(BF16)
        w2_sc[...] = w2_ref[...].astype(BF16)

    @pl.when(i < nb_ref[0])
    def _():
        h = _dot(_unpack_bf16_pairs(x_ref[...]).astype(BF16), w1_sc[...]) + b1_ref[...]
        x_glu = jnp.minimum(h[:, 0:ff], SWIGLU_LIMIT)
        x_lin = jnp.clip(h[:, ff:2 * ff], -SWIGLU_LIMIT, SWIGLU_LIMIT)
        act = x_glu * _sigmoid(SWIGLU_ALPHA * x_glu) * (x_lin + 1.0)
        y_ref[...] = _pack_bf16_pairs(_dot(act.astype(BF16), w2_sc[...]) + b2_ref[...])

    @pl.when(i >= nb_ref[0])
    def _():
        y_ref[...] = jnp.zeros(y_ref.shape, jnp.int32)


def _experts(blk_e, n_used, buf, w1, b1, w2, b2):
    n_rows = buf.shape[0]
    F, D = w2.shape[1], w2.shape[2]
    tm = EXPERT_ROWS
    chunk = 2 * LANES
    pick = np.zeros((chunk, chunk), np.float32)
    pick[2 * np.arange(LANES), np.arange(LANES)] = 1.0
    pick[2 * np.arange(LANES) + 1, LANES + np.arange(LANES)] = 1.0
    wspec = lambda a, b: pl.BlockSpec((None, a, b), lambda i, be, nb: (be[i], 0, 0))
    return pl.pallas_call(
        _expert_kernel,
        grid_spec=pltpu.PrefetchScalarGridSpec(
            num_scalar_prefetch=2,
            grid=(n_rows // tm,),
            in_specs=[pl.BlockSpec((tm, D // 2), lambda i, be, nb: (i, 0)), wspec(D, 2 * F), wspec(1, 2 * F), wspec(F, D),
                      wspec(1, D), pl.BlockSpec((chunk, chunk), lambda i, be, nb: (0, 0))],
            out_specs=pl.BlockSpec((tm, D // 2), lambda i, be, nb: (i, 0)),
            scratch_shapes=[pltpu.VMEM((D, 2 * F), BF16), pltpu.VMEM((F, D), BF16)],
        ),
        out_shape=jax.ShapeDtypeStruct((n_rows, D // 2), jnp.int32),
        compiler_params=_params(("arbitrary",)),
        name="experts",
    )(blk_e, n_used, buf, w1, b1, w2, b2, jnp.asarray(pick, dtype=BF16))


def _sc_mesh():
    return plsc.VectorSubcoreMesh(core_axis_name="c", subcore_axis_name="s")


def _sc_worker():
    return lax.axis_index("s") * SC_CORES + lax.axis_index("c")


def _sc_scatter_rows(rows, dest, n_rows):
    T, W = rows.shape
    K = dest.shape[0]
    n_workers = SC_CORES * SC_SUBCORES
    per_w = T // n_workers
    ch = SC_CHUNK
    n_ch = per_w // ch
    assert per_w * n_workers == T and n_ch * ch == per_w and n_ch % 2 == 0
    dest4 = dest.reshape(K, n_workers, n_ch, ch).transpose(1, 2, 0, 3)

    @functools.partial(
        pl.kernel, mesh=_sc_mesh(), out_type=jax.ShapeDtypeStruct((n_rows, W), rows.dtype),
        scratch_types=[pltpu.VMEM((n_ch, K, ch), jnp.int32), pltpu.VMEM((2, ch, W), rows.dtype),
                       pltpu.SemaphoreType.DMA((2,)), pltpu.SemaphoreType.DMA((2,))])
    def scatter(rows_hbm, dest_hbm, out_hbm, idx_v, rows_v, read_sem, write_sem):
        wid = _sc_worker()
        base = wid * per_w
        pltpu.sync_copy(dest_hbm.at[wid], idx_v)

        def read(c, b):
            src = rows_hbm.at[pl.ds(pl.multiple_of(base + c * ch, ch), ch)]
            return pltpu.make_async_copy(src, rows_v.at[b], read_sem.at[b])

        def write(c, b, k):
            return pltpu.make_async_copy(rows_v.at[b], out_hbm.at[idx_v.at[c, k]], write_sem.at[b])

        read(0, 0).start()

        @pl.loop(0, n_ch, step=2)
        def _(i):
            for b in range(2):
                c = i + b
                read(c, b).wait()

                @pl.when(c >= 1)
                def _():
                    for k in range(K):
                        write(c - 1, 1 - b, k).wait()

                @pl.when(c + 1 < n_ch)
                def _():
                    read(c + 1, 1 - b).start()

                for k in range(K):
                    write(c, b, k).start()

        for k in range(K):
            write(n_ch - 1, 1, k).wait()

    return scatter(rows, dest4)


def _sc_gather_rows(table, idx):
    N = idx.shape[0]
    W = table.shape[1]
    n_workers = SC_CORES * SC_SUBCORES
    per_w = N // n_workers
    ch = SC_CHUNK
    n_ch = per_w // ch
    assert per_w * n_workers == N and n_ch * ch == per_w and n_ch % 2 == 0
    idx3 = idx.reshape(n_workers, n_ch, ch)

    @functools.partial(
        pl.kernel, mesh=_sc_mesh(), out_type=jax.ShapeDtypeStruct((N, W), table.dtype),
        scratch_types=[pltpu.VMEM((n_ch, ch), jnp.int32), pltpu.VMEM((2, ch, W), table.dtype),
                       pltpu.SemaphoreType.DMA((2,)), pltpu.SemaphoreType.DMA((2,))])
    def gather(table_hbm, idx_hbm, out_hbm, idx_v, rows_v, gather_sem, write_sem):
        wid = _sc_worker()
        base = wid * per_w
        pltpu.sync_copy(idx_hbm.at[wid], idx_v)

        def fetch(c, b):
            return pltpu.make_async_copy(table_hbm.at[idx_v.at[c]], rows_v.at[b], gather_sem.at[b])

        def write(c, b):
            dst = out_hbm.at[pl.ds(pl.multiple_of(base + c * ch, ch), ch)]
            return pltpu.make_async_copy(rows_v.at[b], dst, write_sem.at[b])

        fetch(0, 0).start()

        @pl.loop(0, n_ch, step=2)
        def _(i):
            for b in range(2):
                c = i + b
                fetch(c, b).wait()

                @pl.when(c >= 1)
                def _():
                    write(c - 1, 1 - b).wait()

                @pl.when(c + 1 < n_ch)
                def _():
                    fetch(c + 1, 1 - b).start()

                write(c, b).start()

        write(n_ch - 1, 1).wait()

    return gather(table, idx3)


def _combine_kernel(x1_ref, yg_ref, rt_ref, mod_ref, *rest):
    o_ref = rest[-1]
    gate2 = mod_ref[5:6, :]
    rt = rt_ref[...]
    acc = rt[:, 2 * TOP_K:2 * TOP_K + 1] * _unpack_bf16_pairs(yg_ref[0])
    for k in range(1, TOP_K):
        acc = acc + rt[:, 2 * TOP_K + k:2 * TOP_K + k + 1] * _unpack_bf16_pairs(yg_ref[k])
    o_ref[...] = x1_ref[...] + gate2 * acc


def _combine(x1, yg, rt, mod3, prev, b0):
    B, S, D = x1.shape
    nb = yg.shape[1]
    tm = min(COMBINE_ROWS, S)
    row = lambda w: pl.BlockSpec((None, tm, w), lambda b, s: (b0 + b, s, 0))
    in_specs = [row(D), pl.BlockSpec((TOP_K, None, tm, D // 2), lambda b, s: (0, b, s, 0)), row(LANES),
                pl.BlockSpec((None, 6, D), lambda b, s: (b0 + b, 0, 0))]
    operands = [x1, yg, rt, mod3]
    if prev is not None:
        in_specs.append(pl.BlockSpec(memory_space=pl.ANY))
        operands.append(prev)
    return pl.pallas_call(
        _combine_kernel,
        grid=(nb, S // tm),
        in_specs=in_specs,
        out_specs=row(D),
        out_shape=jax.ShapeDtypeStruct((B, S, D), F32),
        input_output_aliases={} if prev is None else {4: 0},
        compiler_params=_params(("arbitrary", "arbitrary")),
        name="combine",
    )(*operands)


def _overlap_matrix(n_cmp_pad, n_blk):
    cs = np.arange(n_cmp_pad)[:, None] * CMP_STRIDE
    js = np.arange(LANES)[None, :] * SEL_BLOCK
    m = (cs <= js + SEL_BLOCK - 1) & (cs + CMP_BLOCK - 1 >= js) & (np.arange(LANES)[None, :] < n_blk)
    return jnp.asarray(m.astype(np.float32).T)


def _block_onehot(seq):
    key = np.arange(seq)[:, None]
    j = np.arange(LANES)[None, :]
    return jnp.asarray((j == key // SEL_BLOCK).astype(np.float32), dtype=BF16)


def _layer(x, c, ang, ada_w, ada_b, norm1_g, w_in, q_norm_g, k_norm_g, cmp_pe_k, cmp_pe_v, cmp_wk1, cmp_wk2, cmp_wv1,
           cmp_wv2, conv_w, conv_b, lru_wa, lru_ba, lru_wx, lru_bx, lru_lambda, w_branch_out, w_out, norm2_g,
           router_w, router_b, moe_w1, moe_b1, moe_w2, moe_b2):
    B, S, D = x.shape
    T = B * S
    G = N_KV_GROUPS
    nsa_w = N_HEADS * HEAD_DIM
    kv_w = G * HEAD_DIM

    mod3 = _ada(c, ada_w, ada_b).reshape(B, 6, D)

    o = 0
    wq = w_in[:, o:o + nsa_w]; o += nsa_w
    wkc = w_in[:, o:o + 2 * kv_w]; o += 2 * kv_w
    wkr = w_in[:, o:o + 4 * kv_w]; o += 4 * kv_w
    n_gate = N_HEADS * N_NSA_BRANCHES
    wg_raw = w_in[:, o:o + n_gate].reshape(D, G, n_gate // G); o += n_gate
    wg = jnp.pad(wg_raw, ((0, 0), (0, 0), (0, LANES - n_gate // G))).reshape(D, G * LANES)
    wlx = w_in[:, o:o + D]; o += D
    wlg = w_in[:, o:o + D]; o += D
    wm = w_in[:, o:o + 2 * D]
    bf = lambda a: a.astype(BF16)

    qn, qr, kvc, kvr, gates, lx, lg, mg = _in_proj(x, mod3, norm1_g, ang, q_norm_g, k_norm_g, bf(wq), bf(wkc), bf(wkr),
                                                   bf(wg), bf(wlx), bf(wlg), bf(wm))

    nc = S // CMP_STRIDE
    pe = lambda p: jnp.broadcast_to(p.reshape(1, CMP_BLOCK * HEAD_DIM), (SUBLANES, CMP_BLOCK * HEAD_DIM)).astype(BF16)
    kc, vc = _compress(kvc, pe(cmp_pe_k), pe(cmp_pe_v), bf(cmp_wk1), bf(cmp_wk2), bf(cmp_wv1), bf(cmp_wv2), k_norm_g)

    ocmp, bias = _select(qn, kc, vc, gates, _overlap_matrix(nc, S // SEL_BLOCK), S // SEL_BLOCK)
    o_nsa = _nsa(qr, bias, ocmp, kvr, gates, _block_onehot(S))
    o_lru = _lru(lx, lg, conv_w, conv_b, bf(lru_wa), lru_ba, bf(lru_wx), lru_bx, lru_lambda)

    tm = min(IN_ROWS, S)
    tri = jnp.asarray(np.triu(np.ones((tm, tm), np.float32), 1), dtype=BF16)
    rw = jnp.pad(router_w.T, ((0, LANES - N_EXPERTS), (0, 0)))
    rb = router_b.reshape(N_EXPERTS, 1)
    x1, h2, rt, cnt = _merge(o_nsa, o_lru, mg, x, mod3, norm2_g, bf(w_branch_out[:nsa_w]), bf(w_branch_out[nsa_w:]),
                             bf(w_out), rw, rb, tri)

    rt2 = rt.reshape(T, LANES)
    top_e = rt2[:, 0:TOP_K].astype(jnp.int32)
    rank = rt2[:, TOP_K:2 * TOP_K].astype(jnp.int32)
    counts = cnt[:, 0].astype(jnp.int32)
    padded = (counts + EXPERT_ROWS - 1) // EXPERT_ROWS * EXPERT_ROWS
    ends = jnp.cumsum(padded)
    start = ends - padded
    dest = start[top_e] + rank
    n_rows = -(-(T * TOP_K + N_EXPERTS * (EXPERT_ROWS - 1)) // EXPERT_ROWS) * EXPERT_ROWS
    n_blocks = n_rows // EXPERT_ROWS
    blk_first = jnp.arange(n_blocks, dtype=jnp.int32) * EXPERT_ROWS
    blk_e = jnp.minimum(jnp.sum(ends[None, :] <= blk_first[:, None], axis=1), N_EXPERTS - 1).astype(jnp.int32)
    n_used = (ends[-1:] // EXPERT_ROWS).astype(jnp.int32)

    dest_kt = dest.T
    buf = _sc_scatter_rows(h2.reshape(T, D // 2), dest_kt, n_rows)

    b1 = jnp.concatenate([moe_b1[:, 0::2], moe_b1[:, 1::2]], axis=1)
    ybuf = _experts(blk_e, n_used, buf, moe_w1, b1.reshape(N_EXPERTS, 1, -1), moe_w2, moe_b2.reshape(N_EXPERTS, 1, D))
    nb = B // COMBINE_PARTS if B % COMBINE_PARTS == 0 else B
    dest_parts = dest_kt.reshape(TOP_K, B // nb, nb * S)
    out = None
    for p in range(B // nb):
        yg = _sc_gather_rows(ybuf, dest_parts[:, p].reshape(-1)).reshape(TOP_K, nb, S, D // 2)
        out = _combine(x1, yg, rt, mod3, out, p * nb)
    return out


def kernel(x, c, positions, ada_w, ada_b, norm1_g, w_in, q_norm_g, k_norm_g, cmp_pe_k, cmp_pe_v, cmp_wk1, cmp_wk2, cmp_wv1, cmp_wv2, conv_w, conv_b, lru_wa, lru_ba, lru_wx, lru_bx, lru_lambda, w_branch_out, w_out, norm2_g, router_w, router_b, moe_w1, moe_b1, moe_w2, moe_b2):
    inv = ROPE_THETA ** (-jnp.arange(0, HEAD_DIM, 2, dtype=F32) / HEAD_DIM)
    ang = positions.astype(F32)[..., None] * inv
    ang = jnp.concatenate([ang, ang], axis=-1)
    for l in range(ada_w.shape[0]):
        x = _layer(x, c, ang, ada_w[l], ada_b[l], norm1_g[l], w_in[l], q_norm_g[l], k_norm_g[l], cmp_pe_k[l],
                   cmp_pe_v[l], cmp_wk1[l], cmp_wk2[l], cmp_wv1[l], cmp_wv2[l], conv_w[l], conv_b[l], lru_wa[l],
                   lru_ba[l], lru_wx[l], lru_bx[l], lru_lambda[l], w_branch_out[l], w_out[l], norm2_g[l], router_w[l],
                   router_b[l], moe_w1[l], moe_b1[l], moe_w2[l], moe_b2[l])
    return x
```
